```python
import math
import jax, jax.numpy as jnp
from jax import lax
import numpy as np

D_MODEL = 1024
BATCH = 2
SEQ = 8192
DEPTH = 1

MIX_WIDTH = D_MODEL
DIFF_HEADS = 4
DIFF_HEAD_DIM = 64
DIFF_V_DIM = 2 * DIFF_HEAD_DIM
DIFF_WIDTH = DIFF_HEADS * DIFF_V_DIM
Q_BLOCK = 128
GLA_HEADS = 4
GLA_V_DIM = 128
GLA_K_DIM = GLA_V_DIM // 2
GLA_WIDTH = GLA_HEADS * GLA_V_DIM
GLA_GATE_RANK = 16
GLA_GATE_TEMP = 16.0
GLA_CHUNK = 64
IN_SPLITS = (
    DIFF_HEADS * 2 * DIFF_HEAD_DIM,
    DIFF_HEADS * 2 * DIFF_HEAD_DIM,
    DIFF_HEADS * DIFF_V_DIM,
    GLA_HEADS * GLA_K_DIM,
    GLA_HEADS * GLA_K_DIM,
    GLA_HEADS * GLA_V_DIM,
    GLA_HEADS * GLA_V_DIM,
    GLA_GATE_RANK,
)
IN_WIDTH = sum(IN_SPLITS)
D_FF = int(math.ceil(8 * D_MODEL / 3 / 256) * 256)
EPS = 1e-6

kernel_name = "hybrid_diffattn_gla_parallel_heads"


def rms_norm(x, g):
    xf = x.astype(jnp.float32)
    y = xf * lax.rsqrt(jnp.mean(xf * xf, axis=-1, keepdims=True) + EPS)
    return (y * g.astype(jnp.float32)).astype(x.dtype)


def diff_attention(q, k, v, lam):
    B, H, _, T, d = q.shape
    nb = T // Q_BLOCK
    scale = DIFF_HEAD_DIM ** -0.5
    slopes = 2.0 ** (-8.0 * (jnp.arange(H, dtype=jnp.float32) + 1.0) / H)
    qb = jnp.moveaxis(q.reshape(B, H, 2, nb, Q_BLOCK, d), 3, 0)
    kpos = jnp.arange(T, dtype=jnp.float32)

    def block(args):
        qblk, i = args
        qpos = (i * Q_BLOCK + jnp.arange(Q_BLOCK)).astype(jnp.float32)
        dist = qpos[:, None] - kpos[None, :]
        bias = jnp.where(dist[None] >= 0, -slopes[:, None, None] * dist[None], -jnp.inf)
        s = jnp.einsum('bhjqd,bhjkd->bhjqk', qblk, k).astype(jnp.float32) * scale
        p = jax.nn.softmax(s + bias[None, :, None], axis=-1)
        w = (p[:, :, 0] - lam * p[:, :, 1]).astype(v.dtype)
        return jnp.einsum('bhqk,bhkv->bhqv', w, v)

    out = lax.map(block, (qb, jnp.arange(nb)))
    return out.transpose(1, 0, 3, 2, 4).reshape(B, T, H, v.shape[-1])


def gla_chunked(q, k, v, g):
    B, T, H, dk = q.shape
    dv = v.shape[-1]
    n = T // GLA_CHUNK
    out_dtype = v.dtype

    def to_chunks(a):
        return a.astype(jnp.float32).reshape(B, n, GLA_CHUNK, H, a.shape[-1]).transpose(1, 0, 3, 2, 4)

    qc, kc, vc, gc = map(to_chunks, (q, k, v, g))
    causal = jnp.tril(jnp.ones((GLA_CHUNK, GLA_CHUNK), dtype=bool))

    def step(S, inp):
        qi, ki, vi, gi = inp
        b = jnp.cumsum(gi, axis=-2)
        o_inter = jnp.einsum('bhck,bhkv->bhcv', qi * jnp.exp(b), S)
        diff = b[:, :, :, None, :] - b[:, :, None, :, :]
        decay = jnp.exp(jnp.where(causal[None, None, :, :, None], diff, -jnp.inf))
        A = jnp.einsum('bhtk,bhsk,bhtsk->bhts', qi, ki, decay)
        o_intra = jnp.einsum('bhts,bhsv->bhtv', A, vi)
        b_last = b[:, :, -1:, :]
        S_new = jnp.exp(b_last[:, :, 0, :, None]) * S + jnp.einsum(
            'bhck,bhcv->bhkv', ki * jnp.exp(b_last - b), vi)
        return S_new, o_inter + o_intra

    S0 = jnp.zeros((B, H, dk, dv), jnp.float32)
    _, o = lax.scan(step, S0, (qc, kc, vc, gc))
    return o.transpose(1, 0, 3, 2, 4).reshape(B, T, H, dv).astype(out_dtype)


def setup_inputs(seed: int = 0) -> dict:
    key = jax.random.key(seed)
    ks = jax.random.split(key, 20)
    f32 = jnp.float32
    nrm = lambda k, shape, s: jax.random.normal(k, shape, f32) * s
    L = DEPTH
    return {
        "x": jax.random.normal(ks[0], (BATCH, SEQ, D_MODEL), f32),
        "attn_norm_gain": 1.0 + nrm(ks[1], (L, D_MODEL), 0.02),
        "w_in": nrm(ks[2], (L, D_MODEL, IN_WIDTH), D_MODEL ** -0.5),
        "q_norm_gain": 1.0 + nrm(ks[3], (L, DIFF_HEAD_DIM), 0.02),
        "k_norm_gain": 1.0 + nrm(ks[4], (L, DIFF_HEAD_DIM), 0.02),
        "lambda_q1": nrm(ks[5], (L, DIFF_HEAD_DIM), 0.1),
        "lambda_k1": nrm(ks[6], (L, DIFF_HEAD_DIM), 0.1),
        "lambda_q2": nrm(ks[7], (L, DIFF_HEAD_DIM), 0.1),
        "lambda_k2": nrm(ks[8], (L, DIFF_HEAD_DIM), 0.1),
        "diff_out_norm_gain": 1.0 + nrm(ks[9], (L, DIFF_V_DIM), 0.02),
        "w_gla_gate_up": nrm(ks[10], (L, GLA_GATE_RANK, GLA_HEADS * GLA_K_DIM), GLA_GATE_RANK ** -0.5),
        "b_gla_gate": nrm(ks[11], (L, GLA_HEADS * GLA_K_DIM), 0.01),
        "gla_out_norm_gain": 1.0 + nrm(ks[12], (L, GLA_V_DIM), 0.02),
        "w_out": nrm(ks[13], (L, MIX_WIDTH, D_MODEL), MIX_WIDTH ** -0.5),
        "ffn_norm_gain": 1.0 + nrm(ks[14], (L, D_MODEL), 0.02),
        "w_ffn_gate": nrm(ks[15], (L, D_MODEL, D_FF), D_MODEL ** -0.5),
        "w_ffn_up": nrm(ks[16], (L, D_MODEL, D_FF), D_MODEL ** -0.5),
        "w_ffn_down": nrm(ks[17], (L, D_FF, D_MODEL), D_FF ** -0.5),
    }


def reference(x, attn_norm_gain, w_in, q_norm_gain, k_norm_gain, lambda_q1, lambda_k1,
              lambda_q2, lambda_k2, diff_out_norm_gain, w_gla_gate_up, b_gla_gate,
              gla_out_norm_gain, w_out, ffn_norm_gain, w_ffn_gate, w_ffn_up, w_ffn_down):
    B, T, _ = x.shape
    split_pts = np.cumsum(IN_SPLITS)[:-1].tolist()
    h = x
    for l in range(DEPTH):
        lambda_init = 0.8 - 0.6 * math.exp(-0.3 * l)
        n = rms_norm(h, attn_norm_gain[l])
        proj = n @ w_in[l]
        dq, dk_, dv, gq, gk, gv, gout, glr = jnp.split(proj, split_pts, axis=-1)

        dq = rms_norm(dq.reshape(B, T, DIFF_HEADS, 2, DIFF_HEAD_DIM), q_norm_gain[l])
        dk_ = rms_norm(dk_.reshape(B, T, DIFF_HEADS, 2, DIFF_HEAD_DIM), k_norm_gain[l])
        dq = dq.transpose(0, 2, 3, 1, 4)
        dk_ = dk_.transpose(0, 2, 3, 1, 4)
        dv = dv.reshape(B, T, DIFF_HEADS, DIFF_V_DIM).transpose(0, 2, 1, 3)
        lam = (jnp.exp(jnp.sum(lambda_q1[l].astype(jnp.float32) * lambda_k1[l].astype(jnp.float32)))
               - jnp.exp(jnp.sum(lambda_q2[l].astype(jnp.float32) * lambda_k2[l].astype(jnp.float32)))
               + lambda_init)
        o_diff = diff_attention(dq, dk_, dv, lam)
        o_diff = rms_norm(o_diff, diff_out_norm_gain[l]) * (1.0 - lambda_init)

        gq = gq.reshape(B, T, GLA_HEADS, GLA_K_DIM) * (GLA_K_DIM ** -0.5)
        gk = gk.reshape(B, T, GLA_HEADS, GLA_K_DIM)
        gv = gv.reshape(B, T, GLA_HEADS, GLA_V_DIM)
        log_alpha = jax.nn.log_sigmoid(
            (glr @ w_gla_gate_up[l] + b_gla_gate[l]).astype(jnp.float32)) / GLA_GATE_TEMP
        log_alpha = log_alpha.reshape(B, T, GLA_HEADS, GLA_K_DIM)
        o_gla = gla_chunked(gq, gk, gv, log_alpha)
        o_gla = rms_norm(o_gla, gla_out_norm_gain[l]) * jax.nn.silu(
            gout.reshape(B, T, GLA_HEADS, GLA_V_DIM))

        mixed = jnp.concatenate([o_diff.reshape(B, T, DIFF_WIDTH),
                                 o_gla.reshape(B, T, GLA_WIDTH)], axis=-1)
        h = h + mixed @ w_out[l]

        m = rms_norm(h, ffn_norm_gain[l])
        h = h + (jax.nn.silu(m @ w_ffn_gate[l]) * (m @ w_ffn_up[l])) @ w_ffn_down[l]
    return h
```

```python
import functools
import math

import jax
import jax.numpy as jnp
import numpy as np
from jax import lax
from jax.experimental import pallas as pl
from jax.experimental.pallas import tpu as pltpu

F32 = jnp.float32
BF16 = jnp.bfloat16

D_MODEL = 1024
DIFF_HEADS = 4
DIFF_HEAD_DIM = 64
DIFF_V_DIM = 128
DIFF_WIDTH = DIFF_HEADS * DIFF_V_DIM
GLA_HEADS = 4
GLA_K_DIM = 64
GLA_V_DIM = 128
GLA_WIDTH = GLA_HEADS * GLA_V_DIM
GLA_KW = GLA_HEADS * GLA_K_DIM
GLA_GATE_RANK = 16
GLA_GATE_TEMP = 16.0
GLA_CHUNK = 64
D_FF = 2816
EPS = 1e-6
MAIN_WIDTH = 3072

LANES = 128
ROW_TILE = 512
Q_TILE = 512
K_TILE = 512
GLA_TILE = 512
MXU_DIM = 256
FF_CHUNKS = ((0, 6 * MXU_DIM), (6 * MXU_DIM, D_FF))
ALIBI_SPLIT = 16
NEG_BIG = -1e30
EXP_CLAMP = 80.0
VMEM_LIMIT = 56 * 1024 * 1024


def _rms(x, gain):
    return x * lax.rsqrt(jnp.mean(x * x, axis=-1, keepdims=True) + EPS) * gain


def _proj_kernel(x_ref, ga_ref, w_ref, wlr_ref, gq_ref, gk_ref, cq_ref, ck_ref, eseg_ref,
                 wup_ref, bg_ref,
                 qa_ref, kt_ref, v_ref, gq_out, gk_out, gv_out, go_out, la_out):
    nb = _rms(x_ref[...], ga_ref[...]).astype(BF16)

    def proj(c0, c1):
        return jnp.dot(nb, w_ref[:, c0:c1], preferred_element_type=F32)

    def seg_norm(d, gain):
        ss = jnp.dot((d * d).astype(BF16), eseg_ref[...], preferred_element_type=F32)
        return d * lax.rsqrt(ss * (1.0 / DIFF_HEAD_DIM) + EPS) * gain

    yq = seg_norm(proj(0, 512), gq_ref[...]) * (DIFF_HEAD_DIM ** -0.5)
    yk = seg_norm(proj(512, 1024), gk_ref[...])
    v_ref[...] = proj(1024, 1536).astype(BF16)

    lane = lax.broadcasted_iota(jnp.int32, (ROW_TILE, LANES), 1)
    low = lane < DIFF_HEAD_DIM
    for h in range(DIFF_HEADS):
        qb = yq[:, h * LANES:(h + 1) * LANES]
        kb = yk[:, h * LANES:(h + 1) * LANES]
        cq = cq_ref[h]
        ck = ck_ref[h]
        qa_ref[0, h, 0] = jnp.where(low, qb, cq).astype(BF16)
        qa_ref[0, h, 1] = jnp.where(low, pltpu.roll(qb, DIFF_HEAD_DIM, 1), cq).astype(BF16)
        kt_ref[0, h, 0, 0] = jnp.where(low, kb, ck).T.astype(BF16)
        kt_ref[0, h, 1, 0] = jnp.where(low, pltpu.roll(kb, DIFF_HEAD_DIM, 1), ck).T.astype(BF16)

    gq_out[...] = (proj(1536, 1792) * (GLA_K_DIM ** -0.5)).astype(BF16)
    gk_out[...] = proj(1792, 2048).astype(BF16)
    gv_out[...] = proj(2048, 2560).astype(BF16)
    go_out[...] = proj(2560, 3072).astype(BF16)
    glr = jnp.dot(nb, wlr_ref[...], preferred_element_type=F32)
    z = jnp.dot(glr.astype(BF16), wup_ref[...], preferred_element_type=F32) + bg_ref[...]
    log_sig = jnp.minimum(z, 0.0) - jnp.log(1.0 + jnp.exp(-jnp.abs(z)))
    la_out[...] = log_sig * (1.0 / GLA_GATE_TEMP)


def _const_spec(shape):
    nd = len(shape)
    return pl.BlockSpec(shape, lambda *_: (0,) * nd, pipeline_mode=pl.Buffered(1))


def _proj_call(xf, ga, w_main, w_lr, gq, gk, cq, ck, eseg, wup, bg, batch, seq):
    n_t = seq // ROW_TILE
    n = batch * seq
    row = lambda b, t: (b * n_t + t, 0)
    out_shape = (
        jax.ShapeDtypeStruct((batch, DIFF_HEADS, 2, seq, LANES), BF16),
        jax.ShapeDtypeStruct((batch, DIFF_HEADS, 2, seq // K_TILE, LANES, K_TILE), BF16),
        jax.ShapeDtypeStruct((n, DIFF_WIDTH), BF16),
        jax.ShapeDtypeStruct((n, GLA_KW), BF16),
        jax.ShapeDtypeStruct((n, GLA_KW), BF16),
        jax.ShapeDtypeStruct((n, GLA_WIDTH), BF16),
        jax.ShapeDtypeStruct((n, GLA_WIDTH), BF16),
        jax.ShapeDtypeStruct((n, GLA_KW), F32),
    )
    out_specs = (
        pl.BlockSpec((1, DIFF_HEADS, 2, ROW_TILE, LANES), lambda b, t: (b, 0, 0, t, 0)),
        pl.BlockSpec((1, DIFF_HEADS, 2, 1, LANES, K_TILE), lambda b, t: (b, 0, 0, t, 0, 0)),
        pl.BlockSpec((ROW_TILE, DIFF_WIDTH), row),
        pl.BlockSpec((ROW_TILE, GLA_KW), row),
        pl.BlockSpec((ROW_TILE, GLA_KW), row),
        pl.BlockSpec((ROW_TILE, GLA_WIDTH), row),
        pl.BlockSpec((ROW_TILE, GLA_WIDTH), row),
        pl.BlockSpec((ROW_TILE, GLA_KW), row),
    )
    in_specs = [
        pl.BlockSpec((ROW_TILE, D_MODEL), row),
        _const_spec(ga.shape), _const_spec(w_main.shape), _const_spec(w_lr.shape),
        _const_spec(gq.shape), _const_spec(gk.shape), _const_spec(cq.shape), _const_spec(ck.shape),
        _const_spec(eseg.shape), _const_spec(wup.shape), _const_spec(bg.shape),
    ]
    return pl.pallas_call(
        _proj_kernel, out_shape=out_shape, grid=(batch, n_t),
        in_specs=in_specs, out_specs=out_specs, name="proj",
        compiler_params=pltpu.CompilerParams(
            dimension_semantics=("arbitrary", "arbitrary"), vmem_limit_bytes=VMEM_LIMIT),
    )(xf, ga, w_main, w_lr, gq, gk, cq, ck, eseg, wup, bg)


def _attn_kernel(slope_ref, lq1_ref, lk1_ref, lq2_ref, lk2_ref, gain_ref, q_ref, kt_ref, v_ref,
                 o_ref, m_sc, l_sc, acc_sc, *, lambda_init):
    h = pl.program_id(1)
    qi = pl.program_id(2)
    m_sc[...] = jnp.full(m_sc.shape, NEG_BIG, F32)
    l_sc[...] = jnp.zeros(l_sc.shape, F32)
    acc_sc[...] = jnp.zeros(acc_sc.shape, F32)
    reps = K_TILE // LANES

    def step(kb, masked):
        off = slope_ref[h] * (kb - qi).astype(F32)
        v = v_ref[pl.ds(pl.multiple_of(kb * K_TILE, K_TILE), K_TILE), :]
        for j in range(2):
            s = jnp.dot(q_ref[0, 0, j], kt_ref[0, 0, j, kb], preferred_element_type=F32)
            if masked:
                row = lax.broadcasted_iota(jnp.int32, s.shape, 0)
                col = lax.broadcasted_iota(jnp.int32, s.shape, 1)
                s = jnp.where(col <= row, s, NEG_BIG)
            m_prev = m_sc[j]
            m_new = jnp.maximum(m_prev, jnp.max(s, axis=1, keepdims=True) + off)
            p = jnp.exp(s - jnp.tile(m_new - off, (1, reps)))
            alpha = jnp.exp(m_prev - m_new)
            l_sc[j] = alpha * l_sc[j] + jnp.sum(p, axis=1, keepdims=True)
            acc_sc[j] = alpha * acc_sc[j] + jnp.dot(p.astype(BF16), v, preferred_element_type=F32)
            m_sc[j] = m_new

    def body(kb, carry):
        step(kb, False)
        return carry

    lax.fori_loop(0, qi, body, 0)
    step(qi, True)

    lam = (jnp.exp(jnp.sum(lq1_ref[...] * lk1_ref[...], axis=1, keepdims=True))
           - jnp.exp(jnp.sum(lq2_ref[...] * lk2_ref[...], axis=1, keepdims=True))
           + lambda_init)
    o = acc_sc[0] / l_sc[0] - lam * (acc_sc[1] / l_sc[1])
    o_ref[...] = (_rms(o, gain_ref[...]) * (1.0 - lambda_init)).astype(BF16)


def _attn_call(slope_k, lq1, lk1, lq2, lk2, gain, qa, kt, v, batch, seq, lambda_init):
    n_q = seq // Q_TILE
    n_k = seq // K_TILE
    smem = pl.BlockSpec(memory_space=pltpu.SMEM)
    in_specs = [
        smem,
        _const_spec(lq1.shape), _const_spec(lk1.shape), _const_spec(lq2.shape),
        _const_spec(lk2.shape), _const_spec(gain.shape),
        pl.BlockSpec((1, 1, 2, Q_TILE, LANES), lambda b, h, q: (b, h, 0, q, 0)),
        pl.BlockSpec((1, 1, 2, n_k, LANES, K_TILE), lambda b, h, q: (b, h, 0, 0, 0, 0)),
        pl.BlockSpec((seq, DIFF_V_DIM), lambda b, h, q: (b, h)),
    ]
    return pl.pallas_call(
        functools.partial(_attn_kernel, lambda_init=lambda_init),
        out_shape=jax.ShapeDtypeStruct((batch * seq, DIFF_WIDTH), BF16),
        grid=(batch, DIFF_HEADS, n_q),
        in_specs=in_specs,
        out_specs=pl.BlockSpec((Q_TILE, DIFF_V_DIM), lambda b, h, q: (b * n_q + q, h)),
        scratch_shapes=[pltpu.VMEM((2, Q_TILE, LANES), F32),
                        pltpu.VMEM((2, Q_TILE, LANES), F32),
                        pltpu.VMEM((2, Q_TILE, DIFF_V_DIM), F32)],
        name="diff_attn",
        compiler_params=pltpu.CompilerParams(
            dimension_semantics=("arbitrary", "arbitrary", "arbitrary"),
            vmem_limit_bytes=VMEM_LIMIT),
    )(slope_k, lq1, lk1, lq2, lk2, gain, qa, kt, v)


def _gla_kernel(tril_ref, gain_ref, q_ref, k_ref, v_ref, g_ref, la_ref, o_ref, st_ref):
    @pl.when(pl.program_id(1) == 0)
    def _():
        st_ref[...] = jnp.zeros(st_ref.shape, F32)

    row = lax.broadcasted_iota(jnp.int32, (GLA_CHUNK, GLA_CHUNK), 0)
    col = lax.broadcasted_iota(jnp.int32, (GLA_CHUNK, GLA_CHUNK), 1)
    causal = col <= row
    nt = (((1,), (1,)), ((), ()))

    def chunk(ci, carry):
        rows = pl.ds(pl.multiple_of(ci * GLA_CHUNK, GLA_CHUNK), GLA_CHUNK)
        la = la_ref[rows, :]
        hi = la.astype(BF16)
        r1 = la - hi.astype(F32)
        mid = r1.astype(BF16)
        lo = (r1 - mid.astype(F32)).astype(BF16)
        tril = tril_ref[...]
        b = (jnp.dot(tril, hi, preferred_element_type=F32)
             + jnp.dot(tril, mid, preferred_element_type=F32)
             + jnp.dot(tril, lo, preferred_element_type=F32))
        b_last = b[GLA_CHUNK - 1:GLA_CHUNK, :]
        q = q_ref[rows, :].astype(F32)
        k = k_ref[rows, :].astype(F32)
        q_in = (q * jnp.exp(b)).astype(BF16)
        k_out = (k * jnp.exp(jnp.minimum(-b, EXP_CLAMP))).astype(BF16)
        k_dec = (k * jnp.exp(b_last - b)).astype(BF16)
        d_last = jnp.exp(b_last)
        for h in range(GLA_HEADS):
            ks = slice(h * GLA_K_DIM, (h + 1) * GLA_K_DIM)
            vs = slice(h * GLA_V_DIM, (h + 1) * GLA_V_DIM)
            a = lax.dot_general(q_in[:, ks], k_out[:, ks], nt, preferred_element_type=F32)
            a = jnp.where(causal, a, 0.0).astype(BF16)
            vh = v_ref[rows, vs]
            s_t = st_ref[h]
            o = (jnp.dot(a, vh, preferred_element_type=F32)
                 + lax.dot_general(q_in[:, ks], s_t.astype(BF16), nt, preferred_element_type=F32))
            v_t = vh.astype(F32).T.astype(BF16)
            st_ref[h] = s_t * d_last[:, ks] + jnp.dot(v_t, k_dec[:, ks], preferred_element_type=F32)
            g = g_ref[rows, vs].astype(F32)
            o_ref[rows, vs] = (_rms(o, gain_ref[...]) * (g * jax.nn.sigmoid(g))).astype(BF16)
        return carry

    lax.fori_loop(0, GLA_TILE // GLA_CHUNK, chunk, 0)


def _gla_call(tril, gain, gq, gk, gv, go, la, batch, seq):
    n_t = seq // GLA_TILE
    row = lambda b, t: (b * n_t + t, 0)
    in_specs = [
        _const_spec(tril.shape), _const_spec(gain.shape),
        pl.BlockSpec((GLA_TILE, GLA_KW), row),
        pl.BlockSpec((GLA_TILE, GLA_KW), row),
        pl.BlockSpec((GLA_TILE, GLA_WIDTH), row),
        pl.BlockSpec((GLA_TILE, GLA_WIDTH), row),
        pl.BlockSpec((GLA_TILE, GLA_KW), row),
    ]
    return pl.pallas_call(
        _gla_kernel,
        out_shape=jax.ShapeDtypeStruct((batch * seq, GLA_WIDTH), BF16),
        grid=(batch, n_t),
        in_specs=in_specs,
        out_specs=pl.BlockSpec((GLA_TILE, GLA_WIDTH), row),
        scratch_shapes=[pltpu.VMEM((GLA_HEADS, GLA_V_DIM, GLA_K_DIM), F32)],
        name="gla",
        compiler_params=pltpu.CompilerParams(
            dimension_semantics=("arbitrary", "arbitrary"), vmem_limit_bytes=VMEM_LIMIT),
    )(tril, gain, gq, gk, gv, go, la)


def _ffn_kernel(x_ref, od_ref, og_ref, wod_ref, wog_ref, gf_ref, wg_ref, wu_ref, wd_ref, o_ref):
    h = (x_ref[...]
         + jnp.dot(od_ref[...], wod_ref[...], preferred_element_type=F32)
         + jnp.dot(og_ref[...], wog_ref[...], preferred_element_type=F32))
    m = _rms(h, gf_ref[...]).astype(BF16)
    o_ref[...] = h
    for c0, c1 in FF_CHUNKS:
        gate = jnp.dot(m, wg_ref[:, c0:c1], preferred_element_type=F32)
        up = jnp.dot(m, wu_ref[:, c0:c1], preferred_element_type=F32)
        act = (gate * jax.nn.sigmoid(gate) * up).astype(BF16)
        o_ref[...] += jnp.dot(act, wd_ref[c0:c1, :], preferred_element_type=F32)


def _ffn_call(xf, od, og, wod, wog, gf, wg, wu, wd):
    n = xf.shape[0]
    row = lambda t: (t, 0)
    in_specs = [
        pl.BlockSpec((ROW_TILE, D_MODEL), row),
        pl.BlockSpec((ROW_TILE, DIFF_WIDTH), row),
        pl.BlockSpec((ROW_TILE, GLA_WIDTH), row),
        _const_spec(wod.shape), _const_spec(wog.shape), _const_spec(gf.shape),
        _const_spec(wg.shape), _const_spec(wu.shape), _const_spec(wd.shape),
    ]
    return pl.pallas_call(
        _ffn_kernel,
        out_shape=jax.ShapeDtypeStruct((n, D_MODEL), F32),
        grid=(n // ROW_TILE,),
        in_specs=in_specs,
        out_specs=pl.BlockSpec((ROW_TILE, D_MODEL), row),
        name="out_ffn",
        compiler_params=pltpu.CompilerParams(
            dimension_semantics=("arbitrary",), vmem_limit_bytes=VMEM_LIMIT),
    )(xf, od, og, wod, wog, gf, wg, wu, wd)


def _alibi_tables():
    assert ROW_TILE == Q_TILE == K_TILE
    pos = np.arange(ROW_TILE)
    a = (pos // ALIBI_SPLIT) * ALIBI_SPLIT
    c = pos % ALIBI_SPLIT
    cq = np.zeros((DIFF_HEADS, ROW_TILE, LANES), np.float32)
    ck = np.zeros((DIFF_HEADS, ROW_TILE, LANES), np.float32)
    d = DIFF_HEAD_DIM
    for h in range(DIFF_HEADS):
        slope = 2.0 ** (-8.0 * (h + 1) / DIFF_HEADS)
        cq[h, :, d + 0] = -slope * a
        cq[h, :, d + 1] = -slope * c
        cq[h, :, d + 2] = 1.0
        cq[h, :, d + 3] = 1.0
        ck[h, :, d + 0] = 1.0
        ck[h, :, d + 1] = 1.0
        ck[h, :, d + 2] = slope * a
        ck[h, :, d + 3] = slope * c
    return jnp.asarray(cq), jnp.asarray(ck)


def kernel(x, attn_norm_gain, w_in, q_norm_gain, k_norm_gain, lambda_q1, lambda_k1, lambda_q2,
           lambda_k2, diff_out_norm_gain, w_gla_gate_up, b_gla_gate, gla_out_norm_gain, w_out,
           ffn_norm_gain, w_ffn_gate, w_ffn_up, w_ffn_down):
    batch, seq, _ = x.shape
    depth = w_in.shape[0]
    assert seq % ROW_TILE == 0 and seq % GLA_TILE == 0

    cq, ck = _alibi_tables()
    seg = np.arange(DIFF_WIDTH) // DIFF_HEAD_DIM
    eseg = jnp.asarray((seg[:, None] == seg[None, :]).astype(np.float32), dtype=BF16)
    tril = jnp.asarray(np.tril(np.ones((GLA_CHUNK, GLA_CHUNK), np.float32)), dtype=BF16)
    slope_k = jnp.asarray(
        [2.0 ** (-8.0 * (h + 1) / DIFF_HEADS) * K_TILE for h in range(DIFF_HEADS)], F32)
    n_seg = DIFF_WIDTH // DIFF_HEAD_DIM

    hcur = x.reshape(batch * seq, D_MODEL)
    for l in range(depth):
        lambda_init = 0.8 - 0.6 * math.exp(-0.3 * l)
        w_main = w_in[l][:, :MAIN_WIDTH].astype(BF16)
        w_lr = jnp.pad(w_in[l][:, MAIN_WIDTH:], ((0, 0), (0, LANES - GLA_GATE_RANK))).astype(BF16)
        wup = jnp.pad(w_gla_gate_up[l], ((0, LANES - GLA_GATE_RANK), (0, 0))).astype(BF16)
        row2 = lambda a: a.reshape(1, -1).astype(F32)

        qa, kt, v, gq, gk, gv, go, la = _proj_call(
            hcur, row2(attn_norm_gain[l]), w_main, w_lr,
            row2(jnp.tile(q_norm_gain[l], n_seg)), row2(jnp.tile(k_norm_gain[l], n_seg)),
            cq, ck, eseg, wup, row2(b_gla_gate[l]), batch, seq)

        o_diff = _attn_call(slope_k, row2(lambda_q1[l]), row2(lambda_k1[l]), row2(lambda_q2[l]),
                            row2(lambda_k2[l]), row2(diff_out_norm_gain[l]), qa, kt, v,
                            batch, seq, lambda_init)
        o_gla = _gla_call(tril, row2(gla_out_norm_gain[l]), gq, gk, gv, go, la, batch, seq)

        wo = w_out[l].astype(BF16)
        hcur = _ffn_call(hcur, o_diff, o_gla, wo[:DIFF_WIDTH], wo[DIFF_WIDTH:],
                         row2(ffn_norm_gain[l]), w_ffn_gate[l].astype(BF16),
                         w_ffn_up[l].astype(BF16), w_ffn_down[l].astype(BF16))
    return hcur.reshape(batch, seq, D_MODEL)
```

```python
import functools
import math

import jax
import jax.numpy as jnp
import numpy as np
from jax import lax
from jax.experimental import pallas as pl
from jax.experimental.pallas import tpu as pltpu

F32 = jnp.float32
BF16 = jnp.bfloat16

D_MODEL = 1024
DIFF_HEADS = 4
DIFF_HEAD_DIM = 64
DIFF_V_DIM = 128
DIFF_WIDTH = DIFF_HEADS * DIFF_V_DIM
GLA_HEADS = 4
GLA_K_DIM = 64
GLA_V_DIM = 128
GLA_WIDTH = GLA_HEADS * GLA_V_DIM
GLA_KW = GLA_HEADS * GLA_K_DIM
GLA_GATE_RANK = 16
GLA_GATE_TEMP = 16.0
GLA_CHUNK = 64
D_FF = 2816
EPS = 1e-6
MAIN_WIDTH = 3072

LANES = 128
ROW_TILE = 512
Q_TILE = 512
VT_ROWS = 144
K_TILE = 512
GLA_TILE = 512
MXU_DIM = 256
FF_CHUNKS = ((0, 6 * MXU_DIM), (6 * MXU_DIM, D_FF))
ALIBI_SPLIT = 16
NEG_BIG = -1e30
EXP_CLAMP = 80.0
VMEM_LIMIT = 56 * 1024 * 1024


def _rms(x, gain):
    return x * lax.rsqrt(jnp.mean(x * x, axis=-1, keepdims=True) + EPS) * gain


def _proj_kernel(x_ref, ga_ref, w_ref, wlr_ref, gq_ref, gk_ref, cq_ref, ck_ref, eseg_ref,
                 wup_ref, bg_ref,
                 qt_ref, ka_ref, vt_ref, gq_out, gk_out, gv_out, go_out, la_out):
    nb = _rms(x_ref[...], ga_ref[...]).astype(BF16)

    def proj(c0, c1):
        return jnp.dot(nb, w_ref[:, c0:c1], preferred_element_type=F32)

    def seg_norm(d, gain):
        ss = jnp.dot((d * d).astype(BF16), eseg_ref[...], preferred_element_type=F32)
        return d * lax.rsqrt(ss * (1.0 / DIFF_HEAD_DIM) + EPS) * gain

    yq = seg_norm(proj(0, 512), gq_ref[...]) * (DIFF_HEAD_DIM ** -0.5)
    yk = seg_norm(proj(512, 1024), gk_ref[...])
    dv = proj(1024, 1536)

    lane = lax.broadcasted_iota(jnp.int32, (ROW_TILE, LANES), 1)
    low = lane < DIFF_HEAD_DIM
    ones_rows = (lax.broadcasted_iota(jnp.int32, (VT_ROWS - DIFF_V_DIM, ROW_TILE), 0) == 0)
    for h in range(DIFF_HEADS):
        qb = yq[:, h * LANES:(h + 1) * LANES]
        kb = yk[:, h * LANES:(h + 1) * LANES]
        cq = cq_ref[h]
        ck = ck_ref[h]
        qt_ref[0, h, 0, 0] = jnp.where(low, qb, cq).T.astype(BF16)
        qt_ref[0, h, 1, 0] = jnp.where(low, pltpu.roll(qb, DIFF_HEAD_DIM, 1), cq).T.astype(BF16)
        ka_ref[0, h, 0] = jnp.where(low, kb, ck).astype(BF16)
        ka_ref[0, h, 1] = jnp.where(low, pltpu.roll(kb, DIFF_HEAD_DIM, 1), ck).astype(BF16)
        vt_ref[0, h, 0, :DIFF_V_DIM, :] = dv[:, h * LANES:(h + 1) * LANES].T.astype(BF16)
        vt_ref[0, h, 0, DIFF_V_DIM:, :] = ones_rows.astype(BF16)

    gq_out[...] = (proj(1536, 1792) * (GLA_K_DIM ** -0.5)).astype(BF16)
    gk_out[...] = proj(1792, 2048).astype(BF16)
    gv_out[...] = proj(2048, 2560).astype(BF16)
    go_out[...] = proj(2560, 3072).astype(BF16)
    glr = jnp.dot(nb, wlr_ref[...], preferred_element_type=F32)
    z = jnp.dot(glr.astype(BF16), wup_ref[...], preferred_element_type=F32) + bg_ref[...]
    log_sig = jnp.minimum(z, 0.0) - jnp.log(1.0 + jnp.exp(-jnp.abs(z)))
    la_out[...] = log_sig * (1.0 / GLA_GATE_TEMP)


def _const_spec(shape):
    nd = len(shape)
    return pl.BlockSpec(shape, lambda *_: (0,) * nd, pipeline_mode=pl.Buffered(1))


def _proj_call(xf, ga, w_main, w_lr, gq, gk, cq, ck, eseg, wup, bg, batch, seq):
    n_t = seq // ROW_TILE
    n = batch * seq
    row = lambda b, t: (b * n_t + t, 0)
    out_shape = (
        jax.ShapeDtypeStruct((batch, DIFF_HEADS, 2, seq // Q_TILE, LANES, Q_TILE), BF16),
        jax.ShapeDtypeStruct((batch, DIFF_HEADS, 2, seq, LANES), BF16),
        jax.ShapeDtypeStruct((batch, DIFF_HEADS, seq // K_TILE, VT_ROWS, K_TILE), BF16),
        jax.ShapeDtypeStruct((n, GLA_KW), BF16),
        jax.ShapeDtypeStruct((n, GLA_KW), BF16),
        jax.ShapeDtypeStruct((n, GLA_WIDTH), BF16),
        jax.ShapeDtypeStruct((n, GLA_WIDTH), BF16),
        jax.ShapeDtypeStruct((n, GLA_KW), F32),
    )
    out_specs = (
        pl.BlockSpec((1, DIFF_HEADS, 2, 1, LANES, Q_TILE), lambda b, t: (b, 0, 0, t, 0, 0)),
        pl.BlockSpec((1, DIFF_HEADS, 2, ROW_TILE, LANES), lambda b, t: (b, 0, 0, t, 0)),
        pl.BlockSpec((1, DIFF_HEADS, 1, VT_ROWS, K_TILE), lambda b, t: (b, 0, t, 0, 0)),
        pl.BlockSpec((ROW_TILE, GLA_KW), row),
        pl.BlockSpec((ROW_TILE, GLA_KW), row),
        pl.BlockSpec((ROW_TILE, GLA_WIDTH), row),
        pl.BlockSpec((ROW_TILE, GLA_WIDTH), row),
        pl.BlockSpec((ROW_TILE, GLA_KW), row),
    )
    in_specs = [
        pl.BlockSpec((ROW_TILE, D_MODEL), row),
        _const_spec(ga.shape), _const_spec(w_main.shape), _const_spec(w_lr.shape),
        _const_spec(gq.shape), _const_spec(gk.shape), _const_spec(cq.shape), _const_spec(ck.shape),
        _const_spec(eseg.shape), _const_spec(wup.shape), _const_spec(bg.shape),
    ]
    return pl.pallas_call(
        _proj_kernel, out_shape=out_shape, grid=(batch, n_t),
        in_specs=in_specs, out_specs=out_specs, name="proj",
        compiler_params=pltpu.CompilerParams(
            dimension_semantics=("arbitrary", "arbitrary"), vmem_limit_bytes=VMEM_LIMIT),
    )(xf, ga, w_main, w_lr, gq, gk, cq, ck, eseg, wup, bg)


def _attn_kernel(slope_ref, lq1_ref, lk1_ref, lq2_ref, lk2_ref, gain_ref, qt_ref, k_ref, vt_ref,
                 o_ref, s_sc, cm_sc, m_sc, acc_sc, *, lambda_init):
    h = pl.program_id(1)
    qi = pl.program_id(2)
    m_sc[...] = jnp.full(m_sc.shape, NEG_BIG, F32)
    acc_sc[...] = jnp.zeros(acc_sc.shape, F32)

    def score(j, kb, masked):
        rows = pl.ds(pl.multiple_of(kb * K_TILE, K_TILE), K_TILE)
        s = jnp.dot(k_ref[0, 0, j, rows, :], qt_ref[0, 0, j, 0], preferred_element_type=F32)
        if masked:
            key = lax.broadcasted_iota(jnp.int32, s.shape, 0)
            qry = lax.broadcasted_iota(jnp.int32, s.shape, 1)
            s = jnp.where(key <= qry, s, NEG_BIG)
        s_sc[j] = s
        cm_sc[j] = jnp.max(s, axis=0, keepdims=True)

    def absorb(j, kb):
        off = slope_ref[h] * (kb - qi).astype(F32)
        m_prev = m_sc[j]
        m_new = jnp.maximum(m_prev, cm_sc[j] + off)
        p = jnp.exp(s_sc[j] - (m_new - off))
        alpha = jnp.exp(m_prev - m_new)
        acc_sc[j] = alpha * acc_sc[j] + jnp.dot(vt_ref[0, 0, kb], p.astype(BF16),
                                                preferred_element_type=F32)
        m_sc[j] = m_new

    @pl.when(qi == 0)
    def _():
        score(0, 0, True)

    @pl.when(qi > 0)
    def _():
        score(0, 0, False)

    def body(kb, carry):
        score(1, kb, False)
        absorb(0, kb)
        score(0, kb + 1, False)
        absorb(1, kb)
        return carry

    lax.fori_loop(0, qi - 1, body, 0)

    @pl.when(qi > 0)
    def _():
        score(1, qi - 1, False)
        absorb(0, qi - 1)
        score(0, qi, True)
        absorb(1, qi - 1)

    score(1, qi, True)
    absorb(0, qi)
    absorb(1, qi)

    lam = (jnp.exp(jnp.sum(lq1_ref[...] * lk1_ref[...], axis=1, keepdims=True))
           - jnp.exp(jnp.sum(lq2_ref[...] * lk2_ref[...], axis=1, keepdims=True))
           + lambda_init)
    a1 = acc_sc[0]
    a2 = acc_sc[1]
    o_t = (a1[:DIFF_V_DIM] / a1[DIFF_V_DIM:DIFF_V_DIM + 1]
           - lam * (a2[:DIFF_V_DIM] / a2[DIFF_V_DIM:DIFF_V_DIM + 1]))
    o_ref[...] = (_rms(o_t.T, gain_ref[...]) * (1.0 - lambda_init)).astype(BF16)


def _attn_call(slope_k, lq1, lk1, lq2, lk2, gain, qt, ka, vt, batch, seq, lambda_init):
    n_q = seq // Q_TILE
    n_k = seq // K_TILE
    smem = pl.BlockSpec(memory_space=pltpu.SMEM)
    in_specs = [
        smem,
        _const_spec(lq1.shape), _const_spec(lk1.shape), _const_spec(lq2.shape),
        _const_spec(lk2.shape), _const_spec(gain.shape),
        pl.BlockSpec((1, 1, 2, 1, LANES, Q_TILE), lambda b, h, q: (b, h, 0, q, 0, 0)),
        pl.BlockSpec((1, 1, 2, seq, LANES), lambda b, h, q: (b, h, 0, 0, 0)),
        pl.BlockSpec((1, 1, n_k, VT_ROWS, K_TILE), lambda b, h, q: (b, h, 0, 0, 0)),
    ]
    return pl.pallas_call(
        functools.partial(_attn_kernel, lambda_init=lambda_init),
        out_shape=jax.ShapeDtypeStruct((batch * seq, DIFF_WIDTH), BF16),
        grid=(batch, DIFF_HEADS, n_q),
        in_specs=in_specs,
        out_specs=pl.BlockSpec((Q_TILE, DIFF_V_DIM), lambda b, h, q: (b * n_q + q, h)),
        scratch_shapes=[pltpu.VMEM((2, K_TILE, Q_TILE), F32),
                        pltpu.VMEM((2, 1, Q_TILE), F32),
                        pltpu.VMEM((2, 1, Q_TILE), F32),
                        pltpu.VMEM((2, VT_ROWS, Q_TILE), F32)],
        name="diff_attn",
        compiler_params=pltpu.CompilerParams(
            dimension_semantics=("arbitrary", "arbitrary", "arbitrary"),
            vmem_limit_bytes=VMEM_LIMIT),
    )(slope_k, lq1, lk1, lq2, lk2, gain, qt, ka, vt)


def _gla_kernel(tril_ref, gain_ref, q_ref, k_ref, v_ref, g_ref, la_ref, o_ref, st_ref):
    @pl.when(pl.program_id(1) == 0)
    def _():
        st_ref[...] = jnp.zeros(st_ref.shape, F32)

    row = lax.broadcasted_iota(jnp.int32, (GLA_CHUNK, GLA_CHUNK), 0)
    col = lax.broadcasted_iota(jnp.int32, (GLA_CHUNK, GLA_CHUNK), 1)
    causal = col <= row
    nt = (((1,), (1,)), ((), ()))

    def chunk(ci, carry):
        rows = pl.ds(pl.multiple_of(ci * GLA_CHUNK, GLA_CHUNK), GLA_CHUNK)
        la = la_ref[rows, :]
        hi = la.astype(BF16)
        r1 = la - hi.astype(F32)
        mid = r1.astype(BF16)
        lo = (r1 - mid.astype(F32)).astype(BF16)
        tril = tril_ref[...]
        b = (jnp.dot(tril, hi, preferred_element_type=F32)
             + jnp.dot(tril, mid, preferred_element_type=F32)
             + jnp.dot(tril, lo, preferred_element_type=F32))
        b_last = b[GLA_CHUNK - 1:GLA_CHUNK, :]
        q = q_ref[rows, :].astype(F32)
        k = k_ref[rows, :].astype(F32)
        q_in = (q * jnp.exp(b)).astype(BF16)
        k_out = (k * jnp.exp(jnp.minimum(-b, EXP_CLAMP))).astype(BF16)
        k_dec = (k * jnp.exp(b_last - b)).astype(BF16)
        d_last = jnp.exp(b_last)
        for h in range(GLA_HEADS):
            ks = slice(h * GLA_K_DIM, (h + 1) * GLA_K_DIM)
            vs = slice(h * GLA_V_DIM, (h + 1) * GLA_V_DIM)
            a = lax.dot_general(q_in[:, ks], k_out[:, ks], nt, preferred_element_type=F32)
            a = jnp.where(causal, a, 0.0).astype(BF16)
            vh = v_ref[rows, vs]
            s_t = st_ref[h]
            o = (jnp.dot(a, vh, preferred_element_type=F32)
                 + lax.dot_general(q_in[:, ks], s_t.astype(BF16), nt, preferred_element_type=F32))
            v_t = vh.astype(F32).T.astype(BF16)
            st_ref[h] = s_t * d_last[:, ks] + jnp.dot(v_t, k_dec[:, ks], preferred_element_type=F32)
            g = g_ref[rows, vs].astype(F32)
            o_ref[rows, vs] = (_rms(o, gain_ref[...]) * (g * jax.nn.sigmoid(g))).astype(BF16)
        return carry

    lax.fori_loop(0, GLA_TILE // GLA_CHUNK, chunk, 0)


def _gla_call(tril, gain, gq, gk, gv, go, la, batch, seq):
    n_t = seq // GLA_TILE
    row = lambda b, t: (b * n_t + t, 0)
    in_specs = [
        _const_spec(tril.shape), _const_spec(gain.shape),
        pl.BlockSpec((GLA_TILE, GLA_KW), row),
        pl.BlockSpec((GLA_TILE, GLA_KW), row),
        pl.BlockSpec((GLA_TILE, GLA_WIDTH), row),
        pl.BlockSpec((GLA_TILE, GLA_WIDTH), row),
        pl.BlockSpec((GLA_TILE, GLA_KW), row),
    ]
    return pl.pallas_call(
        _gla_kernel,
        out_shape=jax.ShapeDtypeStruct((batch * seq, GLA_WIDTH), BF16),
        grid=(batch, n_t),
        in_specs=in_specs,
        out_specs=pl.BlockSpec((GLA_TILE, GLA_WIDTH), row),
        scratch_shapes=[pltpu.VMEM((GLA_HEADS, GLA_V_DIM, GLA_K_DIM), F32)],
        name="gla",
        compiler_params=pltpu.CompilerParams(
            dimension_semantics=("arbitrary", "arbitrary"), vmem_limit_bytes=VMEM_LIMIT),
    )(tril, gain, gq, gk, gv, go, la)


def _ffn_kernel(x_ref, od_ref, og_ref, wod_ref, wog_ref, gf_ref, wg_ref, wu_ref, wd_ref, o_ref):
    h = (x_ref[...]
         + jnp.dot(od_ref[...], wod_ref[...], preferred_element_type=F32)
         + jnp.dot(og_ref[...], wog_ref[...], preferred_element_type=F32))
    m = _rms(h, gf_ref[...]).astype(BF16)
    o_ref[...] = h
    for c0, c1 in FF_CHUNKS:
        gate = jnp.dot(m, wg_ref[:, c0:c1], preferred_element_type=F32)
        up = jnp.dot(m, wu_ref[:, c0:c1], preferred_element_type=F32)
        act = (gate * jax.nn.sigmoid(gate) * up).astype(BF16)
        o_ref[...] += jnp.dot(act, wd_ref[c0:c1, :], preferred_element_type=F32)


def _ffn_call(xf, od, og, wod, wog, gf, wg, wu, wd):
    n = xf.shape[0]
    row = lambda t: (t, 0)
    in_specs = [
        pl.BlockSpec((ROW_TILE, D_MODEL), row),
        pl.BlockSpec((ROW_TILE, DIFF_WIDTH), row),
        pl.BlockSpec((ROW_TILE, GLA_WIDTH), row),
        _const_spec(wod.shape), _const_spec(wog.shape), _const_spec(gf.shape),
        _const_spec(wg.shape), _const_spec(wu.shape), _const_spec(wd.shape),
    ]
    return pl.pallas_call(
        _ffn_kernel,
        out_shape=jax.ShapeDtypeStruct((n, D_MODEL), F32),
        grid=(n // ROW_TILE,),
        in_specs=in_specs,
        out_specs=pl.BlockSpec((ROW_TILE, D_MODEL), row),
        name="out_ffn",
        compiler_params=pltpu.CompilerParams(
            dimension_semantics=("arbitrary",), vmem_limit_bytes=VMEM_LIMIT),
    )(xf, od, og, wod, wog, gf, wg, wu, wd)


def _alibi_tables():
    assert ROW_TILE == Q_TILE == K_TILE
    pos = np.arange(ROW_TILE)
    a = (pos // ALIBI_SPLIT) * ALIBI_SPLIT
    c = pos % ALIBI_SPLIT
    cq = np.zeros((DIFF_HEADS, ROW_TILE, LANES), np.float32)
    ck = np.zeros((DIFF_HEADS, ROW_TILE, LANES), np.float32)
    d = DIFF_HEAD_DIM
    for h in range(DIFF_HEADS):
        slope = 2.0 ** (-8.0 * (h + 1) / DIFF_HEADS)
        cq[h, :, d + 0] = -slope * a
        cq[h, :, d + 1] = -slope * c
        cq[h, :, d + 2] = 1.0
        cq[h, :, d + 3] = 1.0
        ck[h, :, d + 0] = 1.0
        ck[h, :, d + 1] = 1.0
        ck[h, :, d + 2] = slope * a
        ck[h, :, d + 3] = slope * c
    return jnp.asarray(cq), jnp.asarray(ck)


def kernel(x, attn_norm_gain, w_in, q_norm_gain, k_norm_gain, lambda_q1, lambda_k1, lambda_q2,
           lambda_k2, diff_out_norm_gain, w_gla_gate_up, b_gla_gate, gla_out_norm_gain, w_out,
           ffn_norm_gain, w_ffn_gate, w_ffn_up, w_ffn_down):
    batch, seq, _ = x.shape
    depth = w_in.shape[0]
    assert seq % ROW_TILE == 0 and seq % GLA_TILE == 0

    cq, ck = _alibi_tables()
    seg = np.arange(DIFF_WIDTH) // DIFF_HEAD_DIM
    eseg = jnp.asarray((seg[:, None] == seg[None, :]).astype(np.float32), dtype=BF16)
    tril = jnp.asarray(np.tril(np.ones((GLA_CHUNK, GLA_CHUNK), np.float32)), dtype=BF16)
    slope_k = jnp.asarray(
        [2.0 ** (-8.0 * (h + 1) / DIFF_HEADS) * K_TILE for h in range(DIFF_HEADS)], F32)
    n_seg = DIFF_WIDTH // DIFF_HEAD_DIM

    hcur = x.reshape(batch * seq, D_MODEL)
    for l in range(depth):
        lambda_init = 0.8 - 0.6 * math.exp(-0.3 * l)
        w_main = w_in[l][:, :MAIN_WIDTH].astype(BF16)
        w_lr = jnp.pad(w_in[l][:, MAIN_WIDTH:], ((0, 0), (0, LANES - GLA_GATE_RANK))).astype(BF16)
        wup = jnp.pad(w_gla_gate_up[l], ((0, LANES - GLA_GATE_RANK), (0, 0))).astype(BF16)
        row2 = lambda a: a.reshape(1, -1).astype(F32)

        qt, ka, vt, gq, gk, gv, go, la = _proj_call(
            hcur, row2(attn_norm_gain[l]), w_main, w_lr,
            row2(jnp.tile(q_norm_gain[l], n_seg)), row2(jnp.tile(k_norm_gain[l], n_seg)),
            cq, ck, eseg, wup, row2(b_gla_gate[l]), batch, seq)

        o_diff = _attn_call(slope_k, row2(lambda_q1[l]), row2(lambda_k1[l]), row2(lambda_q2[l]),
                            row2(lambda_k2[l]), row2(diff_out_norm_gain[l]), qt, ka, vt,
                            batch, seq, lambda_init)
        o_gla = _gla_call(tril, row2(gla_out_norm_gain[l]), gq, gk, gv, go, la, batch, seq)

        wo = w_out[l].astype(BF16)
        hcur = _ffn_call(hcur, o_diff, o_gla, wo[:DIFF_WIDTH], wo[DIFF_WIDTH:],
                         row2(ffn_norm_gain[l]), w_ffn_gate[l].astype(BF16),
                         w_ffn_up[l].astype(BF16), w_ffn_down[l].astype(BF16))
    return hcur.reshape(batch, seq, D_MODEL)
```

```python
import functools
import math

import jax
import jax.numpy as jnp
import numpy as np
from jax import lax
from jax.experimental import pallas as pl
from jax.experimental.pallas import tpu as pltpu

F32 = jnp.float32
BF16 = jnp.bfloat16

D_MODEL = 1024
DIFF_HEADS = 4
DIFF_HEAD_DIM = 64
DIFF_V_DIM = 128
DIFF_WIDTH = DIFF_HEADS * DIFF_V_DIM
GLA_HEADS = 4
GLA_K_DIM = 64
GLA_V_DIM = 128
GLA_WIDTH = GLA_HEADS * GLA_V_DIM
GLA_KW = GLA_HEADS * GLA_K_DIM
GLA_GATE_RANK = 16
GLA_GATE_TEMP = 16.0
GLA_CHUNK = 64
D_FF = 2816
EPS = 1e-6
MAIN_WIDTH = 3072

LANES = 128
ROW_TILE = 512
Q_TILE = 512
VT_ROWS = 144
K_TILE = 512
LOOP_UNROLL = 4
GLA_TILE = 512
MXU_DIM = 256
FF_CHUNKS = ((0, 6 * MXU_DIM), (6 * MXU_DIM, D_FF))
ALIBI_SPLIT = 16
NEG_BIG = -1e30
EXP_CLAMP = 80.0
VMEM_LIMIT = 56 * 1024 * 1024


def _rms(x, gain):
    return x * lax.rsqrt(jnp.mean(x * x, axis=-1, keepdims=True) + EPS) * gain


def _proj_kernel(x_ref, ga_ref, w_ref, wlr_ref, gq_ref, gk_ref, cq_ref, ck_ref, eseg_ref,
                 wup_ref, bg_ref,
                 qt_ref, ka_ref, vt_ref, gq_out, gk_out, gv_out, go_out, la_out):
    nb = _rms(x_ref[...], ga_ref[...]).astype(BF16)

    def proj(c0, c1):
        return jnp.dot(nb, w_ref[:, c0:c1], preferred_element_type=F32)

    def seg_norm(d, gain):
        ss = jnp.dot((d * d).astype(BF16), eseg_ref[...], preferred_element_type=F32)
        return d * lax.rsqrt(ss * (1.0 / DIFF_HEAD_DIM) + EPS) * gain

    yq = seg_norm(proj(0, 512), gq_ref[...]) * (DIFF_HEAD_DIM ** -0.5)
    yk = seg_norm(proj(512, 1024), gk_ref[...])
    dv = proj(1024, 1536)

    lane = lax.broadcasted_iota(jnp.int32, (ROW_TILE, LANES), 1)
    low = lane < DIFF_HEAD_DIM
    ones_rows = (lax.broadcasted_iota(jnp.int32, (VT_ROWS - DIFF_V_DIM, ROW_TILE), 0) == 0)
    for h in range(DIFF_HEADS):
        qb = yq[:, h * LANES:(h + 1) * LANES]
        kb = yk[:, h * LANES:(h + 1) * LANES]
        cq = cq_ref[h]
        ck = ck_ref[h]
        qt_ref[0, h, 0, 0] = jnp.where(low, qb, cq).T.astype(BF16)
        qt_ref[0, h, 1, 0] = jnp.where(low, pltpu.roll(qb, DIFF_HEAD_DIM, 1), cq).T.astype(BF16)
        ka_ref[0, h, 0] = jnp.where(low, kb, ck).astype(BF16)
        ka_ref[0, h, 1] = jnp.where(low, pltpu.roll(kb, DIFF_HEAD_DIM, 1), ck).astype(BF16)
        vt_ref[0, h, 0, :DIFF_V_DIM, :] = dv[:, h * LANES:(h + 1) * LANES].T.astype(BF16)
        vt_ref[0, h, 0, DIFF_V_DIM:, :] = ones_rows.astype(BF16)

    gq_out[...] = (proj(1536, 1792) * (GLA_K_DIM ** -0.5)).astype(BF16)
    gk_out[...] = proj(1792, 2048).astype(BF16)
    gv_out[...] = proj(2048, 2560).astype(BF16)
    go_out[...] = proj(2560, 3072).astype(BF16)
    glr = jnp.dot(nb, wlr_ref[...], preferred_element_type=F32)
    z = jnp.dot(glr.astype(BF16), wup_ref[...], preferred_element_type=F32) + bg_ref[...]
    log_sig = jnp.minimum(z, 0.0) - jnp.log(1.0 + jnp.exp(-jnp.abs(z)))
    la_out[...] = log_sig * (1.0 / GLA_GATE_TEMP)


def _const_spec(shape):
    nd = len(shape)
    return pl.BlockSpec(shape, lambda *_: (0,) * nd, pipeline_mode=pl.Buffered(1))


def _proj_call(xf, ga, w_main, w_lr, gq, gk, cq, ck, eseg, wup, bg, batch, seq):
    n_t = seq // ROW_TILE
    n = batch * seq
    row = lambda b, t: (b * n_t + t, 0)
    out_shape = (
        jax.ShapeDtypeStruct((batch, DIFF_HEADS, 2, seq // Q_TILE, LANES, Q_TILE), BF16),
        jax.ShapeDtypeStruct((batch, DIFF_HEADS, 2, seq, LANES), BF16),
        jax.ShapeDtypeStruct((batch, DIFF_HEADS, seq // K_TILE, VT_ROWS, K_TILE), BF16),
        jax.ShapeDtypeStruct((n, GLA_KW), BF16),
        jax.ShapeDtypeStruct((n, GLA_KW), BF16),
        jax.ShapeDtypeStruct((n, GLA_WIDTH), BF16),
        jax.ShapeDtypeStruct((n, GLA_WIDTH), BF16),
        jax.ShapeDtypeStruct((n, GLA_KW), F32),
    )
    out_specs = (
        pl.BlockSpec((1, DIFF_HEADS, 2, 1, LANES, Q_TILE), lambda b, t: (b, 0, 0, t, 0, 0)),
        pl.BlockSpec((1, DIFF_HEADS, 2, ROW_TILE, LANES), lambda b, t: (b, 0, 0, t, 0)),
        pl.BlockSpec((1, DIFF_HEADS, 1, VT_ROWS, K_TILE), lambda b, t: (b, 0, t, 0, 0)),
        pl.BlockSpec((ROW_TILE, GLA_KW), row),
        pl.BlockSpec((ROW_TILE, GLA_KW), row),
        pl.BlockSpec((ROW_TILE, GLA_WIDTH), row),
        pl.BlockSpec((ROW_TILE, GLA_WIDTH), row),
        pl.BlockSpec((ROW_TILE, GLA_KW), row),
    )
    in_specs = [
        pl.BlockSpec((ROW_TILE, D_MODEL), row),
        _const_spec(ga.shape), _const_spec(w_main.shape), _const_spec(w_lr.shape),
        _const_spec(gq.shape), _const_spec(gk.shape), _const_spec(cq.shape), _const_spec(ck.shape),
        _const_spec(eseg.shape), _const_spec(wup.shape), _const_spec(bg.shape),
    ]
    return pl.pallas_call(
        _proj_kernel, out_shape=out_shape, grid=(batch, n_t),
        in_specs=in_specs, out_specs=out_specs, name="proj",
        compiler_params=pltpu.CompilerParams(
            dimension_semantics=("arbitrary", "arbitrary"), vmem_limit_bytes=VMEM_LIMIT),
    )(xf, ga, w_main, w_lr, gq, gk, cq, ck, eseg, wup, bg)


def _attn_kernel(slope_ref, lq1_ref, lk1_ref, lq2_ref, lk2_ref, gain_ref, qt_ref, k_ref, vt_ref,
                 o_ref, s_sc, cm_sc, m_sc, acc_sc, *, lambda_init):
    h = pl.program_id(1)
    qi = pl.program_id(2)
    m_sc[...] = jnp.full(m_sc.shape, NEG_BIG, F32)
    acc_sc[...] = jnp.zeros(acc_sc.shape, F32)

    def score(j, kb, masked):
        rows = pl.ds(pl.multiple_of(kb * K_TILE, K_TILE), K_TILE)
        s = jnp.dot(k_ref[0, 0, j, rows, :], qt_ref[0, 0, j, 0], preferred_element_type=F32)
        if masked:
            key = lax.broadcasted_iota(jnp.int32, s.shape, 0)
            qry = lax.broadcasted_iota(jnp.int32, s.shape, 1)
            s = jnp.where(key <= qry, s, NEG_BIG)
        s_sc[j] = s
        cm_sc[j] = jnp.max(s, axis=0, keepdims=True)

    def absorb(j, kb):
        off = slope_ref[h] * (kb - qi).astype(F32)
        m_prev = m_sc[j]
        m_new = jnp.maximum(m_prev, cm_sc[j] + off)
        p = jnp.exp(s_sc[j] - (m_new - off))
        alpha = jnp.exp(m_prev - m_new)
        acc_sc[j] = alpha * acc_sc[j] + jnp.dot(vt_ref[0, 0, kb], p.astype(BF16),
                                                preferred_element_type=F32)
        m_sc[j] = m_new

    @pl.when(qi > 0)
    def _():
        score(0, 0, False)

    def unit(kb):
        score(1, kb, False)
        absorb(0, kb)
        score(0, kb + 1, False)
        absorb(1, kb)

    def body(kp, carry):
        for u in range(LOOP_UNROLL):
            unit(LOOP_UNROLL * kp + u)
        return carry

    n_full = jnp.maximum(qi - 1, 0)
    lax.fori_loop(0, n_full // LOOP_UNROLL, body, 0)

    done = (n_full // LOOP_UNROLL) * LOOP_UNROLL
    size = LOOP_UNROLL // 2
    while size >= 1:
        take = (n_full - done) >= size

        @pl.when(take)
        def _(done=done, size=size):
            for u in range(size):
                unit(done + u)

        done = done + jnp.where(take, size, 0)
        size //= 2

    def finish():
        score(1, qi, True)
        absorb(0, qi)
        absorb(1, qi)
        lam = (jnp.exp(jnp.sum(lq1_ref[...] * lk1_ref[...], axis=1, keepdims=True))
               - jnp.exp(jnp.sum(lq2_ref[...] * lk2_ref[...], axis=1, keepdims=True))
               + lambda_init)
        a1 = acc_sc[0]
        a2 = acc_sc[1]
        o_t = (a1[:DIFF_V_DIM] / a1[DIFF_V_DIM:DIFF_V_DIM + 1]
               - lam * (a2[:DIFF_V_DIM] / a2[DIFF_V_DIM:DIFF_V_DIM + 1]))
        o_ref[...] = (_rms(o_t.T, gain_ref[...]) * (1.0 - lambda_init)).astype(BF16)

    @pl.when(qi > 0)
    def _():
        score(1, qi - 1, False)
        absorb(0, qi - 1)
        score(0, qi, True)
        absorb(1, qi - 1)
        finish()

    @pl.when(qi == 0)
    def _():
        score(0, 0, True)
        finish()


def _attn_call(slope_k, lq1, lk1, lq2, lk2, gain, qt, ka, vt, batch, seq, lambda_init):
    n_q = seq // Q_TILE
    n_k = seq // K_TILE
    smem = pl.BlockSpec(memory_space=pltpu.SMEM)
    in_specs = [
        smem,
        _const_spec(lq1.shape), _const_spec(lk1.shape), _const_spec(lq2.shape),
        _const_spec(lk2.shape), _const_spec(gain.shape),
        pl.BlockSpec((1, 1, 2, 1, LANES, Q_TILE), lambda b, h, q: (b, h, 0, q, 0, 0)),
        pl.BlockSpec((1, 1, 2, seq, LANES), lambda b, h, q: (b, h, 0, 0, 0)),
        pl.BlockSpec((1, 1, n_k, VT_ROWS, K_TILE), lambda b, h, q: (b, h, 0, 0, 0)),
    ]
    return pl.pallas_call(
        functools.partial(_attn_kernel, lambda_init=lambda_init),
        out_shape=jax.ShapeDtypeStruct((batch * seq, DIFF_WIDTH), BF16),
        grid=(batch, DIFF_HEADS, n_q),
        in_specs=in_specs,
        out_specs=pl.BlockSpec((Q_TILE, DIFF_V_DIM), lambda b, h, q: (b * n_q + q, h)),
        scratch_shapes=[pltpu.VMEM((2, K_TILE, Q_TILE), F32),
                        pltpu.VMEM((2, 1, Q_TILE), F32),
                        pltpu.VMEM((2, 1, Q_TILE), F32),
                        pltpu.VMEM((2, VT_ROWS, Q_TILE), F32)],
        name="diff_attn",
        compiler_params=pltpu.CompilerParams(
            dimension_semantics=("arbitrary", "arbitrary", "arbitrary"),
            vmem_limit_bytes=VMEM_LIMIT),
    )(slope_k, lq1, lk1, lq2, lk2, gain, qt, ka, vt)


def _gla_kernel(tril_ref, gain_ref, q_ref, k_ref, v_ref, g_ref, la_ref, o_ref, st_ref):
    @pl.when(pl.program_id(1) == 0)
    def _():
        st_ref[...] = jnp.zeros(st_ref.shape, F32)

    row = lax.broadcasted_iota(jnp.int32, (GLA_CHUNK, GLA_CHUNK), 0)
    col = lax.broadcasted_iota(jnp.int32, (GLA_CHUNK, GLA_CHUNK), 1)
    causal = col <= row
    nt = (((1,), (1,)), ((), ()))

    def chunk(ci, carry):
        rows = pl.ds(pl.multiple_of(ci * GLA_CHUNK, GLA_CHUNK), GLA_CHUNK)
        la = la_ref[rows, :]
        hi = la.astype(BF16)
        r1 = la - hi.astype(F32)
        mid = r1.astype(BF16)
        lo = (r1 - mid.astype(F32)).astype(BF16)
        tril = tril_ref[...]
        b = (jnp.dot(tril, hi, preferred_element_type=F32)
             + jnp.dot(tril, mid, preferred_element_type=F32)
             + jnp.dot(tril, lo, preferred_element_type=F32))
        b_last = b[GLA_CHUNK - 1:GLA_CHUNK, :]
        q = q_ref[rows, :].astype(F32)
        k = k_ref[rows, :].astype(F32)
        q_in = (q * jnp.exp(b)).astype(BF16)
        k_out = (k * jnp.exp(jnp.minimum(-b, EXP_CLAMP))).astype(BF16)
        k_dec = (k * jnp.exp(b_last - b)).astype(BF16)
        d_last = jnp.exp(b_last)
        for h in range(GLA_HEADS):
            ks = slice(h * GLA_K_DIM, (h + 1) * GLA_K_DIM)
            vs = slice(h * GLA_V_DIM, (h + 1) * GLA_V_DIM)
            a = lax.dot_general(q_in[:, ks], k_out[:, ks], nt, preferred_element_type=F32)
            a = jnp.where(causal, a, 0.0).astype(BF16)
            vh = v_ref[rows, vs]
            s_t = st_ref[h]
            o = (jnp.dot(a, vh, preferred_element_type=F32)
                 + lax.dot_general(q_in[:, ks], s_t.astype(BF16), nt, preferred_element_type=F32))
            v_t = vh.astype(F32).T.astype(BF16)
            st_ref[h] = s_t * d_last[:, ks] + jnp.dot(v_t, k_dec[:, ks], preferred_element_type=F32)
            g = g_ref[rows, vs].astype(F32)
            o_ref[rows, vs] = (_rms(o, gain_ref[...]) * (g * jax.nn.sigmoid(g))).astype(BF16)
        return carry

    lax.fori_loop(0, GLA_TILE // GLA_CHUNK, chunk, 0)


def _gla_call(tril, gain, gq, gk, gv, go, la, batch, seq):
    n_t = seq // GLA_TILE
    row = lambda b, t: (b * n_t + t, 0)
    in_specs = [
        _const_spec(tril.shape), _const_spec(gain.shape),
        pl.BlockSpec((GLA_TILE, GLA_KW), row),
        pl.BlockSpec((GLA_TILE, GLA_KW), row),
        pl.BlockSpec((GLA_TILE, GLA_WIDTH), row),
        pl.BlockSpec((GLA_TILE, GLA_WIDTH), row),
        pl.BlockSpec((GLA_TILE, GLA_KW), row),
    ]
    return pl.pallas_call(
        _gla_kernel,
        out_shape=jax.ShapeDtypeStruct((batch * seq, GLA_WIDTH), BF16),
        grid=(batch, n_t),
        in_specs=in_specs,
        out_specs=pl.BlockSpec((GLA_TILE, GLA_WIDTH), row),
        scratch_shapes=[pltpu.VMEM((GLA_HEADS, GLA_V_DIM, GLA_K_DIM), F32)],
        name="gla",
        compiler_params=pltpu.CompilerParams(
            dimension_semantics=("arbitrary", "arbitrary"), vmem_limit_bytes=VMEM_LIMIT),
    )(tril, gain, gq, gk, gv, go, la)


def _ffn_kernel(x_ref, od_ref, og_ref, wod_ref, wog_ref, gf_ref, wg_ref, wu_ref, wd_ref, o_ref):
    h = (x_ref[...]
         + jnp.dot(od_ref[...], wod_ref[...], preferred_element_type=F32)
         + jnp.dot(og_ref[...], wog_ref[...], preferred_element_type=F32))
    m = _rms(h, gf_ref[...]).astype(BF16)
    o_ref[...] = h
    for c0, c1 in FF_CHUNKS:
        gate = jnp.dot(m, wg_ref[:, c0:c1], preferred_element_type=F32)
        up = jnp.dot(m, wu_ref[:, c0:c1], preferred_element_type=F32)
        act = (gate * jax.nn.sigmoid(gate) * up).astype(BF16)
        o_ref[...] += jnp.dot(act, wd_ref[c0:c1, :], preferred_element_type=F32)


def _ffn_call(xf, od, og, wod, wog, gf, wg, wu, wd):
    n = xf.shape[0]
    row = lambda t: (t, 0)
    in_specs = [
        pl.BlockSpec((ROW_TILE, D_MODEL), row),
        pl.BlockSpec((ROW_TILE, DIFF_WIDTH), row),
        pl.BlockSpec((ROW_TILE, GLA_WIDTH), row),
        _const_spec(wod.shape), _const_spec(wog.shape), _const_spec(gf.shape),
        _const_spec(wg.shape), _const_spec(wu.shape), _const_spec(wd.shape),
    ]
    return pl.pallas_call(
        _ffn_kernel,
        out_shape=jax.ShapeDtypeStruct((n, D_MODEL), F32),
        grid=(n // ROW_TILE,),
        in_specs=in_specs,
        out_specs=pl.BlockSpec((ROW_TILE, D_MODEL), row),
        name="out_ffn",
        compiler_params=pltpu.CompilerParams(
            dimension_semantics=("arbitrary",), vmem_limit_bytes=VMEM_LIMIT),
    )(xf, od, og, wod, wog, gf, wg, wu, wd)


def _alibi_tables():
    assert ROW_TILE == Q_TILE == K_TILE
    pos = np.arange(ROW_TILE)
    a = (pos // ALIBI_SPLIT) * ALIBI_SPLIT
    c = pos % ALIBI_SPLIT
    cq = np.zeros((DIFF_HEADS, ROW_TILE, LANES), np.float32)
    ck = np.zeros((DIFF_HEADS, ROW_TILE, LANES), np.float32)
    d = DIFF_HEAD_DIM
    for h in range(DIFF_HEADS):
        slope = 2.0 ** (-8.0 * (h + 1) / DIFF_HEADS)
        cq[h, :, d + 0] = -slope * a
        cq[h, :, d + 1] = -slope * c
        cq[h, :, d + 2] = 1.0
        cq[h, :, d + 3] = 1.0
        ck[h, :, d + 0] = 1.0
        ck[h, :, d + 1] = 1.0
        ck[h, :, d + 2] = slope * a
        ck[h, :, d + 3] = slope * c
    return jnp.asarray(cq), jnp.asarray(ck)


def kernel(x, attn_norm_gain, w_in, q_norm_gain, k_norm_gain, lambda_q1, lambda_k1, lambda_q2,
           lambda_k2, diff_out_norm_gain, w_gla_gate_up, b_gla_gate, gla_out_norm_gain, w_out,
           ffn_norm_gain, w_ffn_gate, w_ffn_up, w_ffn_down):
    batch, seq, _ = x.shape
    depth = w_in.shape[0]
    assert seq % ROW_TILE == 0 and seq % GLA_TILE == 0

    cq, ck = _alibi_tables()
    seg = np.arange(DIFF_WIDTH) // DIFF_HEAD_DIM
    eseg = jnp.asarray((seg[:, None] == seg[None, :]).astype(np.float32), dtype=BF16)
    tril = jnp.asarray(np.tril(np.ones((GLA_CHUNK, GLA_CHUNK), np.float32)), dtype=BF16)
    slope_k = jnp.asarray(
        [2.0 ** (-8.0 * (h + 1) / DIFF_HEADS) * K_TILE for h in range(DIFF_HEADS)], F32)
    n_seg = DIFF_WIDTH // DIFF_HEAD_DIM

    hcur = x.reshape(batch * seq, D_MODEL)
    for l in range(depth):
        lambda_init = 0.8 - 0.6 * math.exp(-0.3 * l)
        w_main = w_in[l][:, :MAIN_WIDTH].astype(BF16)
        w_lr = jnp.pad(w_in[l][:, MAIN_WIDTH:], ((0, 0), (0, LANES - GLA_GATE_RANK))).astype(BF16)
        wup = jnp.pad(w_gla_gate_up[l], ((0, LANES - GLA_GATE_RANK), (0, 0))).astype(BF16)
        row2 = lambda a: a.reshape(1, -1).astype(F32)

        qt, ka, vt, gq, gk, gv, go, la = _proj_call(
            hcur, row2(attn_norm_gain[l]), w_main, w_lr,
            row2(jnp.tile(q_norm_gain[l], n_seg)), row2(jnp.tile(k_norm_gain[l], n_seg)),
            cq, ck, eseg, wup, row2(b_gla_gate[l]), batch, seq)

        o_diff = _attn_call(slope_k, row2(lambda_q1[l]), row2(lambda_k1[l]), row2(lambda_q2[l]),
                            row2(lambda_k2[l]), row2(diff_out_norm_gain[l]), qt, ka, vt,
                            batch, seq, lambda_init)
        o_gla = _gla_call(tril, row2(gla_out_norm_gain[l]), gq, gk, gv, go, la, batch, seq)

        wo = w_out[l].astype(BF16)
        hcur = _ffn_call(hcur, o_diff, o_gla, wo[:DIFF_WIDTH], wo[DIFF_WIDTH:],
                         row2(ffn_norm_gain[l]), w_ffn_gate[l].astype(BF16),
                         w_ffn_up[l].astype(BF16), w_ffn_down[l].astype(BF16))
    return hcur.reshape(batch, seq, D_MODEL)
```

```python
import functools
import math

import jax
import jax.numpy as jnp
import numpy as np
from jax import lax
from jax.experimental import pallas as pl
from jax.experimental.pallas import tpu as pltpu

F32 = jnp.float32
BF16 = jnp.bfloat16

D_MODEL = 1024
DIFF_HEADS = 4
DIFF_HEAD_DIM = 64
DIFF_V_DIM = 128
DIFF_WIDTH = DIFF_HEADS * DIFF_V_DIM
GLA_HEADS = 4
GLA_K_DIM = 64
GLA_V_DIM = 128
GLA_WIDTH = GLA_HEADS * GLA_V_DIM
GLA_KW = GLA_HEADS * GLA_K_DIM
GLA_GATE_RANK = 16
GLA_GATE_TEMP = 16.0
D_FF = 2816
EPS = 1e-6
MAIN_WIDTH = 3072
IN_WIDTH = MAIN_WIDTH + GLA_GATE_RANK

LANES = 128
ROW_TILE = 512
Q_TILE = 512
VT_ROWS = 144
K_TILE = 512
LOOP_UNROLL = 4
GLA_TILE = 512
GLA_CHUNK = 256
MXU_DIM = 256
FF_CHUNKS = ((0, 6 * MXU_DIM), (6 * MXU_DIM, D_FF))
ALIBI_SPLIT = 16
NEG_BIG = -1e30
EXP_CLAMP = 80.0
VMEM_LIMIT = 56 * 1024 * 1024


def _rms(x, gain):
    return x * lax.rsqrt(jnp.mean(x * x, axis=-1, keepdims=True) + EPS) * gain


def _proj_kernel(x_ref, ga_ref, w_ref, gq_ref, gk_ref, cq_ref, ck_ref, eseg_ref,
                 wup_ref, bg_ref,
                 qt_ref, ka_ref, vt_ref, gq_out, gk_out, gv_out, go_out, la_out):
    nb = _rms(x_ref[...], ga_ref[...]).astype(BF16)

    def proj(c0, c1):
        return jnp.dot(nb, w_ref[:, c0:c1], preferred_element_type=F32)

    def seg_norm(d, gain):
        ss = jnp.dot((d * d).astype(BF16), eseg_ref[...], preferred_element_type=F32)
        return d * lax.rsqrt(ss * (1.0 / DIFF_HEAD_DIM) + EPS) * gain

    yq = seg_norm(proj(0, 512), gq_ref[...]) * (DIFF_HEAD_DIM ** -0.5)
    yk = seg_norm(proj(512, 1024), gk_ref[...])
    dv = proj(1024, 1536)

    lane = lax.broadcasted_iota(jnp.int32, (ROW_TILE, LANES), 1)
    low = lane < DIFF_HEAD_DIM
    ones_rows = (lax.broadcasted_iota(jnp.int32, (VT_ROWS - DIFF_V_DIM, ROW_TILE), 0) == 0)
    for h in range(DIFF_HEADS):
        qb = yq[:, h * LANES:(h + 1) * LANES]
        kb = yk[:, h * LANES:(h + 1) * LANES]
        cq = cq_ref[h]
        ck = ck_ref[h]
        qt_ref[0, h, 0, 0] = jnp.where(low, qb, cq).T.astype(BF16)
        qt_ref[0, h, 1, 0] = jnp.where(low, pltpu.roll(qb, DIFF_HEAD_DIM, 1), cq).T.astype(BF16)
        ka_ref[0, h, 0] = jnp.where(low, kb, ck).astype(BF16)
        ka_ref[0, h, 1] = jnp.where(low, pltpu.roll(kb, DIFF_HEAD_DIM, 1), ck).astype(BF16)
        vt_ref[0, h, 0, :DIFF_V_DIM, :] = dv[:, h * LANES:(h + 1) * LANES].T.astype(BF16)
        vt_ref[0, h, 0, DIFF_V_DIM:, :] = ones_rows.astype(BF16)

    gq_out[...] = (proj(1536, 1792) * (GLA_K_DIM ** -0.5)).astype(BF16)
    gk_out[...] = proj(1792, 2048).astype(BF16)
    gv_out[...] = proj(2048, 2560).astype(BF16)
    go_out[...] = proj(2560, 3072).astype(BF16)
    glr = proj(MAIN_WIDTH, IN_WIDTH)
    z = jnp.dot(glr.astype(BF16), wup_ref[...], preferred_element_type=F32) + bg_ref[...]
    log_sig = jnp.minimum(z, 0.0) - jnp.log(1.0 + jnp.exp(-jnp.abs(z)))
    la_out[...] = log_sig * (1.0 / GLA_GATE_TEMP)


def _const_spec(shape):
    nd = len(shape)
    return pl.BlockSpec(shape, lambda *_: (0,) * nd, pipeline_mode=pl.Buffered(1))


def _proj_call(xf, ga, w_main, gq, gk, cq, ck, eseg, wup, bg, batch, seq):
    n_t = seq // ROW_TILE
    n = batch * seq
    row = lambda b, t: (b * n_t + t, 0)
    out_shape = (
        jax.ShapeDtypeStruct((batch, DIFF_HEADS, 2, seq // Q_TILE, LANES, Q_TILE), BF16),
        jax.ShapeDtypeStruct((batch, DIFF_HEADS, 2, seq, LANES), BF16),
        jax.ShapeDtypeStruct((batch, DIFF_HEADS, seq // K_TILE, VT_ROWS, K_TILE), BF16),
        jax.ShapeDtypeStruct((n, GLA_KW), BF16),
        jax.ShapeDtypeStruct((n, GLA_KW), BF16),
        jax.ShapeDtypeStruct((n, GLA_WIDTH), BF16),
        jax.ShapeDtypeStruct((n, GLA_WIDTH), BF16),
        jax.ShapeDtypeStruct((n, GLA_KW), F32),
    )
    out_specs = (
        pl.BlockSpec((1, DIFF_HEADS, 2, 1, LANES, Q_TILE), lambda b, t: (b, 0, 0, t, 0, 0)),
        pl.BlockSpec((1, DIFF_HEADS, 2, ROW_TILE, LANES), lambda b, t: (b, 0, 0, t, 0)),
        pl.BlockSpec((1, DIFF_HEADS, 1, VT_ROWS, K_TILE), lambda b, t: (b, 0, t, 0, 0)),
        pl.BlockSpec((ROW_TILE, GLA_KW), row),
        pl.BlockSpec((ROW_TILE, GLA_KW), row),
        pl.BlockSpec((ROW_TILE, GLA_WIDTH), row),
        pl.BlockSpec((ROW_TILE, GLA_WIDTH), row),
        pl.BlockSpec((ROW_TILE, GLA_KW), row),
    )
    in_specs = [
        pl.BlockSpec((ROW_TILE, D_MODEL), row),
        _const_spec(ga.shape), _const_spec(w_main.shape),
        _const_spec(gq.shape), _const_spec(gk.shape), _const_spec(cq.shape), _const_spec(ck.shape),
        _const_spec(eseg.shape), _const_spec(wup.shape), _const_spec(bg.shape),
    ]
    return pl.pallas_call(
        _proj_kernel, out_shape=out_shape, grid=(batch, n_t),
        in_specs=in_specs, out_specs=out_specs, name="proj",
        compiler_params=pltpu.CompilerParams(
            dimension_semantics=("arbitrary", "arbitrary"), vmem_limit_bytes=VMEM_LIMIT),
    )(xf, ga, w_main, gq, gk, cq, ck, eseg, wup, bg)


def _attn_kernel(slope_ref, lq1_ref, lk1_ref, lq2_ref, lk2_ref, gain_ref, qt_ref, k_ref, vt_ref,
                 o_ref, s_sc, cm_sc, m_sc, acc_sc, *, lambda_init):
    h = pl.program_id(1)
    qi = pl.program_id(2)
    m_sc[...] = jnp.full(m_sc.shape, NEG_BIG, F32)
    acc_sc[...] = jnp.zeros(acc_sc.shape, F32)

    def score(j, kb, masked):
        rows = pl.ds(pl.multiple_of(kb * K_TILE, K_TILE), K_TILE)
        s = jnp.dot(k_ref[0, 0, j, rows, :], qt_ref[0, 0, j, 0], preferred_element_type=F32)
        if masked:
            key = lax.broadcasted_iota(jnp.int32, s.shape, 0)
            qry = lax.broadcasted_iota(jnp.int32, s.shape, 1)
            s = jnp.where(key <= qry, s, NEG_BIG)
        s_sc[j] = s
        cm_sc[j] = jnp.max(s, axis=0, keepdims=True)

    def absorb(j, kb):
        off = slope_ref[h] * (kb - qi).astype(F32)
        m_prev = m_sc[j]
        m_new = jnp.maximum(m_prev, cm_sc[j] + off)
        p = jnp.exp(s_sc[j] - (m_new - off))
        alpha = jnp.exp(m_prev - m_new)
        acc_sc[j] = alpha * acc_sc[j] + jnp.dot(vt_ref[0, 0, kb], p.astype(BF16),
                                                preferred_element_type=F32)
        m_sc[j] = m_new

    @pl.when(qi > 0)
    def _():
        score(0, 0, False)

    def unit(kb):
        score(1, kb, False)
        absorb(0, kb)
        score(0, kb + 1, False)
        absorb(1, kb)

    def body(kp, carry):
        for u in range(LOOP_UNROLL):
            unit(LOOP_UNROLL * kp + u)
        return carry

    n_full = jnp.maximum(qi - 1, 0)
    lax.fori_loop(0, n_full // LOOP_UNROLL, body, 0)

    done = (n_full // LOOP_UNROLL) * LOOP_UNROLL
    size = LOOP_UNROLL // 2
    while size >= 1:
        take = (n_full - done) >= size

        @pl.when(take)
        def _(done=done, size=size):
            for u in range(size):
                unit(done + u)

        done = done + jnp.where(take, size, 0)
        size //= 2

    def finish():
        score(1, qi, True)
        absorb(0, qi)
        absorb(1, qi)
        lam = (jnp.exp(jnp.sum(lq1_ref[...] * lk1_ref[...], axis=1, keepdims=True))
               - jnp.exp(jnp.sum(lq2_ref[...] * lk2_ref[...], axis=1, keepdims=True))
               + lambda_init)
        a1 = acc_sc[0]
        a2 = acc_sc[1]
        o_t = (a1[:DIFF_V_DIM] / a1[DIFF_V_DIM:DIFF_V_DIM + 1]
               - lam * (a2[:DIFF_V_DIM] / a2[DIFF_V_DIM:DIFF_V_DIM + 1]))
        o_ref[...] = (_rms(o_t.T, gain_ref[...]) * (1.0 - lambda_init)).astype(BF16)

    @pl.when(qi > 0)
    def _():
        score(1, qi - 1, False)
        absorb(0, qi - 1)
        score(0, qi, True)
        absorb(1, qi - 1)
        finish()

    @pl.when(qi == 0)
    def _():
        score(0, 0, True)
        finish()


def _attn_call(slope_k, lq1, lk1, lq2, lk2, gain, qt, ka, vt, batch, seq, lambda_init):
    n_q = seq // Q_TILE
    n_k = seq // K_TILE
    smem = pl.BlockSpec(memory_space=pltpu.SMEM)
    in_specs = [
        smem,
        _const_spec(lq1.shape), _const_spec(lk1.shape), _const_spec(lq2.shape),
        _const_spec(lk2.shape), _const_spec(gain.shape),
        pl.BlockSpec((1, 1, 2, 1, LANES, Q_TILE), lambda b, h, q: (b, h, 0, q, 0, 0)),
        pl.BlockSpec((1, 1, 2, seq, LANES), lambda b, h, q: (b, h, 0, 0, 0)),
        pl.BlockSpec((1, 1, n_k, VT_ROWS, K_TILE), lambda b, h, q: (b, h, 0, 0, 0)),
    ]
    return pl.pallas_call(
        functools.partial(_attn_kernel, lambda_init=lambda_init),
        out_shape=jax.ShapeDtypeStruct((batch * seq, DIFF_WIDTH), BF16),
        grid=(batch, DIFF_HEADS, n_q),
        in_specs=in_specs,
        out_specs=pl.BlockSpec((Q_TILE, DIFF_V_DIM), lambda b, h, q: (b * n_q + q, h)),
        scratch_shapes=[pltpu.VMEM((2, K_TILE, Q_TILE), F32),
                        pltpu.VMEM((2, 1, Q_TILE), F32),
                        pltpu.VMEM((2, 1, Q_TILE), F32),
                        pltpu.VMEM((2, VT_ROWS, Q_TILE), F32)],
        name="diff_attn",
        compiler_params=pltpu.CompilerParams(
            dimension_semantics=("arbitrary", "arbitrary", "arbitrary"),
            vmem_limit_bytes=VMEM_LIMIT),
    )(slope_k, lq1, lk1, lq2, lk2, gain, qt, ka, vt)


def _gla_kernel(tril_ref, ind_ref, gain_ref, q_ref, k_ref, v_ref, g_ref, la_ref, o_ref,
                st_ref, b_sc, oi_sc, acc_sc, kf_sc, vf_sc):
    @pl.when(pl.program_id(1) == 0)
    def _():
        st_ref[...] = jnp.zeros(st_ref.shape, F32)

    c_len = GLA_CHUNK
    row = lax.broadcasted_iota(jnp.int32, (c_len, c_len), 0)
    col = lax.broadcasted_iota(jnp.int32, (c_len, c_len), 1)
    causal = col <= row
    nt = (((1,), (1,)), ((), ()))

    def finalize(o, rows, vs):
        g = g_ref[rows, vs].astype(F32)
        return (_rms(o, gain_ref[...]) * (g * jax.nn.sigmoid(g))).astype(BF16)

    overflow = []
    for c in range(GLA_TILE // c_len):
        rows = slice(c * c_len, (c + 1) * c_len)
        la = la_ref[rows, :]
        hi = la.astype(BF16)
        r1 = la - hi.astype(F32)
        mid = r1.astype(BF16)
        lo = (r1 - mid.astype(F32)).astype(BF16)
        tril = tril_ref[...]
        b = (jnp.dot(tril, hi, preferred_element_type=F32)
             + jnp.dot(tril, mid, preferred_element_type=F32)
             + jnp.dot(tril, lo, preferred_element_type=F32))
        b_sc[rows, :] = b
        b_last = b[c_len - 1:c_len, :]
        overflow.append(jnp.max(-b_last) > EXP_CLAMP)
        q = q_ref[rows, :].astype(F32)
        k = k_ref[rows, :].astype(F32)
        q_in = (q * jnp.exp(b)).astype(BF16)
        k_out = (k * jnp.exp(jnp.minimum(-b, EXP_CLAMP))).astype(BF16)
        k_dec = (k * jnp.exp(b_last - b)).astype(BF16)
        d_last = jnp.exp(b_last)
        for h in range(GLA_HEADS):
            ks = slice(h * GLA_K_DIM, (h + 1) * GLA_K_DIM)
            vs = slice(h * GLA_V_DIM, (h + 1) * GLA_V_DIM)
            a = lax.dot_general(q_in[:, ks], k_out[:, ks], nt, preferred_element_type=F32)
            a = jnp.where(causal, a, 0.0).astype(BF16)
            vh = v_ref[rows, vs]
            s_t = st_ref[h]
            o_inter = lax.dot_general(q_in[:, ks], s_t.astype(BF16), nt,
                                      preferred_element_type=F32)
            oi_sc[rows, vs] = o_inter
            o = o_inter + jnp.dot(a, vh, preferred_element_type=F32)
            v_t = vh.astype(F32).T.astype(BF16)
            st_ref[h] = s_t * d_last[:, ks] + jnp.dot(v_t, k_dec[:, ks], preferred_element_type=F32)
            o_ref[rows, vs] = finalize(o, rows, vs)

    def exact_intra(c):
        rows = slice(c * c_len, (c + 1) * c_len)
        q = q_ref[rows, :].astype(F32)
        b = b_sc[rows, :]
        kf_sc[...] = k_ref[rows, :].astype(F32)
        vf_sc[...] = v_ref[rows, :].astype(F32)
        acc_sc[...] = jnp.zeros(acc_sc.shape, F32)
        t_idx = lax.broadcasted_iota(jnp.int32, (c_len, GLA_KW), 0)

        def body(s, carry):
            k_s = kf_sc[pl.ds(s, 1), :]
            b_s = b_sc[pl.ds(c * c_len + s, 1), :]
            v_s = vf_sc[pl.ds(s, 1), :]
            w = q * k_s * jnp.exp(jnp.minimum(b - b_s, 0.0))
            w = jnp.where(t_idx >= s, w, 0.0).astype(BF16)
            a_s = jnp.dot(w, ind_ref[...], preferred_element_type=F32)
            acc_sc[...] += a_s * v_s
            return carry

        lax.fori_loop(0, c_len, body, 0)
        for h in range(GLA_HEADS):
            vs = slice(h * GLA_V_DIM, (h + 1) * GLA_V_DIM)
            o_ref[rows, vs] = finalize(oi_sc[rows, vs] + acc_sc[:, vs], rows, vs)

    for c in range(GLA_TILE // c_len):
        @pl.when(overflow[c])
        def _(c=c):
            exact_intra(c)


def _gla_call(tril, ind, gain, gq, gk, gv, go, la, batch, seq):
    n_t = seq // GLA_TILE
    row = lambda b, t: (b * n_t + t, 0)
    in_specs = [
        _const_spec(tril.shape), _const_spec(ind.shape), _const_spec(gain.shape),
        pl.BlockSpec((GLA_TILE, GLA_KW), row),
        pl.BlockSpec((GLA_TILE, GLA_KW), row),
        pl.BlockSpec((GLA_TILE, GLA_WIDTH), row),
        pl.BlockSpec((GLA_TILE, GLA_WIDTH), row),
        pl.BlockSpec((GLA_TILE, GLA_KW), row),
    ]
    return pl.pallas_call(
        _gla_kernel,
        out_shape=jax.ShapeDtypeStruct((batch * seq, GLA_WIDTH), BF16),
        grid=(batch, n_t),
        in_specs=in_specs,
        out_specs=pl.BlockSpec((GLA_TILE, GLA_WIDTH), row),
        scratch_shapes=[pltpu.VMEM((GLA_HEADS, GLA_V_DIM, GLA_K_DIM), F32),
                        pltpu.VMEM((GLA_TILE, GLA_KW), F32),
                        pltpu.VMEM((GLA_TILE, GLA_WIDTH), F32),
                        pltpu.VMEM((GLA_CHUNK, GLA_WIDTH), F32),
                        pltpu.VMEM((GLA_CHUNK, GLA_KW), F32),
                        pltpu.VMEM((GLA_CHUNK, GLA_WIDTH), F32)],
        name="gla",
        compiler_params=pltpu.CompilerParams(
            dimension_semantics=("arbitrary", "arbitrary"), vmem_limit_bytes=VMEM_LIMIT),
    )(tril, ind, gain, gq, gk, gv, go, la)


def _ffn_kernel(x_ref, od_ref, og_ref, wod_ref, wog_ref, gf_ref, wg_ref, wu_ref, wd_ref, o_ref):
    h = (x_ref[...]
         + jnp.dot(od_ref[...], wod_ref[...], preferred_element_type=F32)
         + jnp.dot(og_ref[...], wog_ref[...], preferred_element_type=F32))
    m = _rms(h, gf_ref[...]).astype(BF16)
    o_ref[...] = h
    for c0, c1 in FF_CHUNKS:
        gate = jnp.dot(m, wg_ref[:, c0:c1], preferred_element_type=F32)
        up = jnp.dot(m, wu_ref[:, c0:c1], preferred_element_type=F32)
        act = (gate * jax.nn.sigmoid(gate) * up).astype(BF16)
        o_ref[...] += jnp.dot(act, wd_ref[c0:c1, :], preferred_element_type=F32)


def _ffn_call(xf, od, og, wod, wog, gf, wg, wu, wd):
    n = xf.shape[0]
    row = lambda t: (t, 0)
    in_specs = [
        pl.BlockSpec((ROW_TILE, D_MODEL), row),
        pl.BlockSpec((ROW_TILE, DIFF_WIDTH), row),
        pl.BlockSpec((ROW_TILE, GLA_WIDTH), row),
        _const_spec(wod.shape), _const_spec(wog.shape), _const_spec(gf.shape),
        _const_spec(wg.shape), _const_spec(wu.shape), _const_spec(wd.shape),
    ]
    return pl.pallas_call(
        _ffn_kernel,
        out_shape=jax.ShapeDtypeStruct((n, D_MODEL), F32),
        grid=(n // ROW_TILE,),
        in_specs=in_specs,
        out_specs=pl.BlockSpec((ROW_TILE, D_MODEL), row),
        name="out_ffn",
        compiler_params=pltpu.CompilerParams(
            dimension_semantics=("arbitrary",), vmem_limit_bytes=VMEM_LIMIT),
    )(xf, od, og, wod, wog, gf, wg, wu, wd)


def _alibi_tables():
    assert ROW_TILE == Q_TILE == K_TILE
    pos = np.arange(ROW_TILE)
    a = (pos // ALIBI_SPLIT) * ALIBI_SPLIT
    c = pos % ALIBI_SPLIT
    cq = np.zeros((DIFF_HEADS, ROW_TILE, LANES), np.float32)
    ck = np.zeros((DIFF_HEADS, ROW_TILE, LANES), np.float32)
    d = DIFF_HEAD_DIM
    for h in range(DIFF_HEADS):
        slope = 2.0 ** (-8.0 * (h + 1) / DIFF_HEADS)
        cq[h, :, d + 0] = -slope * a
        cq[h, :, d + 1] = -slope * c
        cq[h, :, d + 2] = 1.0
        cq[h, :, d + 3] = 1.0
        ck[h, :, d + 0] = 1.0
        ck[h, :, d + 1] = 1.0
        ck[h, :, d + 2] = slope * a
        ck[h, :, d + 3] = slope * c
    return jnp.asarray(cq), jnp.asarray(ck)


def kernel(x, attn_norm_gain, w_in, q_norm_gain, k_norm_gain, lambda_q1, lambda_k1, lambda_q2,
           lambda_k2, diff_out_norm_gain, w_gla_gate_up, b_gla_gate, gla_out_norm_gain, w_out,
           ffn_norm_gain, w_ffn_gate, w_ffn_up, w_ffn_down):
    batch, seq, _ = x.shape
    depth = w_in.shape[0]
    assert seq % ROW_TILE == 0 and seq % GLA_TILE == 0

    cq, ck = _alibi_tables()
    seg = np.arange(DIFF_WIDTH) // DIFF_HEAD_DIM
    eseg = jnp.asarray((seg[:, None] == seg[None, :]).astype(np.float32), dtype=BF16)
    tril = jnp.asarray(np.tril(np.ones((GLA_CHUNK, GLA_CHUNK), np.float32)), dtype=BF16)
    ind = jnp.asarray((np.arange(GLA_KW)[:, None] // GLA_K_DIM
                       == np.arange(GLA_WIDTH)[None, :] // GLA_V_DIM).astype(np.float32), dtype=BF16)
    slope_k = jnp.asarray(
        [2.0 ** (-8.0 * (h + 1) / DIFF_HEADS) * K_TILE for h in range(DIFF_HEADS)], F32)
    n_seg = DIFF_WIDTH // DIFF_HEAD_DIM

    hcur = x.reshape(batch * seq, D_MODEL)
    for l in range(depth):
        lambda_init = 0.8 - 0.6 * math.exp(-0.3 * l)
        w_main = w_in[l].astype(BF16)
        wup = w_gla_gate_up[l].astype(BF16)
        row2 = lambda a: a.reshape(1, -1).astype(F32)

        qt, ka, vt, gq, gk, gv, go, la = _proj_call(
            hcur, row2(attn_norm_gain[l]), w_main,
            row2(jnp.tile(q_norm_gain[l], n_seg)), row2(jnp.tile(k_norm_gain[l], n_seg)),
            cq, ck, eseg, wup, row2(b_gla_gate[l]), batch, seq)

        o_diff = _attn_call(slope_k, row2(lambda_q1[l]), row2(lambda_k1[l]), row2(lambda_q2[l]),
                            row2(lambda_k2[l]), row2(diff_out_norm_gain[l]), qt, ka, vt,
                            batch, seq, lambda_init)
        o_gla = _gla_call(tril, ind, row2(gla_out_norm_gain[l]), gq, gk, gv, go, la, batch, seq)

        wo = w_out[l].astype(BF16)
        hcur = _ffn_call(hcur, o_diff, o_gla, wo[:DIFF_WIDTH], wo[DIFF_WIDTH:],
                         row2(ffn_norm_gain[l]), w_ffn_gate[l].astype(BF16),
                         w_ffn_up[l].astype(BF16), w_ffn_down[l].astype(BF16))
    return hcur.reshape(batch, seq, D_MODEL)
```

```python
import functools
import math

import jax
import jax.numpy as jnp
import numpy as np
from jax import lax
from jax.experimental import pallas as pl
from jax.experimental.pallas import tpu as pltpu

F32 = jnp.float32
BF16 = jnp.bfloat16

D_MODEL = 1024
DIFF_HEADS = 4
DIFF_HEAD_DIM = 64
DIFF_V_DIM = 128
DIFF_WIDTH = DIFF_HEADS * DIFF_V_DIM
GLA_HEADS = 4
GLA_K_DIM = 64
GLA_V_DIM = 128
GLA_WIDTH = GLA_HEADS * GLA_V_DIM
GLA_KW = GLA_HEADS * GLA_K_DIM
GLA_GATE_RANK = 16
GLA_GATE_TEMP = 16.0
D_FF = 2816
EPS = 1e-6
MAIN_WIDTH = 3072
IN_WIDTH = MAIN_WIDTH + GLA_GATE_RANK

LANES = 128
ROW_TILE = 512
Q_TILE = 512
VT_ROWS = 144
K_TILE = 512
LOOP_UNROLL = 2
HEADS_PER_STEP = 2
GLA_TILE = 512
GLA_CHUNK = 256
MXU_DIM = 256
FF_CHUNKS = ((0, 6 * MXU_DIM), (6 * MXU_DIM, D_FF))
ALIBI_SPLIT = 16
NEG_BIG = -1e30
EXP_CLAMP = 80.0
VMEM_LIMIT = 56 * 1024 * 1024


def _rms(x, gain):
    return x * lax.rsqrt(jnp.mean(x * x, axis=-1, keepdims=True) + EPS) * gain


def _proj_kernel(x_ref, ga_ref, w_ref, gq_ref, gk_ref, cq_ref, ck_ref, eseg_ref,
                 wup_ref, bg_ref,
                 qt_ref, ka_ref, vt_ref, gq_out, gk_out, gv_out, go_out, la_out):
    nb = _rms(x_ref[...], ga_ref[...]).astype(BF16)

    def proj(c0, c1):
        return jnp.dot(nb, w_ref[:, c0:c1], preferred_element_type=F32)

    def seg_norm(d, gain):
        ss = jnp.dot((d * d).astype(BF16), eseg_ref[...], preferred_element_type=F32)
        return d * lax.rsqrt(ss * (1.0 / DIFF_HEAD_DIM) + EPS) * gain

    yq = seg_norm(proj(0, 512), gq_ref[...]) * (DIFF_HEAD_DIM ** -0.5)
    yk = seg_norm(proj(512, 1024), gk_ref[...])
    dv = proj(1024, 1536)

    lane = lax.broadcasted_iota(jnp.int32, (ROW_TILE, LANES), 1)
    low = lane < DIFF_HEAD_DIM
    ones_rows = (lax.broadcasted_iota(jnp.int32, (VT_ROWS - DIFF_V_DIM, ROW_TILE), 0) == 0)
    for h in range(DIFF_HEADS):
        qb = yq[:, h * LANES:(h + 1) * LANES]
        kb = yk[:, h * LANES:(h + 1) * LANES]
        cq = cq_ref[h]
        ck = ck_ref[h]
        qt_ref[0, h, 0, 0] = jnp.where(low, qb, cq).T.astype(BF16)
        qt_ref[0, h, 1, 0] = jnp.where(low, pltpu.roll(qb, DIFF_HEAD_DIM, 1), cq).T.astype(BF16)
        ka_ref[0, h, 0] = jnp.where(low, kb, ck).astype(BF16)
        ka_ref[0, h, 1] = jnp.where(low, pltpu.roll(kb, DIFF_HEAD_DIM, 1), ck).astype(BF16)
        vt_ref[0, h, 0, :DIFF_V_DIM, :] = dv[:, h * LANES:(h + 1) * LANES].T.astype(BF16)
        vt_ref[0, h, 0, DIFF_V_DIM:, :] = ones_rows.astype(BF16)

    gq_out[...] = (proj(1536, 1792) * (GLA_K_DIM ** -0.5)).astype(BF16)
    gk_out[...] = proj(1792, 2048).astype(BF16)
    gv_out[...] = proj(2048, 2560).astype(BF16)
    go_out[...] = proj(2560, 3072).astype(BF16)
    glr = proj(MAIN_WIDTH, IN_WIDTH)
    z = jnp.dot(glr.astype(BF16), wup_ref[...], preferred_element_type=F32) + bg_ref[...]
    log_sig = jnp.minimum(z, 0.0) - jnp.log(1.0 + jnp.exp(-jnp.abs(z)))
    la_out[...] = log_sig * (1.0 / GLA_GATE_TEMP)


def _const_spec(shape):
    nd = len(shape)
    return pl.BlockSpec(shape, lambda *_: (0,) * nd, pipeline_mode=pl.Buffered(1))


def _proj_call(xf, ga, w_main, gq, gk, cq, ck, eseg, wup, bg, batch, seq):
    n_t = seq // ROW_TILE
    n = batch * seq
    row = lambda b, t: (b * n_t + t, 0)
    out_shape = (
        jax.ShapeDtypeStruct((batch, DIFF_HEADS, 2, seq // Q_TILE, LANES, Q_TILE), BF16),
        jax.ShapeDtypeStruct((batch, DIFF_HEADS, 2, seq, LANES), BF16),
        jax.ShapeDtypeStruct((batch, DIFF_HEADS, seq // K_TILE, VT_ROWS, K_TILE), BF16),
        jax.ShapeDtypeStruct((n, GLA_KW), BF16),
        jax.ShapeDtypeStruct((n, GLA_KW), BF16),
        jax.ShapeDtypeStruct((n, GLA_WIDTH), BF16),
        jax.ShapeDtypeStruct((n, GLA_WIDTH), BF16),
        jax.ShapeDtypeStruct((n, GLA_KW), F32),
    )
    out_specs = (
        pl.BlockSpec((1, DIFF_HEADS, 2, 1, LANES, Q_TILE), lambda b, t: (b, 0, 0, t, 0, 0)),
        pl.BlockSpec((1, DIFF_HEADS, 2, ROW_TILE, LANES), lambda b, t: (b, 0, 0, t, 0)),
        pl.BlockSpec((1, DIFF_HEADS, 1, VT_ROWS, K_TILE), lambda b, t: (b, 0, t, 0, 0)),
        pl.BlockSpec((ROW_TILE, GLA_KW), row),
        pl.BlockSpec((ROW_TILE, GLA_KW), row),
        pl.BlockSpec((ROW_TILE, GLA_WIDTH), row),
        pl.BlockSpec((ROW_TILE, GLA_WIDTH), row),
        pl.BlockSpec((ROW_TILE, GLA_KW), row),
    )
    in_specs = [
        pl.BlockSpec((ROW_TILE, D_MODEL), row),
        _const_spec(ga.shape), _const_spec(w_main.shape),
        _const_spec(gq.shape), _const_spec(gk.shape), _const_spec(cq.shape), _const_spec(ck.shape),
        _const_spec(eseg.shape), _const_spec(wup.shape), _const_spec(bg.shape),
    ]
    return pl.pallas_call(
        _proj_kernel, out_shape=out_shape, grid=(batch, n_t),
        in_specs=in_specs, out_specs=out_specs, name="proj",
        compiler_params=pltpu.CompilerParams(
            dimension_semantics=("arbitrary", "arbitrary"), vmem_limit_bytes=VMEM_LIMIT),
    )(xf, ga, w_main, gq, gk, cq, ck, eseg, wup, bg)


def _attn_kernel(slope_ref, lq1_ref, lk1_ref, lq2_ref, lk2_ref, gain_ref, qt_ref, k_ref, vt_ref,
                 o_ref, s_sc, cm_sc, m_sc, acc_sc, *, lambda_init):
    hg = pl.program_id(1)
    qi = pl.program_id(2)
    m_sc[...] = jnp.full(m_sc.shape, NEG_BIG, F32)
    acc_sc[...] = jnp.zeros(acc_sc.shape, F32)
    group = range(HEADS_PER_STEP)

    def score1(g, j, kb, masked):
        rows = pl.ds(pl.multiple_of(kb * K_TILE, K_TILE), K_TILE)
        s = jnp.dot(k_ref[0, g, j, rows, :], qt_ref[0, g, j, 0], preferred_element_type=F32)
        if masked:
            key = lax.broadcasted_iota(jnp.int32, s.shape, 0)
            qry = lax.broadcasted_iota(jnp.int32, s.shape, 1)
            s = jnp.where(key <= qry, s, NEG_BIG)
        s_sc[g, j] = s
        cm_sc[g, j] = jnp.max(s, axis=0, keepdims=True)

    def absorb1(g, j, kb):
        off = slope_ref[hg * HEADS_PER_STEP + g] * (kb - qi).astype(F32)
        m_prev = m_sc[g, j]
        m_new = jnp.maximum(m_prev, cm_sc[g, j] + off)
        p = jnp.exp(s_sc[g, j] - (m_new - off))
        alpha = jnp.exp(m_prev - m_new)
        acc_sc[g, j] = alpha * acc_sc[g, j] + jnp.dot(vt_ref[0, g, kb], p.astype(BF16),
                                                      preferred_element_type=F32)
        m_sc[g, j] = m_new

    def score(j, kb, masked):
        for g in group:
            score1(g, j, kb, masked)

    def absorb(j, kb):
        for g in group:
            absorb1(g, j, kb)

    @pl.when(qi > 0)
    def _():
        score(0, 0, False)

    def unit(kb):
        score(1, kb, False)
        absorb(0, kb)
        score(0, kb + 1, False)
        absorb(1, kb)

    def body(kp, carry):
        for u in range(LOOP_UNROLL):
            unit(LOOP_UNROLL * kp + u)
        return carry

    n_full = jnp.maximum(qi - 1, 0)
    lax.fori_loop(0, n_full // LOOP_UNROLL, body, 0)

    done = (n_full // LOOP_UNROLL) * LOOP_UNROLL
    size = LOOP_UNROLL // 2
    while size >= 1:
        take = (n_full - done) >= size

        @pl.when(take)
        def _(done=done, size=size):
            for u in range(size):
                unit(done + u)

        done = done + jnp.where(take, size, 0)
        size //= 2

    def finish():
        score(1, qi, True)
        absorb(0, qi)
        absorb(1, qi)
        lam = (jnp.exp(jnp.sum(lq1_ref[...] * lk1_ref[...], axis=1, keepdims=True))
               - jnp.exp(jnp.sum(lq2_ref[...] * lk2_ref[...], axis=1, keepdims=True))
               + lambda_init)
        for g in group:
            a1 = acc_sc[g, 0]
            a2 = acc_sc[g, 1]
            o_t = (a1[:DIFF_V_DIM] / a1[DIFF_V_DIM:DIFF_V_DIM + 1]
                   - lam * (a2[:DIFF_V_DIM] / a2[DIFF_V_DIM:DIFF_V_DIM + 1]))
            o_ref[:, g * DIFF_V_DIM:(g + 1) * DIFF_V_DIM] = (
                _rms(o_t.T, gain_ref[...]) * (1.0 - lambda_init)).astype(BF16)

    @pl.when(qi > 0)
    def _():
        score(1, qi - 1, False)
        absorb(0, qi - 1)
        score(0, qi, True)
        absorb(1, qi - 1)
        finish()

    @pl.when(qi == 0)
    def _():
        score(0, 0, True)
        finish()


def _attn_call(slope_k, lq1, lk1, lq2, lk2, gain, qt, ka, vt, batch, seq, lambda_init):
    n_q = seq // Q_TILE
    n_k = seq // K_TILE
    hps = HEADS_PER_STEP
    smem = pl.BlockSpec(memory_space=pltpu.SMEM)
    in_specs = [
        smem,
        _const_spec(lq1.shape), _const_spec(lk1.shape), _const_spec(lq2.shape),
        _const_spec(lk2.shape), _const_spec(gain.shape),
        pl.BlockSpec((1, hps, 2, 1, LANES, Q_TILE), lambda b, h, q: (b, h, 0, q, 0, 0)),
        pl.BlockSpec((1, hps, 2, seq, LANES), lambda b, h, q: (b, h, 0, 0, 0)),
        pl.BlockSpec((1, hps, n_k, VT_ROWS, K_TILE), lambda b, h, q: (b, h, 0, 0, 0)),
    ]
    return pl.pallas_call(
        functools.partial(_attn_kernel, lambda_init=lambda_init),
        out_shape=jax.ShapeDtypeStruct((batch * seq, DIFF_WIDTH), BF16),
        grid=(batch, DIFF_HEADS // hps, n_q),
        in_specs=in_specs,
        out_specs=pl.BlockSpec((Q_TILE, hps * DIFF_V_DIM), lambda b, h, q: (b * n_q + q, h)),
        scratch_shapes=[pltpu.VMEM((hps, 2, K_TILE, Q_TILE), F32),
                        pltpu.VMEM((hps, 2, 1, Q_TILE), F32),
                        pltpu.VMEM((hps, 2, 1, Q_TILE), F32),
                        pltpu.VMEM((hps, 2, VT_ROWS, Q_TILE), F32)],
        name="diff_attn",
        compiler_params=pltpu.CompilerParams(
            dimension_semantics=("arbitrary", "arbitrary", "arbitrary"),
            vmem_limit_bytes=VMEM_LIMIT),
    )(slope_k, lq1, lk1, lq2, lk2, gain, qt, ka, vt)


def _gla_kernel(tril_ref, ind_ref, gain_ref, q_ref, k_ref, v_ref, g_ref, la_ref, o_ref,
                st_ref, b_sc, oi_sc, acc_sc, kf_sc, vf_sc):
    @pl.when(pl.program_id(1) == 0)
    def _():
        st_ref[...] = jnp.zeros(st_ref.shape, F32)

    c_len = GLA_CHUNK
    row = lax.broadcasted_iota(jnp.int32, (c_len, c_len), 0)
    col = lax.broadcasted_iota(jnp.int32, (c_len, c_len), 1)
    causal = col <= row
    nt = (((1,), (1,)), ((), ()))

    def finalize(o, rows, vs):
        g = g_ref[rows, vs].astype(F32)
        return (_rms(o, gain_ref[...]) * (g * jax.nn.sigmoid(g))).astype(BF16)

    overflow = []
    for c in range(GLA_TILE // c_len):
        rows = slice(c * c_len, (c + 1) * c_len)
        la = la_ref[rows, :]
        hi = la.astype(BF16)
        r1 = la - hi.astype(F32)
        mid = r1.astype(BF16)
        lo = (r1 - mid.astype(F32)).astype(BF16)
        tril = tril_ref[...]
        b = (jnp.dot(tril, hi, preferred_element_type=F32)
             + jnp.dot(tril, mid, preferred_element_type=F32)
             + jnp.dot(tril, lo, preferred_element_type=F32))
        b_sc[rows, :] = b
        b_last = b[c_len - 1:c_len, :]
        overflow.append(jnp.max(-b_last) > EXP_CLAMP)
        q = q_ref[rows, :].astype(F32)
        k = k_ref[rows, :].astype(F32)
        q_in = (q * jnp.exp(b)).astype(BF16)
        k_out = (k * jnp.exp(jnp.minimum(-b, EXP_CLAMP))).astype(BF16)
        k_dec = (k * jnp.exp(b_last - b)).astype(BF16)
        d_last = jnp.exp(b_last)
        for h in range(GLA_HEADS):
            ks = slice(h * GLA_K_DIM, (h + 1) * GLA_K_DIM)
            vs = slice(h * GLA_V_DIM, (h + 1) * GLA_V_DIM)
            a = lax.dot_general(q_in[:, ks], k_out[:, ks], nt, preferred_element_type=F32)
            a = jnp.where(causal, a, 0.0).astype(BF16)
            vh = v_ref[rows, vs]
            s_t = st_ref[h]
            o_inter = lax.dot_general(q_in[:, ks], s_t.astype(BF16), nt,
                                      preferred_element_type=F32)
            oi_sc[rows, vs] = o_inter
            o = o_inter + jnp.dot(a, vh, preferred_element_type=F32)
            v_t = vh.astype(F32).T.astype(BF16)
            st_ref[h] = s_t * d_last[:, ks] + jnp.dot(v_t, k_dec[:, ks], preferred_element_type=F32)
            o_ref[rows, vs] = finalize(o, rows, vs)

    def exact_intra(c):
        rows = slice(c * c_len, (c + 1) * c_len)
        q = q_ref[rows, :].astype(F32)
        b = b_sc[rows, :]
        kf_sc[...] = k_ref[rows, :].astype(F32)
        vf_sc[...] = v_ref[rows, :].astype(F32)
        acc_sc[...] = jnp.zeros(acc_sc.shape, F32)
        t_idx = lax.broadcasted_iota(jnp.int32, (c_len, GLA_KW), 0)

        def body(s, carry):
            k_s = kf_sc[pl.ds(s, 1), :]
            b_s = b_sc[pl.ds(c * c_len + s, 1), :]
            v_s = vf_sc[pl.ds(s, 1), :]
            w = q * k_s * jnp.exp(jnp.minimum(b - b_s, 0.0))
            w = jnp.where(t_idx >= s, w, 0.0).astype(BF16)
            a_s = jnp.dot(w, ind_ref[...], preferred_element_type=F32)
            acc_sc[...] += a_s * v_s
            return carry

        lax.fori_loop(0, c_len, body, 0)
        for h in range(GLA_HEADS):
            vs = slice(h * GLA_V_DIM, (h + 1) * GLA_V_DIM)
            o_ref[rows, vs] = finalize(oi_sc[rows, vs] + acc_sc[:, vs], rows, vs)

    for c in range(GLA_TILE // c_len):
        @pl.when(overflow[c])
        def _(c=c):
            exact_intra(c)


def _gla_call(tril, ind, gain, gq, gk, gv, go, la, batch, seq):
    n_t = seq // GLA_TILE
    row = lambda b, t: (b * n_t + t, 0)
    in_specs = [
        _const_spec(tril.shape), _const_spec(ind.shape), _const_spec(gain.shape),
        pl.BlockSpec((GLA_TILE, GLA_KW), row),
        pl.BlockSpec((GLA_TILE, GLA_KW), row),
        pl.BlockSpec((GLA_TILE, GLA_WIDTH), row),
        pl.BlockSpec((GLA_TILE, GLA_WIDTH), row),
        pl.BlockSpec((GLA_TILE, GLA_KW), row),
    ]
    return pl.pallas_call(
        _gla_kernel,
        out_shape=jax.ShapeDtypeStruct((batch * seq, GLA_WIDTH), BF16),
        grid=(batch, n_t),
        in_specs=in_specs,
        out_specs=pl.BlockSpec((GLA_TILE, GLA_WIDTH), row),
        scratch_shapes=[pltpu.VMEM((GLA_HEADS, GLA_V_DIM, GLA_K_DIM), F32),
                        pltpu.VMEM((GLA_TILE, GLA_KW), F32),
                        pltpu.VMEM((GLA_TILE, GLA_WIDTH), F32),
                        pltpu.VMEM((GLA_CHUNK, GLA_WIDTH), F32),
                        pltpu.VMEM((GLA_CHUNK, GLA_KW), F32),
                        pltpu.VMEM((GLA_CHUNK, GLA_WIDTH), F32)],
        name="gla",
        compiler_params=pltpu.CompilerParams(
            dimension_semantics=("arbitrary", "arbitrary"), vmem_limit_bytes=VMEM_LIMIT),
    )(tril, ind, gain, gq, gk, gv, go, la)


def _ffn_kernel(x_ref, od_ref, og_ref, wod_ref, wog_ref, gf_ref, wg_ref, wu_ref, wd_ref, o_ref):
    h = (x_ref[...]
         + jnp.dot(od_ref[...], wod_ref[...], preferred_element_type=F32)
         + jnp.dot(og_ref[...], wog_ref[...], preferred_element_type=F32))
    m = _rms(h, gf_ref[...]).astype(BF16)
    o_ref[...] = h
    for c0, c1 in FF_CHUNKS:
        gate = jnp.dot(m, wg_ref[:, c0:c1], preferred_element_type=F32)
        up = jnp.dot(m, wu_ref[:, c0:c1], preferred_element_type=F32)
        act = (gate * jax.nn.sigmoid(gate) * up).astype(BF16)
        o_ref[...] += jnp.dot(act, wd_ref[c0:c1, :], preferred_element_type=F32)


def _ffn_call(xf, od, og, wod, wog, gf, wg, wu, wd):
    n = xf.shape[0]
    row = lambda t: (t, 0)
    in_specs = [
        pl.BlockSpec((ROW_TILE, D_MODEL), row),
        pl.BlockSpec((ROW_TILE, DIFF_WIDTH), row),
        pl.BlockSpec((ROW_TILE, GLA_WIDTH), row),
        _const_spec(wod.shape), _const_spec(wog.shape), _const_spec(gf.shape),
        _const_spec(wg.shape), _const_spec(wu.shape), _const_spec(wd.shape),
    ]
    return pl.pallas_call(
        _ffn_kernel,
        out_shape=jax.ShapeDtypeStruct((n, D_MODEL), F32),
        grid=(n // ROW_TILE,),
        in_specs=in_specs,
        out_specs=pl.BlockSpec((ROW_TILE, D_MODEL), row),
        name="out_ffn",
        compiler_params=pltpu.CompilerParams(
            dimension_semantics=("arbitrary",), vmem_limit_bytes=VMEM_LIMIT),
    )(xf, od, og, wod, wog, gf, wg, wu, wd)


def _alibi_tables():
    assert ROW_TILE == Q_TILE == K_TILE
    pos = np.arange(ROW_TILE)
    a = (pos // ALIBI_SPLIT) * ALIBI_SPLIT
    c = pos % ALIBI_SPLIT
    cq = np.zeros((DIFF_HEADS, ROW_TILE, LANES), np.float32)
    ck = np.zeros((DIFF_HEADS, ROW_TILE, LANES), np.float32)
    d = DIFF_HEAD_DIM
    for h in range(DIFF_HEADS):
        slope = 2.0 ** (-8.0 * (h + 1) / DIFF_HEADS)
        cq[h, :, d + 0] = -slope * a
        cq[h, :, d + 1] = -slope * c
        cq[h, :, d + 2] = 1.0
        cq[h, :, d + 3] = 1.0
        ck[h, :, d + 0] = 1.0
        ck[h, :, d + 1] = 1.0
        ck[h, :, d + 2] = slope * a
        ck[h, :, d + 3] = slope * c
    return jnp.asarray(cq), jnp.asarray(ck)


def kernel(x, attn_norm_gain, w_in, q_norm_gain, k_norm_gain, lambda_q1, lambda_k1, lambda_q2,
           lambda_k2, diff_out_norm_gain, w_gla_gate_up, b_gla_gate, gla_out_norm_gain, w_out,
           ffn_norm_gain, w_ffn_gate, w_ffn_up, w_ffn_down):
    batch, seq, _ = x.shape
    depth = w_in.shape[0]
    assert seq % ROW_TILE == 0 and seq % GLA_TILE == 0

    cq, ck = _alibi_tables()
    seg = np.arange(DIFF_WIDTH) // DIFF_HEAD_DIM
    eseg = jnp.asarray((seg[:, None] == seg[None, :]).astype(np.float32), dtype=BF16)
    tril = jnp.asarray(np.tril(np.ones((GLA_CHUNK, GLA_CHUNK), np.float32)), dtype=BF16)
    ind = jnp.asarray((np.arange(GLA_KW)[:, None] // GLA_K_DIM
                       == np.arange(GLA_WIDTH)[None, :] // GLA_V_DIM).astype(np.float32), dtype=BF16)
    slope_k = jnp.asarray(
        [2.0 ** (-8.0 * (h + 1) / DIFF_HEADS) * K_TILE for h in range(DIFF_HEADS)], F32)
    n_seg = DIFF_WIDTH // DIFF_HEAD_DIM

    hcur = x.reshape(batch * seq, D_MODEL)
    for l in range(depth):
        lambda_init = 0.8 - 0.6 * math.exp(-0.3 * l)
        w_main = w_in[l].astype(BF16)
        wup = w_gla_gate_up[l].astype(BF16)
        row2 = lambda a: a.reshape(1, -1).astype(F32)

        qt, ka, vt, gq, gk, gv, go, la = _proj_call(
            hcur, row2(attn_norm_gain[l]), w_main,
            row2(jnp.tile(q_norm_gain[l], n_seg)), row2(jnp.tile(k_norm_gain[l], n_seg)),
            cq, ck, eseg, wup, row2(b_gla_gate[l]), batch, seq)

        o_diff = _attn_call(slope_k, row2(lambda_q1[l]), row2(lambda_k1[l]), row2(lambda_q2[l]),
                            row2(lambda_k2[l]), row2(diff_out_norm_gain[l]), qt, ka, vt,
                            batch, seq, lambda_init)
        o_gla = _gla_call(tril, ind, row2(gla_out_norm_gain[l]), gq, gk, gv, go, la, batch, seq)

        wo = w_out[l].astype(BF16)
        hcur = _ffn_call(hcur, o_diff, o_gla, wo[:DIFF_WIDTH], wo[DIFF_WIDTH:],
                         row2(ffn_norm_gain[l]), w_ffn_gate[l].astype(BF16),
                         w_ffn_up[l].astype(BF16), w_ffn_down[l].astype(BF16))
    return hcur.reshape(batch, seq, D_MODEL)
```

```python
import functools
import math

import jax
import jax.numpy as jnp
import numpy as np
from jax import lax
from jax.experimental import pallas as pl
from jax.experimental.pallas import tpu as pltpu

F32 = jnp.float32
BF16 = jnp.bfloat16

D_MODEL = 1024
DIFF_HEADS = 4
DIFF_HEAD_DIM = 64
DIFF_V_DIM = 128
DIFF_WIDTH = DIFF_HEADS * DIFF_V_DIM
GLA_HEADS = 4
GLA_K_DIM = 64
GLA_V_DIM = 128
GLA_WIDTH = GLA_HEADS * GLA_V_DIM
GLA_KW = GLA_HEADS * GLA_K_DIM
GLA_GATE_RANK = 16
GLA_GATE_TEMP = 16.0
D_FF = 2816
EPS = 1e-6
MAIN_WIDTH = 3072
IN_WIDTH = MAIN_WIDTH + GLA_GATE_RANK

LANES = 128
ROW_TILE = 512
Q_TILE = 512
VT_ROWS = 144
K_TILE = 512
LOOP_UNROLL = 2
HEADS_PER_STEP = 2
GLA_TILE = 512
GLA_CHUNK = 256
MXU_DIM = 256
FF_CHUNKS = ((0, 6 * MXU_DIM), (6 * MXU_DIM, D_FF))
ALIBI_SPLIT = 16
NEG_BIG = -1e30
LOG2E = math.log2(math.e)
EXP_CLAMP = 80.0
VMEM_LIMIT = 56 * 1024 * 1024


def _rms(x, gain):
    return x * lax.rsqrt(jnp.mean(x * x, axis=-1, keepdims=True) + EPS) * gain


def _proj_kernel(x_ref, ga_ref, w_ref, gq_ref, gk_ref, cq_ref, ck_ref,
                 wup_ref, bg_ref,
                 qt_ref, ka_ref, vt_ref, gq_out, gk_out, gv_out, go_out, la_out):
    nb = _rms(x_ref[...], ga_ref[...]).astype(BF16)

    def proj(c0, c1):
        return jnp.dot(nb, w_ref[:, c0:c1], preferred_element_type=F32)

    def head_norm_t(x_t, gain_col):
        ms = jnp.mean(x_t * x_t, axis=0, keepdims=True)
        return x_t * lax.rsqrt(ms + EPS) * gain_col

    dq = proj(0, 512)
    dk = proj(512, 1024)
    dv = proj(1024, 1536)
    d = DIFF_HEAD_DIM
    q_scale = d ** -0.5 * LOG2E
    ones_rows = (lax.broadcasted_iota(jnp.int32, (VT_ROWS - DIFF_V_DIM, ROW_TILE), 0) == 0)
    for h in range(DIFF_HEADS):
        q_t = dq[:, h * LANES:(h + 1) * LANES].T
        k_t = dk[:, h * LANES:(h + 1) * LANES].T
        for j in range(2):
            yq = head_norm_t(q_t[j * d:(j + 1) * d], gq_ref[...]) * q_scale
            yk = head_norm_t(k_t[j * d:(j + 1) * d], gk_ref[...])
            qt_ref[0, h, j, 0, :d, :] = yq.astype(BF16)
            qt_ref[0, h, j, 0, d:, :] = cq_ref[h]
            ka_ref[0, h, j] = jnp.concatenate([yk, ck_ref[h]], axis=0).T.astype(BF16)
        vt_ref[0, h, 0, :DIFF_V_DIM, :] = dv[:, h * LANES:(h + 1) * LANES].T.astype(BF16)
        vt_ref[0, h, 0, DIFF_V_DIM:, :] = ones_rows.astype(BF16)

    gq_out[...] = (proj(1536, 1792) * (GLA_K_DIM ** -0.5)).astype(BF16)
    gk_out[...] = proj(1792, 2048).astype(BF16)
    gv_out[...] = proj(2048, 2560).astype(BF16)
    go_out[...] = proj(2560, 3072).astype(BF16)
    glr = proj(MAIN_WIDTH, IN_WIDTH)
    z = jnp.dot(glr.astype(BF16), wup_ref[...], preferred_element_type=F32) + bg_ref[...]
    log_sig = jnp.minimum(z, 0.0) - jnp.log(1.0 + jnp.exp(-jnp.abs(z)))
    la_out[...] = log_sig * (1.0 / GLA_GATE_TEMP)


def _const_spec(shape):
    nd = len(shape)
    return pl.BlockSpec(shape, lambda *_: (0,) * nd, pipeline_mode=pl.Buffered(1))


def _proj_call(xf, ga, w_main, gq, gk, cq, ck, wup, bg, batch, seq):
    n_t = seq // ROW_TILE
    n = batch * seq
    row = lambda b, t: (b * n_t + t, 0)
    out_shape = (
        jax.ShapeDtypeStruct((batch, DIFF_HEADS, 2, seq // Q_TILE, LANES, Q_TILE), BF16),
        jax.ShapeDtypeStruct((batch, DIFF_HEADS, 2, seq, LANES), BF16),
        jax.ShapeDtypeStruct((batch, DIFF_HEADS, seq // K_TILE, VT_ROWS, K_TILE), BF16),
        jax.ShapeDtypeStruct((n, GLA_KW), BF16),
        jax.ShapeDtypeStruct((n, GLA_KW), BF16),
        jax.ShapeDtypeStruct((n, GLA_WIDTH), BF16),
        jax.ShapeDtypeStruct((n, GLA_WIDTH), BF16),
        jax.ShapeDtypeStruct((n, GLA_KW), F32),
    )
    out_specs = (
        pl.BlockSpec((1, DIFF_HEADS, 2, 1, LANES, Q_TILE), lambda b, t: (b, 0, 0, t, 0, 0)),
        pl.BlockSpec((1, DIFF_HEADS, 2, ROW_TILE, LANES), lambda b, t: (b, 0, 0, t, 0)),
        pl.BlockSpec((1, DIFF_HEADS, 1, VT_ROWS, K_TILE), lambda b, t: (b, 0, t, 0, 0)),
        pl.BlockSpec((ROW_TILE, GLA_KW), row),
        pl.BlockSpec((ROW_TILE, GLA_KW), row),
        pl.BlockSpec((ROW_TILE, GLA_WIDTH), row),
        pl.BlockSpec((ROW_TILE, GLA_WIDTH), row),
        pl.BlockSpec((ROW_TILE, GLA_KW), row),
    )
    in_specs = [
        pl.BlockSpec((ROW_TILE, D_MODEL), row),
        _const_spec(ga.shape), _const_spec(w_main.shape),
        _const_spec(gq.shape), _const_spec(gk.shape), _const_spec(cq.shape), _const_spec(ck.shape),
        _const_spec(wup.shape), _const_spec(bg.shape),
    ]
    return pl.pallas_call(
        _proj_kernel, out_shape=out_shape, grid=(batch, n_t),
        in_specs=in_specs, out_specs=out_specs, name="proj",
        compiler_params=pltpu.CompilerParams(
            dimension_semantics=("arbitrary", "arbitrary"), vmem_limit_bytes=VMEM_LIMIT),
    )(xf, ga, w_main, gq, gk, cq, ck, wup, bg)


def _attn_kernel(slope_ref, lq1_ref, lk1_ref, lq2_ref, lk2_ref, gain_ref, qt_ref, k_ref, vt_ref,
                 o_ref, s_sc, cm_sc, m_sc, acc_sc, *, lambda_init):
    hg = pl.program_id(1)
    qi = pl.program_id(2)
    m_sc[...] = jnp.full(m_sc.shape, NEG_BIG, F32)
    acc_sc[...] = jnp.zeros(acc_sc.shape, F32)
    group = range(HEADS_PER_STEP)

    def score1(g, j, kb, masked):
        rows = pl.ds(pl.multiple_of(kb * K_TILE, K_TILE), K_TILE)
        s = jnp.dot(k_ref[0, g, j, rows, :], qt_ref[0, g, j, 0], preferred_element_type=F32)
        if masked:
            key = lax.broadcasted_iota(jnp.int32, s.shape, 0)
            qry = lax.broadcasted_iota(jnp.int32, s.shape, 1)
            s = jnp.where(key <= qry, s, NEG_BIG)
        s_sc[g, j] = s
        cm_sc[g, j] = jnp.max(s, axis=0, keepdims=True)

    def absorb1(g, j, kb):
        off = slope_ref[hg * HEADS_PER_STEP + g] * (kb - qi).astype(F32)
        m_prev = m_sc[g, j]
        m_new = jnp.maximum(m_prev, cm_sc[g, j] + off)
        p = jnp.exp2(s_sc[g, j] - (m_new - off))
        alpha = jnp.exp2(m_prev - m_new)
        acc_sc[g, j] = alpha * acc_sc[g, j] + jnp.dot(vt_ref[0, g, kb], p.astype(BF16),
                                                      preferred_element_type=F32)
        m_sc[g, j] = m_new

    def score(j, kb, masked):
        for g in group:
            score1(g, j, kb, masked)

    def absorb(j, kb):
        for g in group:
            absorb1(g, j, kb)

    @pl.when(qi > 0)
    def _():
        score(0, 0, False)

    def unit(kb):
        score(1, kb, False)
        absorb(0, kb)
        score(0, kb + 1, False)
        absorb(1, kb)

    def body(kp, carry):
        for u in range(LOOP_UNROLL):
            unit(LOOP_UNROLL * kp + u)
        return carry

    n_full = jnp.maximum(qi - 1, 0)
    lax.fori_loop(0, n_full // LOOP_UNROLL, body, 0)

    done = (n_full // LOOP_UNROLL) * LOOP_UNROLL
    size = LOOP_UNROLL // 2
    while size >= 1:
        take = (n_full - done) >= size

        @pl.when(take)
        def _(done=done, size=size):
            for u in range(size):
                unit(done + u)

        done = done + jnp.where(take, size, 0)
        size //= 2

    def finish():
        score(1, qi, True)
        absorb(0, qi)
        absorb(1, qi)
        lam = (jnp.exp(jnp.sum(lq1_ref[...] * lk1_ref[...], axis=1, keepdims=True))
               - jnp.exp(jnp.sum(lq2_ref[...] * lk2_ref[...], axis=1, keepdims=True))
               + lambda_init)
        for g in group:
            a1 = acc_sc[g, 0]
            a2 = acc_sc[g, 1]
            o_t = (a1[:DIFF_V_DIM] / a1[DIFF_V_DIM:DIFF_V_DIM + 1]
                   - lam * (a2[:DIFF_V_DIM] / a2[DIFF_V_DIM:DIFF_V_DIM + 1]))
            o_ref[:, g * DIFF_V_DIM:(g + 1) * DIFF_V_DIM] = (
                _rms(o_t.T, gain_ref[...]) * (1.0 - lambda_init)).astype(BF16)

    @pl.when(qi > 0)
    def _():
        score(1, qi - 1, False)
        absorb(0, qi - 1)
        score(0, qi, True)
        absorb(1, qi - 1)
        finish()

    @pl.when(qi == 0)
    def _():
        score(0, 0, True)
        finish()


def _attn_call(slope_k, lq1, lk1, lq2, lk2, gain, qt, ka, vt, batch, seq, lambda_init):
    n_q = seq // Q_TILE
    n_k = seq // K_TILE
    hps = HEADS_PER_STEP
    smem = pl.BlockSpec(memory_space=pltpu.SMEM)
    in_specs = [
        smem,
        _const_spec(lq1.shape), _const_spec(lk1.shape), _const_spec(lq2.shape),
        _const_spec(lk2.shape), _const_spec(gain.shape),
        pl.BlockSpec((1, hps, 2, 1, LANES, Q_TILE), lambda b, h, q: (b, h, 0, q, 0, 0)),
        pl.BlockSpec((1, hps, 2, seq, LANES), lambda b, h, q: (b, h, 0, 0, 0)),
        pl.BlockSpec((1, hps, n_k, VT_ROWS, K_TILE), lambda b, h, q: (b, h, 0, 0, 0)),
    ]
    return pl.pallas_call(
        functools.partial(_attn_kernel, lambda_init=lambda_init),
        out_shape=jax.ShapeDtypeStruct((batch * seq, DIFF_WIDTH), BF16),
        grid=(batch, DIFF_HEADS // hps, n_q),
        in_specs=in_specs,
        out_specs=pl.BlockSpec((Q_TILE, hps * DIFF_V_DIM), lambda b, h, q: (b * n_q + q, h)),
        scratch_shapes=[pltpu.VMEM((hps, 2, K_TILE, Q_TILE), F32),
                        pltpu.VMEM((hps, 2, 1, Q_TILE), F32),
                        pltpu.VMEM((hps, 2, 1, Q_TILE), F32),
                        pltpu.VMEM((hps, 2, VT_ROWS, Q_TILE), F32)],
        name="diff_attn",
        compiler_params=pltpu.CompilerParams(
            dimension_semantics=("arbitrary", "arbitrary", "arbitrary"),
            vmem_limit_bytes=VMEM_LIMIT),
    )(slope_k, lq1, lk1, lq2, lk2, gain, qt, ka, vt)


def _gla_kernel(tril_ref, ind_ref, gain_ref, q_ref, k_ref, v_ref, g_ref, la_ref, o_ref,
                st_ref, b_sc, oi_sc, acc_sc, kf_sc, vf_sc):
    @pl.when(pl.program_id(1) == 0)
    def _():
        st_ref[...] = jnp.zeros(st_ref.shape, F32)

    c_len = GLA_CHUNK
    row = lax.broadcasted_iota(jnp.int32, (c_len, c_len), 0)
    col = lax.broadcasted_iota(jnp.int32, (c_len, c_len), 1)
    causal = col <= row
    nt = (((1,), (1,)), ((), ()))

    def finalize(o, rows, vs):
        g = g_ref[rows, vs].astype(F32)
        return (_rms(o, gain_ref[...]) * (g * jax.nn.sigmoid(g))).astype(BF16)

    overflow = []
    for c in range(GLA_TILE // c_len):
        rows = slice(c * c_len, (c + 1) * c_len)
        la = la_ref[rows, :]
        hi = la.astype(BF16)
        r1 = la - hi.astype(F32)
        mid = r1.astype(BF16)
        lo = (r1 - mid.astype(F32)).astype(BF16)
        tril = tril_ref[...]
        b = (jnp.dot(tril, hi, preferred_element_type=F32)
             + jnp.dot(tril, mid, preferred_element_type=F32)
             + jnp.dot(tril, lo, preferred_element_type=F32))
        b_sc[rows, :] = b
        b_last = b[c_len - 1:c_len, :]
        overflow.append(jnp.max(-b_last) > EXP_CLAMP)
        q = q_ref[rows, :].astype(F32)
        k = k_ref[rows, :].astype(F32)
        q_in = (q * jnp.exp(b)).astype(BF16)
        k_out = (k * jnp.exp(jnp.minimum(-b, EXP_CLAMP))).astype(BF16)
        k_dec = (k * jnp.exp(b_last - b)).astype(BF16)
        d_last = jnp.exp(b_last)
        for h in range(GLA_HEADS):
            ks = slice(h * GLA_K_DIM, (h + 1) * GLA_K_DIM)
            vs = slice(h * GLA_V_DIM, (h + 1) * GLA_V_DIM)
            a = lax.dot_general(q_in[:, ks], k_out[:, ks], nt, preferred_element_type=F32)
            a = jnp.where(causal, a, 0.0).astype(BF16)
            vh = v_ref[rows, vs]
            s_t = st_ref[h]
            o_inter = lax.dot_general(q_in[:, ks], s_t.astype(BF16), nt,
                                      preferred_element_type=F32)
            oi_sc[rows, vs] = o_inter
            o = o_inter + jnp.dot(a, vh, preferred_element_type=F32)
            v_t = vh.astype(F32).T.astype(BF16)
            st_ref[h] = s_t * d_last[:, ks] + jnp.dot(v_t, k_dec[:, ks], preferred_element_type=F32)
            o_ref[rows, vs] = finalize(o, rows, vs)

    def exact_intra(c):
        rows = slice(c * c_len, (c + 1) * c_len)
        q = q_ref[rows, :].astype(F32)
        b = b_sc[rows, :]
        kf_sc[...] = k_ref[rows, :].astype(F32)
        vf_sc[...] = v_ref[rows, :].astype(F32)
        acc_sc[...] = jnp.zeros(acc_sc.shape, F32)
        t_idx = lax.broadcasted_iota(jnp.int32, (c_len, GLA_KW), 0)

        def body(s, carry):
            k_s = kf_sc[pl.ds(s, 1), :]
            b_s = b_sc[pl.ds(c * c_len + s, 1), :]
            v_s = vf_sc[pl.ds(s, 1), :]
            w = q * k_s * jnp.exp(jnp.minimum(b - b_s, 0.0))
            w = jnp.where(t_idx >= s, w, 0.0).astype(BF16)
            a_s = jnp.dot(w, ind_ref[...], preferred_element_type=F32)
            acc_sc[...] += a_s * v_s
            return carry

        lax.fori_loop(0, c_len, body, 0)
        for h in range(GLA_HEADS):
            vs = slice(h * GLA_V_DIM, (h + 1) * GLA_V_DIM)
            o_ref[rows, vs] = finalize(oi_sc[rows, vs] + acc_sc[:, vs], rows, vs)

    for c in range(GLA_TILE // c_len):
        @pl.when(overflow[c])
        def _(c=c):
            exact_intra(c)


def _gla_call(tril, ind, gain, gq, gk, gv, go, la, batch, seq):
    n_t = seq // GLA_TILE
    row = lambda b, t: (b * n_t + t, 0)
    in_specs = [
        _const_spec(tril.shape), _const_spec(ind.shape), _const_spec(gain.shape),
        pl.BlockSpec((GLA_TILE, GLA_KW), row),
        pl.BlockSpec((GLA_TILE, GLA_KW), row),
        pl.BlockSpec((GLA_TILE, GLA_WIDTH), row),
        pl.BlockSpec((GLA_TILE, GLA_WIDTH), row),
        pl.BlockSpec((GLA_TILE, GLA_KW), row),
    ]
    return pl.pallas_call(
        _gla_kernel,
        out_shape=jax.ShapeDtypeStruct((batch * seq, GLA_WIDTH), BF16),
        grid=(batch, n_t),
        in_specs=in_specs,
        out_specs=pl.BlockSpec((GLA_TILE, GLA_WIDTH), row),
        scratch_shapes=[pltpu.VMEM((GLA_HEADS, GLA_V_DIM, GLA_K_DIM), F32),
                        pltpu.VMEM((GLA_TILE, GLA_KW), F32),
                        pltpu.VMEM((GLA_TILE, GLA_WIDTH), F32),
                        pltpu.VMEM((GLA_CHUNK, GLA_WIDTH), F32),
                        pltpu.VMEM((GLA_CHUNK, GLA_KW), F32),
                        pltpu.VMEM((GLA_CHUNK, GLA_WIDTH), F32)],
        name="gla",
        compiler_params=pltpu.CompilerParams(
            dimension_semantics=("arbitrary", "arbitrary"), vmem_limit_bytes=VMEM_LIMIT),
    )(tril, ind, gain, gq, gk, gv, go, la)


def _ffn_kernel(x_ref, od_ref, og_ref, wod_ref, wog_ref, gf_ref, wg_ref, wu_ref, wd_ref, o_ref):
    h = (x_ref[...]
         + jnp.dot(od_ref[...], wod_ref[...], preferred_element_type=F32)
         + jnp.dot(og_ref[...], wog_ref[...], preferred_element_type=F32))
    m = _rms(h, gf_ref[...]).astype(BF16)
    o_ref[...] = h
    for c0, c1 in FF_CHUNKS:
        gate = jnp.dot(m, wg_ref[:, c0:c1], preferred_element_type=F32)
        up = jnp.dot(m, wu_ref[:, c0:c1], preferred_element_type=F32)
        act = (gate * jax.nn.sigmoid(gate) * up).astype(BF16)
        o_ref[...] += jnp.dot(act, wd_ref[c0:c1, :], preferred_element_type=F32)


def _ffn_call(xf, od, og, wod, wog, gf, wg, wu, wd):
    n = xf.shape[0]
    row = lambda t: (t, 0)
    in_specs = [
        pl.BlockSpec((ROW_TILE, D_MODEL), row),
        pl.BlockSpec((ROW_TILE, DIFF_WIDTH), row),
        pl.BlockSpec((ROW_TILE, GLA_WIDTH), row),
        _const_spec(wod.shape), _const_spec(wog.shape), _const_spec(gf.shape),
        _const_spec(wg.shape), _const_spec(wu.shape), _const_spec(wd.shape),
    ]
    return pl.pallas_call(
        _ffn_kernel,
        out_shape=jax.ShapeDtypeStruct((n, D_MODEL), F32),
        grid=(n // ROW_TILE,),
        in_specs=in_specs,
        out_specs=pl.BlockSpec((ROW_TILE, D_MODEL), row),
        name="out_ffn",
        compiler_params=pltpu.CompilerParams(
            dimension_semantics=("arbitrary",), vmem_limit_bytes=VMEM_LIMIT),
    )(xf, od, og, wod, wog, gf, wg, wu, wd)


def _bf16_pieces(value, n=3):
    pieces = []
    rest = np.float32(value)
    for _ in range(n):
        piece = np.float32(np.asarray(rest, dtype=BF16))
        pieces.append(float(piece))
        rest = np.float32(rest - piece)
    return pieces


def _alibi_tables():
    assert ROW_TILE == Q_TILE == K_TILE
    pos = np.arange(ROW_TILE)
    a = (pos // ALIBI_SPLIT) * ALIBI_SPLIT
    c = pos % ALIBI_SPLIT
    extra = LANES - DIFF_HEAD_DIM
    cq = np.zeros((DIFF_HEADS, extra, ROW_TILE), np.float32)
    ck = np.zeros((DIFF_HEADS, extra, ROW_TILE), np.float32)
    for h in range(DIFF_HEADS):
        r = 0
        for piece in _bf16_pieces(_slope(h) * LOG2E):
            for q_row, k_row in ((piece, a), (piece, c), (-a, piece), (-c, piece)):
                cq[h, r, :] = q_row
                ck[h, r, :] = k_row
                r += 1
        assert r <= extra
    return jnp.asarray(cq, dtype=BF16), jnp.asarray(ck)


def _slope(h):
    return 2.0 ** (-8.0 * (h + 1) / DIFF_HEADS)


def kernel(x, attn_norm_gain, w_in, q_norm_gain, k_norm_gain, lambda_q1, lambda_k1, lambda_q2,
           lambda_k2, diff_out_norm_gain, w_gla_gate_up, b_gla_gate, gla_out_norm_gain, w_out,
           ffn_norm_gain, w_ffn_gate, w_ffn_up, w_ffn_down):
    batch, seq, _ = x.shape
    depth = w_in.shape[0]
    assert seq % ROW_TILE == 0 and seq % GLA_TILE == 0

    cq, ck = _alibi_tables()
    tril = jnp.asarray(np.tril(np.ones((GLA_CHUNK, GLA_CHUNK), np.float32)), dtype=BF16)
    ind = jnp.asarray((np.arange(GLA_KW)[:, None] // GLA_K_DIM
                       == np.arange(GLA_WIDTH)[None, :] // GLA_V_DIM).astype(np.float32), dtype=BF16)
    slope_k = jnp.asarray(
        [sum(_bf16_pieces(_slope(h) * LOG2E)) * K_TILE for h in range(DIFF_HEADS)], F32)

    hcur = x.reshape(batch * seq, D_MODEL)
    for l in range(depth):
        lambda_init = 0.8 - 0.6 * math.exp(-0.3 * l)
        w_main = w_in[l].astype(BF16)
        wup = w_gla_gate_up[l].astype(BF16)
        row2 = lambda a: a.reshape(1, -1).astype(F32)
        col2 = lambda a: a.reshape(-1, 1).astype(F32)

        qt, ka, vt, gq, gk, gv, go, la = _proj_call(
            hcur, row2(attn_norm_gain[l]), w_main, col2(q_norm_gain[l]), col2(k_norm_gain[l]),
            cq, ck, wup, row2(b_gla_gate[l]), batch, seq)

        o_diff = _attn_call(slope_k, row2(lambda_q1[l]), row2(lambda_k1[l]), row2(lambda_q2[l]),
                            row2(lambda_k2[l]), row2(diff_out_norm_gain[l]), qt, ka, vt,
                            batch, seq, lambda_init)
        o_gla = _gla_call(tril, ind, row2(gla_out_norm_gain[l]), gq, gk, gv, go, la, batch, seq)

        wo = w_out[l].astype(BF16)
        hcur = _ffn_call(hcur, o_diff, o_gla, wo[:DIFF_WIDTH], wo[DIFF_WIDTH:],
                         row2(ffn_norm_gain[l]), w_ffn_gate[l].astype(BF16),
                         w_ffn_up[l].astype(BF16), w_ffn_down[l].astype(BF16))
    return hcur.reshape(batch, seq, D_MODEL)
```

```python
import functools
import math

import jax
import jax.numpy as jnp
import numpy as np
from jax import lax
from jax.experimental import pallas as pl
from jax.experimental.pallas import tpu as pltpu

F32 = jnp.float32
BF16 = jnp.bfloat16

D_MODEL = 1024
DIFF_HEADS = 4
DIFF_HEAD_DIM = 64
DIFF_V_DIM = 128
DIFF_WIDTH = DIFF_HEADS * DIFF_V_DIM
GLA_HEADS = 4
GLA_K_DIM = 64
GLA_V_DIM = 128
GLA_WIDTH = GLA_HEADS * GLA_V_DIM
GLA_KW = GLA_HEADS * GLA_K_DIM
GLA_GATE_RANK = 16
GLA_GATE_TEMP = 16.0
D_FF = 2816
EPS = 1e-6
MAIN_WIDTH = 3072
IN_WIDTH = MAIN_WIDTH + GLA_GATE_RANK

LANES = 128
ROW_TILE = 512
Q_TILE = 512
VT_ROWS = 144
K_TILE = 512
LOOP_UNROLL = 2
HEADS_PER_STEP = 2
GLA_TILE = 512
GLA_CHUNK = 256
MXU_DIM = 256
WEIGHT_CAST_COLS = 512
FF_CHUNKS = ((0, 6 * MXU_DIM), (6 * MXU_DIM, D_FF))
ALIBI_SPLIT = 16
NEG_BIG = -1e30
LOG2E = math.log2(math.e)
EXP_CLAMP = 80.0
VMEM_LIMIT = 56 * 1024 * 1024


def _rms(x, gain):
    return x * lax.rsqrt(jnp.mean(x * x, axis=-1, keepdims=True) + EPS) * gain


def _proj_kernel(x_ref, ga_ref, w_ref, gq_ref, gk_ref, cq_ref, ck_ref,
                 wup_ref, bg_ref, tril_ref, ind_ref, ggain_ref,
                 qt_ref, ka_ref, vt_ref, og_ref,
                 wb_sc, gq_out, gk_out, gv_out, go_out, la_out,
                 st_ref, b_sc, oi_sc, acc_sc, kf_sc, vf_sc):
    @pl.when((pl.program_id(0) == 0) & (pl.program_id(1) == 0))
    def _():
        for c0 in range(0, IN_WIDTH, WEIGHT_CAST_COLS):
            c1 = min(c0 + WEIGHT_CAST_COLS, IN_WIDTH)
            wb_sc[:, c0:c1] = w_ref[:, c0:c1].astype(BF16)

    nb = _rms(x_ref[...], ga_ref[...]).astype(BF16)

    def proj(c0, c1):
        return jnp.dot(nb, wb_sc[:, c0:c1], preferred_element_type=F32)

    def head_norm_t(x_t, gain_col):
        ms = jnp.mean(x_t * x_t, axis=0, keepdims=True)
        return x_t * lax.rsqrt(ms + EPS) * gain_col

    gla_cols = proj(1536, IN_WIDTH)
    gq_out[...] = (gla_cols[:, 0:256] * (GLA_K_DIM ** -0.5)).astype(BF16)
    gk_out[...] = gla_cols[:, 256:512].astype(BF16)
    gv_out[...] = gla_cols[:, 512:1024].astype(BF16)
    go_out[...] = gla_cols[:, 1024:1536].astype(BF16)
    glr = gla_cols[:, 1536:]
    z = jnp.dot(glr.astype(BF16), wup_ref[...], preferred_element_type=F32) + bg_ref[...]
    log_sig = jnp.minimum(z, 0.0) - jnp.log(1.0 + jnp.exp(-jnp.abs(z)))
    la_out[...] = log_sig * (1.0 / GLA_GATE_TEMP)

    qkv = []

    def project_qkv():
        full = proj(0, 1536)
        qkv.extend([full[:, 0:512], full[:, 512:1024], full[:, 1024:1536]])

    redo, overflow = _gla_tile(tril_ref, ind_ref, ggain_ref, gq_out, gk_out, gv_out, go_out, la_out,
                               og_ref, st_ref, b_sc, oi_sc, acc_sc, kf_sc, vf_sc,
                               after_cumsum=project_qkv)
    dq, dk, dv = qkv

    d = DIFF_HEAD_DIM
    q_scale = d ** -0.5 * LOG2E
    ones_rows = (lax.broadcasted_iota(jnp.int32, (VT_ROWS - DIFF_V_DIM, ROW_TILE), 0) == 0)
    for h in range(DIFF_HEADS):
        q_t = dq[:, h * LANES:(h + 1) * LANES].T
        k_t = dk[:, h * LANES:(h + 1) * LANES].T
        for j in range(2):
            yq = head_norm_t(q_t[j * d:(j + 1) * d], gq_ref[...]) * q_scale
            yk = head_norm_t(k_t[j * d:(j + 1) * d], gk_ref[...])
            qt_ref[0, h, j, 0, :d, :] = yq.astype(BF16)
            qt_ref[0, h, j, 0, d:, :] = cq_ref[h]
            ka_ref[0, h, j] = jnp.concatenate([yk, ck_ref[h]], axis=0).T.astype(BF16)
        vt_ref[0, h, 0, :DIFF_V_DIM, :] = dv[:, h * LANES:(h + 1) * LANES].T.astype(BF16)
        vt_ref[0, h, 0, DIFF_V_DIM:, :] = ones_rows.astype(BF16)

    for fn, flag in zip(redo, overflow):
        pl.when(flag)(fn)


def _const_spec(shape):
    nd = len(shape)
    return pl.BlockSpec(shape, lambda *_: (0,) * nd, pipeline_mode=pl.Buffered(1))


def _proj_call(xf, ga, w_main, gq, gk, cq, ck, wup, bg, tril, ind, ggain, batch, seq):
    assert ROW_TILE == GLA_TILE
    n_t = seq // ROW_TILE
    n = batch * seq
    row = lambda b, t: (b * n_t + t, 0)
    out_shape = (
        jax.ShapeDtypeStruct((batch, DIFF_HEADS, 2, seq // Q_TILE, LANES, Q_TILE), BF16),
        jax.ShapeDtypeStruct((batch, DIFF_HEADS, 2, seq, LANES), BF16),
        jax.ShapeDtypeStruct((batch, DIFF_HEADS, seq // K_TILE, VT_ROWS, K_TILE), BF16),
        jax.ShapeDtypeStruct((n, GLA_WIDTH), BF16),
    )
    out_specs = (
        pl.BlockSpec((1, DIFF_HEADS, 2, 1, LANES, Q_TILE), lambda b, t: (b, 0, 0, t, 0, 0)),
        pl.BlockSpec((1, DIFF_HEADS, 2, ROW_TILE, LANES), lambda b, t: (b, 0, 0, t, 0)),
        pl.BlockSpec((1, DIFF_HEADS, 1, VT_ROWS, K_TILE), lambda b, t: (b, 0, t, 0, 0)),
        pl.BlockSpec((ROW_TILE, GLA_WIDTH), row),
    )
    consts = (ga, w_main, gq, gk, cq, ck, wup, bg, tril, ind, ggain)
    in_specs = [pl.BlockSpec((ROW_TILE, D_MODEL), row)] + [_const_spec(a.shape) for a in consts]
    scratch_shapes = [
        pltpu.VMEM((D_MODEL, IN_WIDTH), BF16),
        pltpu.VMEM((ROW_TILE, GLA_KW), BF16),
        pltpu.VMEM((ROW_TILE, GLA_KW), BF16),
        pltpu.VMEM((ROW_TILE, GLA_WIDTH), BF16),
        pltpu.VMEM((ROW_TILE, GLA_WIDTH), BF16),
        pltpu.VMEM((ROW_TILE, GLA_KW), F32),
        pltpu.VMEM((GLA_HEADS, GLA_V_DIM, GLA_K_DIM), F32),
        pltpu.VMEM((GLA_TILE, GLA_KW), F32),
        pltpu.VMEM((GLA_TILE, GLA_WIDTH), F32),
        pltpu.VMEM((GLA_CHUNK, GLA_WIDTH), F32),
        pltpu.VMEM((GLA_CHUNK, GLA_KW), F32),
        pltpu.VMEM((GLA_CHUNK, GLA_WIDTH), F32),
    ]
    return pl.pallas_call(
        _proj_kernel, out_shape=out_shape, grid=(batch, n_t),
        in_specs=in_specs, out_specs=out_specs, name="proj_gla",
        scratch_shapes=scratch_shapes,
        compiler_params=pltpu.CompilerParams(
            dimension_semantics=("arbitrary", "arbitrary"), vmem_limit_bytes=VMEM_LIMIT),
    )(xf, *consts)


def _attn_kernel(slope_ref, lq1_ref, lk1_ref, lq2_ref, lk2_ref, gain_ref, qt_ref, k_ref, vt_ref,
                 o_ref, s_sc, cm_sc, m_sc, acc_sc, *, lambda_init):
    hg = pl.program_id(1)
    qi = pl.program_id(2)
    m_sc[...] = jnp.full(m_sc.shape, NEG_BIG, F32)
    acc_sc[...] = jnp.zeros(acc_sc.shape, F32)
    group = range(HEADS_PER_STEP)

    def score1(g, j, kb, masked):
        rows = pl.ds(pl.multiple_of(kb * K_TILE, K_TILE), K_TILE)
        s = jnp.dot(k_ref[0, g, j, rows, :], qt_ref[0, g, j, 0], preferred_element_type=F32)
        if masked:
            key = lax.broadcasted_iota(jnp.int32, s.shape, 0)
            qry = lax.broadcasted_iota(jnp.int32, s.shape, 1)
            s = jnp.where(key <= qry, s, NEG_BIG)
        s_sc[g, j] = s
        cm_sc[g, j] = jnp.max(s, axis=0, keepdims=True)

    def absorb1(g, j, kb):
        off = slope_ref[hg * HEADS_PER_STEP + g] * (kb - qi).astype(F32)
        m_prev = m_sc[g, j]
        m_new = jnp.maximum(m_prev, cm_sc[g, j] + off)
        p = jnp.exp2(s_sc[g, j] - (m_new - off))
        alpha = jnp.exp2(m_prev - m_new)
        acc_sc[g, j] = alpha * acc_sc[g, j] + jnp.dot(vt_ref[0, g, kb], p.astype(BF16),
                                                      preferred_element_type=F32)
        m_sc[g, j] = m_new

    def score(j, kb, masked):
        for g in group:
            score1(g, j, kb, masked)

    def absorb(j, kb):
        for g in group:
            absorb1(g, j, kb)

    @pl.when(qi > 0)
    def _():
        score(0, 0, False)

    def unit(kb):
        score(1, kb, False)
        absorb(0, kb)
        score(0, kb + 1, False)
        absorb(1, kb)

    def body(kp, carry):
        for u in range(LOOP_UNROLL):
            unit(LOOP_UNROLL * kp + u)
        return carry

    n_full = jnp.maximum(qi - 1, 0)
    lax.fori_loop(0, n_full // LOOP_UNROLL, body, 0)

    done = (n_full // LOOP_UNROLL) * LOOP_UNROLL
    size = LOOP_UNROLL // 2
    while size >= 1:
        take = (n_full - done) >= size

        @pl.when(take)
        def _(done=done, size=size):
            for u in range(size):
                unit(done + u)

        done = done + jnp.where(take, size, 0)
        size //= 2

    def finish():
        score(1, qi, True)
        absorb(0, qi)
        absorb(1, qi)
        lam = (jnp.exp(jnp.sum(lq1_ref[...] * lk1_ref[...], axis=1, keepdims=True))
               - jnp.exp(jnp.sum(lq2_ref[...] * lk2_ref[...], axis=1, keepdims=True))
               + lambda_init)
        for g in group:
            a1 = acc_sc[g, 0]
            a2 = acc_sc[g, 1]
            o_t = (a1[:DIFF_V_DIM] / a1[DIFF_V_DIM:DIFF_V_DIM + 1]
                   - lam * (a2[:DIFF_V_DIM] / a2[DIFF_V_DIM:DIFF_V_DIM + 1]))
            o_ref[:, g * DIFF_V_DIM:(g + 1) * DIFF_V_DIM] = (
                _rms(o_t.T, gain_ref[...]) * (1.0 - lambda_init)).astype(BF16)

    @pl.when(qi > 0)
    def _():
        score(1, qi - 1, False)
        absorb(0, qi - 1)
        score(0, qi, True)
        absorb(1, qi - 1)
        finish()

    @pl.when(qi == 0)
    def _():
        score(0, 0, True)
        finish()


def _attn_call(slope_k, lq1, lk1, lq2, lk2, gain, qt, ka, vt, batch, seq, lambda_init):
    n_q = seq // Q_TILE
    n_k = seq // K_TILE
    hps = HEADS_PER_STEP
    smem = pl.BlockSpec(memory_space=pltpu.SMEM)
    in_specs = [
        smem,
        _const_spec(lq1.shape), _const_spec(lk1.shape), _const_spec(lq2.shape),
        _const_spec(lk2.shape), _const_spec(gain.shape),
        pl.BlockSpec((1, hps, 2, 1, LANES, Q_TILE), lambda b, h, q: (b, h, 0, q, 0, 0)),
        pl.BlockSpec((1, hps, 2, seq, LANES), lambda b, h, q: (b, h, 0, 0, 0)),
        pl.BlockSpec((1, hps, n_k, VT_ROWS, K_TILE), lambda b, h, q: (b, h, 0, 0, 0)),
    ]
    return pl.pallas_call(
        functools.partial(_attn_kernel, lambda_init=lambda_init),
        out_shape=jax.ShapeDtypeStruct((batch * seq, DIFF_WIDTH), BF16),
        grid=(batch, DIFF_HEADS // hps, n_q),
        in_specs=in_specs,
        out_specs=pl.BlockSpec((Q_TILE, hps * DIFF_V_DIM), lambda b, h, q: (b * n_q + q, h)),
        scratch_shapes=[pltpu.VMEM((hps, 2, K_TILE, Q_TILE), F32),
                        pltpu.VMEM((hps, 2, 1, Q_TILE), F32),
                        pltpu.VMEM((hps, 2, 1, Q_TILE), F32),
                        pltpu.VMEM((hps, 2, VT_ROWS, Q_TILE), F32)],
        name="diff_attn",
        compiler_params=pltpu.CompilerParams(
            dimension_semantics=("arbitrary", "arbitrary", "arbitrary"),
            vmem_limit_bytes=VMEM_LIMIT),
    )(slope_k, lq1, lk1, lq2, lk2, gain, qt, ka, vt)


def _gla_tile(tril_ref, ind_ref, gain_ref, q_ref, k_ref, v_ref, g_ref, la_ref, o_ref,
              st_ref, b_sc, oi_sc, acc_sc, kf_sc, vf_sc, after_cumsum=None):
    @pl.when(pl.program_id(1) == 0)
    def _():
        st_ref[...] = jnp.zeros(st_ref.shape, F32)

    c_len = GLA_CHUNK
    row = lax.broadcasted_iota(jnp.int32, (c_len, c_len), 0)
    col = lax.broadcasted_iota(jnp.int32, (c_len, c_len), 1)
    causal = col <= row
    nt = (((1,), (1,)), ((), ()))

    def finalize(o, rows, vs):
        g = g_ref[rows, vs].astype(F32)
        return (_rms(o, gain_ref[...]) * (g * jax.nn.sigmoid(g))).astype(BF16)

    n_chunks = GLA_TILE // c_len
    cums = []
    for c in range(n_chunks):
        la = la_ref[c * c_len:(c + 1) * c_len, :]
        hi = la.astype(BF16)
        r1 = la - hi.astype(F32)
        mid = r1.astype(BF16)
        lo = (r1 - mid.astype(F32)).astype(BF16)
        tril = tril_ref[...]
        cums.append(jnp.dot(tril, hi, preferred_element_type=F32)
                    + jnp.dot(tril, mid, preferred_element_type=F32)
                    + jnp.dot(tril, lo, preferred_element_type=F32))

    if after_cumsum is not None:
        after_cumsum()

    overflow = []
    for c in range(n_chunks):
        rows = slice(c * c_len, (c + 1) * c_len)
        b = cums[c]
        b_sc[rows, :] = b
        b_last = b[c_len - 1:c_len, :]
        overflow.append(jnp.max(-b_last) > EXP_CLAMP)
        q = q_ref[rows, :].astype(F32)
        k = k_ref[rows, :].astype(F32)
        q_in = (q * jnp.exp(b)).astype(BF16)
        k_out = (k * jnp.exp(jnp.minimum(-b, EXP_CLAMP))).astype(BF16)
        k_dec = (k * jnp.exp(b_last - b)).astype(BF16)
        d_last = jnp.exp(b_last)
        for h in range(GLA_HEADS):
            ks = slice(h * GLA_K_DIM, (h + 1) * GLA_K_DIM)
            vs = slice(h * GLA_V_DIM, (h + 1) * GLA_V_DIM)
            a = lax.dot_general(q_in[:, ks], k_out[:, ks], nt, preferred_element_type=F32)
            a = jnp.where(causal, a, 0.0).astype(BF16)
            vh = v_ref[rows, vs]
            s_t = st_ref[h]
            o_inter = lax.dot_general(q_in[:, ks], s_t.astype(BF16), nt,
                                      preferred_element_type=F32)
            oi_sc[rows, vs] = o_inter
            o = o_inter + jnp.dot(a, vh, preferred_element_type=F32)
            v_t = vh.astype(F32).T.astype(BF16)
            st_ref[h] = s_t * d_last[:, ks] + jnp.dot(v_t, k_dec[:, ks], preferred_element_type=F32)
            o_ref[rows, vs] = finalize(o, rows, vs)

    def exact_intra(c):
        rows = slice(c * c_len, (c + 1) * c_len)
        q = q_ref[rows, :].astype(F32)
        b = b_sc[rows, :]
        kf_sc[...] = k_ref[rows, :].astype(F32)
        vf_sc[...] = v_ref[rows, :].astype(F32)
        acc_sc[...] = jnp.zeros(acc_sc.shape, F32)
        t_idx = lax.broadcasted_iota(jnp.int32, (c_len, GLA_KW), 0)

        def body(s, carry):
            k_s = kf_sc[pl.ds(s, 1), :]
            b_s = b_sc[pl.ds(c * c_len + s, 1), :]
            v_s = vf_sc[pl.ds(s, 1), :]
            w = q * k_s * jnp.exp(jnp.minimum(b - b_s, 0.0))
            w = jnp.where(t_idx >= s, w, 0.0).astype(BF16)
            a_s = jnp.dot(w, ind_ref[...], preferred_element_type=F32)
            acc_sc[...] += a_s * v_s
            return carry

        lax.fori_loop(0, c_len, body, 0)
        for h in range(GLA_HEADS):
            vs = slice(h * GLA_V_DIM, (h + 1) * GLA_V_DIM)
            o_ref[rows, vs] = finalize(oi_sc[rows, vs] + acc_sc[:, vs], rows, vs)

    return [functools.partial(exact_intra, c) for c in range(n_chunks)], overflow


def _ffn_kernel(x_ref, od_ref, og_ref, wod_ref, wog_ref, gf_ref, wg_ref, wu_ref, wd_ref, o_ref):
    h = (x_ref[...]
         + jnp.dot(od_ref[...], wod_ref[...], preferred_element_type=F32)
         + jnp.dot(og_ref[...], wog_ref[...], preferred_element_type=F32))
    m = _rms(h, gf_ref[...]).astype(BF16)
    o_ref[...] = h
    for c0, c1 in FF_CHUNKS:
        gate = jnp.dot(m, wg_ref[:, c0:c1], preferred_element_type=F32)
        up = jnp.dot(m, wu_ref[:, c0:c1], preferred_element_type=F32)
        act = (gate * jax.nn.sigmoid(gate) * up).astype(BF16)
        o_ref[...] += jnp.dot(act, wd_ref[c0:c1, :], preferred_element_type=F32)


def _ffn_call(xf, od, og, wod, wog, gf, wg, wu, wd):
    n = xf.shape[0]
    row = lambda t: (t, 0)
    in_specs = [
        pl.BlockSpec((ROW_TILE, D_MODEL), row),
        pl.BlockSpec((ROW_TILE, DIFF_WIDTH), row),
        pl.BlockSpec((ROW_TILE, GLA_WIDTH), row),
        _const_spec(wod.shape), _const_spec(wog.shape), _const_spec(gf.shape),
        _const_spec(wg.shape), _const_spec(wu.shape), _const_spec(wd.shape),
    ]
    return pl.pallas_call(
        _ffn_kernel,
        out_shape=jax.ShapeDtypeStruct((n, D_MODEL), F32),
        grid=(n // ROW_TILE,),
        in_specs=in_specs,
        out_specs=pl.BlockSpec((ROW_TILE, D_MODEL), row),
        name="out_ffn",
        compiler_params=pltpu.CompilerParams(
            dimension_semantics=("arbitrary",), vmem_limit_bytes=VMEM_LIMIT),
    )(xf, od, og, wod, wog, gf, wg, wu, wd)


def _bf16_pieces(value, n=3):
    pieces = []
    rest = np.float32(value)
    for _ in range(n):
        piece = np.float32(np.asarray(rest, dtype=BF16))
        pieces.append(float(piece))
        rest = np.float32(rest - piece)
    return pieces


def _alibi_tables():
    assert ROW_TILE == Q_TILE == K_TILE
    pos = np.arange(ROW_TILE)
    a = (pos // ALIBI_SPLIT) * ALIBI_SPLIT
    c = pos % ALIBI_SPLIT
    extra = LANES - DIFF_HEAD_DIM
    cq = np.zeros((DIFF_HEADS, extra, ROW_TILE), np.float32)
    ck = np.zeros((DIFF_HEADS, extra, ROW_TILE), np.float32)
    for h in range(DIFF_HEADS):
        r = 0
        for piece in _bf16_pieces(_slope(h) * LOG2E):
            for q_row, k_row in ((piece, a), (piece, c), (-a, piece), (-c, piece)):
                cq[h, r, :] = q_row
                ck[h, r, :] = k_row
                r += 1
        assert r <= extra
    return jnp.asarray(cq, dtype=BF16), jnp.asarray(ck)


def _slope(h):
    return 2.0 ** (-8.0 * (h + 1) / DIFF_HEADS)


def kernel(x, attn_norm_gain, w_in, q_norm_gain, k_norm_gain, lambda_q1, lambda_k1, lambda_q2,
           lambda_k2, diff_out_norm_gain, w_gla_gate_up, b_gla_gate, gla_out_norm_gain, w_out,
           ffn_norm_gain, w_ffn_gate, w_ffn_up, w_ffn_down):
    batch, seq, _ = x.shape
    depth = w_in.shape[0]
    assert seq % ROW_TILE == 0 and seq % GLA_TILE == 0

    cq, ck = _alibi_tables()
    tril = jnp.asarray(np.tril(np.ones((GLA_CHUNK, GLA_CHUNK), np.float32)), dtype=BF16)
    ind = jnp.asarray((np.arange(GLA_KW)[:, None] // GLA_K_DIM
                       == np.arange(GLA_WIDTH)[None, :] // GLA_V_DIM).astype(np.float32), dtype=BF16)
    slope_k = jnp.asarray(
        [sum(_bf16_pieces(_slope(h) * LOG2E)) * K_TILE for h in range(DIFF_HEADS)], F32)

    hcur = x.reshape(batch * seq, D_MODEL)
    for l in range(depth):
        lambda_init = 0.8 - 0.6 * math.exp(-0.3 * l)
        w_main = w_in[l]
        wup = w_gla_gate_up[l].astype(BF16)
        row2 = lambda a: a.reshape(1, -1).astype(F32)
        col2 = lambda a: a.reshape(-1, 1).astype(F32)

        qt, ka, vt, o_gla = _proj_call(
            hcur, row2(attn_norm_gain[l]), w_main, col2(q_norm_gain[l]), col2(k_norm_gain[l]),
            cq, ck, wup, row2(b_gla_gate[l]), tril, ind, row2(gla_out_norm_gain[l]), batch, seq)

        o_diff = _attn_call(slope_k, row2(lambda_q1[l]), row2(lambda_k1[l]), row2(lambda_q2[l]),
                            row2(lambda_k2[l]), row2(diff_out_norm_gain[l]), qt, ka, vt,
                            batch, seq, lambda_init)

        wo = w_out[l].astype(BF16)
        hcur = _ffn_call(hcur, o_diff, o_gla, wo[:DIFF_WIDTH], wo[DIFF_WIDTH:],
                         row2(ffn_norm_gain[l]), w_ffn_gate[l].astype(BF16),
                         w_ffn_up[l].astype(BF16), w_ffn_down[l].astype(BF16))
    return hcur.reshape(batch, seq, D_MODEL)
```

```python
import functools
import math

import jax
import jax.numpy as jnp
import numpy as np
from jax import lax
from jax.experimental import pallas as pl
from jax.experimental.pallas import tpu as pltpu

F32 = jnp.float32
BF16 = jnp.bfloat16

D_MODEL = 1024
DIFF_HEADS = 4
DIFF_HEAD_DIM = 64
DIFF_V_DIM = 128
DIFF_WIDTH = DIFF_HEADS * DIFF_V_DIM
GLA_HEADS = 4
GLA_K_DIM = 64
GLA_V_DIM = 128
GLA_WIDTH = GLA_HEADS * GLA_V_DIM
GLA_KW = GLA_HEADS * GLA_K_DIM
GLA_GATE_RANK = 16
GLA_GATE_TEMP = 16.0
D_FF = 2816
EPS = 1e-6
MAIN_WIDTH = 3072
IN_WIDTH = MAIN_WIDTH + GLA_GATE_RANK

LANES = 128
ROW_TILE = 512
Q_TILE = 512
VT_ROWS = 144
K_TILE = 512
LOOP_UNROLL = 2
HEADS_PER_STEP = 2
GLA_TILE = 512
GLA_CHUNK = 256
MXU_DIM = 256
WEIGHT_CAST_COLS = 512
FF_CHUNKS = ((0, 6 * MXU_DIM), (6 * MXU_DIM, D_FF))
ALIBI_SPLIT = 16
NEG_BIG = -1e30
LOG2E = math.log2(math.e)
EXP_CLAMP = 80.0
VMEM_LIMIT = 56 * 1024 * 1024


def _rms(x, gain):
    return x * lax.rsqrt(jnp.mean(x * x, axis=-1, keepdims=True) + EPS) * gain


def _proj_kernel(x_ref, ga_ref, w_ref, gq_ref, gk_ref, cq_ref, ck_ref,
                 wup_ref, bg_ref, tril_ref, ind_ref, ggain_ref,
                 qt_ref, ka_ref, vt_ref, og_ref,
                 wb_sc, gq_out, gk_out, gv_out, go_out, la_out,
                 st_ref, b_sc, oi_sc, acc_sc, kf_sc, vf_sc):
    @pl.when((pl.program_id(0) == 0) & (pl.program_id(1) == 0))
    def _():
        for c0 in range(0, IN_WIDTH, WEIGHT_CAST_COLS):
            c1 = min(c0 + WEIGHT_CAST_COLS, IN_WIDTH)
            wb_sc[:, c0:c1] = w_ref[c0:c1, :].T.astype(BF16)

    nb = _rms(x_ref[...], ga_ref[...]).astype(BF16)

    def proj(c0, c1):
        return jnp.dot(nb, wb_sc[:, c0:c1], preferred_element_type=F32)

    def head_norm_t(x_t, gain_col):
        ms = jnp.mean(x_t * x_t, axis=0, keepdims=True)
        return x_t * lax.rsqrt(ms + EPS) * gain_col

    gla_cols = proj(1536, IN_WIDTH)
    gq_out[...] = (gla_cols[:, 0:256] * (GLA_K_DIM ** -0.5)).astype(BF16)
    gk_out[...] = gla_cols[:, 256:512].astype(BF16)
    gv_out[...] = gla_cols[:, 512:1024].astype(BF16)
    go_out[...] = gla_cols[:, 1024:1536].astype(BF16)
    glr = gla_cols[:, 1536:]
    z = jnp.dot(glr.astype(BF16), wup_ref[...], preferred_element_type=F32) + bg_ref[...]
    log_sig = jnp.minimum(z, 0.0) - jnp.log(1.0 + jnp.exp(-jnp.abs(z)))
    la_out[...] = log_sig * (1.0 / GLA_GATE_TEMP)

    qkv = []

    def project_qkv():
        full = proj(0, 1536)
        qkv.extend([full[:, 0:512], full[:, 512:1024], full[:, 1024:1536]])

    redo, overflow = _gla_tile(tril_ref, ind_ref, ggain_ref, gq_out, gk_out, gv_out, go_out, la_out,
                               og_ref, st_ref, b_sc, oi_sc, acc_sc, kf_sc, vf_sc,
                               after_cumsum=project_qkv)
    dq, dk, dv = qkv

    d = DIFF_HEAD_DIM
    q_scale = d ** -0.5 * LOG2E
    ones_rows = (lax.broadcasted_iota(jnp.int32, (VT_ROWS - DIFF_V_DIM, ROW_TILE), 0) == 0)
    for h in range(DIFF_HEADS):
        q_t = dq[:, h * LANES:(h + 1) * LANES].T
        k_t = dk[:, h * LANES:(h + 1) * LANES].T
        for j in range(2):
            yq = head_norm_t(q_t[j * d:(j + 1) * d], gq_ref[...]) * q_scale
            yk = head_norm_t(k_t[j * d:(j + 1) * d], gk_ref[...])
            qt_ref[0, h, j, 0, :d, :] = yq.astype(BF16)
            qt_ref[0, h, j, 0, d:, :] = cq_ref[h]
            ka_ref[0, h, j] = jnp.concatenate([yk, ck_ref[h]], axis=0).T.astype(BF16)
        vt_ref[0, h, 0, :DIFF_V_DIM, :] = dv[:, h * LANES:(h + 1) * LANES].T.astype(BF16)
        vt_ref[0, h, 0, DIFF_V_DIM:, :] = ones_rows.astype(BF16)

    for fn, flag in zip(redo, overflow):
        pl.when(flag)(fn)


def _const_spec(shape):
    nd = len(shape)
    return pl.BlockSpec(shape, lambda *_: (0,) * nd, pipeline_mode=pl.Buffered(1))


def _proj_call(xf, ga, w_main, gq, gk, cq, ck, wup, bg, tril, ind, ggain, batch, seq):
    assert ROW_TILE == GLA_TILE
    n_t = seq // ROW_TILE
    n = batch * seq
    row = lambda b, t: (b * n_t + t, 0)
    out_shape = (
        jax.ShapeDtypeStruct((batch, DIFF_HEADS, 2, seq // Q_TILE, LANES, Q_TILE), BF16),
        jax.ShapeDtypeStruct((batch, DIFF_HEADS, 2, seq, LANES), BF16),
        jax.ShapeDtypeStruct((batch, DIFF_HEADS, seq // K_TILE, VT_ROWS, K_TILE), BF16),
        jax.ShapeDtypeStruct((n, GLA_WIDTH), BF16),
    )
    out_specs = (
        pl.BlockSpec((1, DIFF_HEADS, 2, 1, LANES, Q_TILE), lambda b, t: (b, 0, 0, t, 0, 0)),
        pl.BlockSpec((1, DIFF_HEADS, 2, ROW_TILE, LANES), lambda b, t: (b, 0, 0, t, 0)),
        pl.BlockSpec((1, DIFF_HEADS, 1, VT_ROWS, K_TILE), lambda b, t: (b, 0, t, 0, 0)),
        pl.BlockSpec((ROW_TILE, GLA_WIDTH), row),
    )
    consts = (ga, w_main, gq, gk, cq, ck, wup, bg, tril, ind, ggain)
    in_specs = [pl.BlockSpec((ROW_TILE, D_MODEL), row)] + [_const_spec(a.shape) for a in consts]
    scratch_shapes = [
        pltpu.VMEM((D_MODEL, IN_WIDTH), BF16),
        pltpu.VMEM((ROW_TILE, GLA_KW), BF16),
        pltpu.VMEM((ROW_TILE, GLA_KW), BF16),
        pltpu.VMEM((ROW_TILE, GLA_WIDTH), BF16),
        pltpu.VMEM((ROW_TILE, GLA_WIDTH), BF16),
        pltpu.VMEM((ROW_TILE, GLA_KW), F32),
        pltpu.VMEM((GLA_HEADS, GLA_V_DIM, GLA_K_DIM), F32),
        pltpu.VMEM((GLA_TILE, GLA_KW), F32),
        pltpu.VMEM((GLA_TILE, GLA_WIDTH), F32),
        pltpu.VMEM((GLA_CHUNK, GLA_WIDTH), F32),
        pltpu.VMEM((GLA_CHUNK, GLA_KW), F32),
        pltpu.VMEM((GLA_CHUNK, GLA_WIDTH), F32),
    ]
    return pl.pallas_call(
        _proj_kernel, out_shape=out_shape, grid=(batch, n_t),
        in_specs=in_specs, out_specs=out_specs, name="proj_gla",
        scratch_shapes=scratch_shapes,
        compiler_params=pltpu.CompilerParams(
            dimension_semantics=("arbitrary", "arbitrary"), vmem_limit_bytes=VMEM_LIMIT),
    )(xf, *consts)


def _attn_kernel(slope_ref, lq1_ref, lk1_ref, lq2_ref, lk2_ref, gain_ref, qt_ref, k_ref, vt_ref,
                 o_ref, s_sc, cm_sc, m_sc, acc_sc, *, lambda_init):
    hg = pl.program_id(1)
    qi = pl.program_id(2)
    m_sc[...] = jnp.full(m_sc.shape, NEG_BIG, F32)
    acc_sc[...] = jnp.zeros(acc_sc.shape, F32)
    group = range(HEADS_PER_STEP)

    def score1(g, j, kb, masked):
        rows = pl.ds(pl.multiple_of(kb * K_TILE, K_TILE), K_TILE)
        s = jnp.dot(k_ref[0, g, j, rows, :], qt_ref[0, g, j, 0], preferred_element_type=F32)
        if masked:
            key = lax.broadcasted_iota(jnp.int32, s.shape, 0)
            qry = lax.broadcasted_iota(jnp.int32, s.shape, 1)
            s = jnp.where(key <= qry, s, NEG_BIG)
        s_sc[g, j] = s
        cm_sc[g, j] = jnp.max(s, axis=0, keepdims=True)

    def absorb1(g, j, kb):
        off = slope_ref[hg * HEADS_PER_STEP + g] * (kb - qi).astype(F32)
        m_prev = m_sc[g, j]
        m_new = jnp.maximum(m_prev, cm_sc[g, j] + off)
        p = jnp.exp2(s_sc[g, j] - (m_new - off))
        alpha = jnp.exp2(m_prev - m_new)
        acc_sc[g, j] = alpha * acc_sc[g, j] + jnp.dot(vt_ref[0, g, kb], p.astype(BF16),
                                                      preferred_element_type=F32)
        m_sc[g, j] = m_new

    def score(j, kb, masked):
        for g in group:
            score1(g, j, kb, masked)

    def absorb(j, kb):
        for g in group:
            absorb1(g, j, kb)

    @pl.when(qi > 0)
    def _():
        score(0, 0, False)

    def unit(kb):
        score(1, kb, False)
        absorb(0, kb)
        score(0, kb + 1, False)
        absorb(1, kb)

    def body(kp, carry):
        for u in range(LOOP_UNROLL):
            unit(LOOP_UNROLL * kp + u)
        return carry

    n_full = jnp.maximum(qi - 1, 0)
    lax.fori_loop(0, n_full // LOOP_UNROLL, body, 0)

    done = (n_full // LOOP_UNROLL) * LOOP_UNROLL
    size = LOOP_UNROLL // 2
    while size >= 1:
        take = (n_full - done) >= size

        @pl.when(take)
        def _(done=done, size=size):
            for u in range(size):
                unit(done + u)

        done = done + jnp.where(take, size, 0)
        size //= 2

    def finish():
        score(1, qi, True)
        absorb(0, qi)
        absorb(1, qi)
        lam = (jnp.exp(jnp.sum(lq1_ref[...] * lk1_ref[...], axis=1, keepdims=True))
               - jnp.exp(jnp.sum(lq2_ref[...] * lk2_ref[...], axis=1, keepdims=True))
               + lambda_init)
        for g in group:
            a1 = acc_sc[g, 0]
            a2 = acc_sc[g, 1]
            o_t = (a1[:DIFF_V_DIM] / a1[DIFF_V_DIM:DIFF_V_DIM + 1]
                   - lam * (a2[:DIFF_V_DIM] / a2[DIFF_V_DIM:DIFF_V_DIM + 1]))
            o_ref[:, g * DIFF_V_DIM:(g + 1) * DIFF_V_DIM] = (
                _rms(o_t.T, gain_ref[...]) * (1.0 - lambda_init)).astype(BF16)

    @pl.when(qi > 0)
    def _():
        score(1, qi - 1, False)
        absorb(0, qi - 1)
        score(0, qi, True)
        absorb(1, qi - 1)
        finish()

    @pl.when(qi == 0)
    def _():
        score(0, 0, True)
        finish()


def _attn_call(slope_k, lq1, lk1, lq2, lk2, gain, qt, ka, vt, batch, seq, lambda_init):
    n_q = seq // Q_TILE
    n_k = seq // K_TILE
    hps = HEADS_PER_STEP
    smem = pl.BlockSpec(memory_space=pltpu.SMEM)
    in_specs = [
        smem,
        _const_spec(lq1.shape), _const_spec(lk1.shape), _const_spec(lq2.shape),
        _const_spec(lk2.shape), _const_spec(gain.shape),
        pl.BlockSpec((1, hps, 2, 1, LANES, Q_TILE), lambda b, h, q: (b, h, 0, q, 0, 0)),
        pl.BlockSpec((1, hps, 2, seq, LANES), lambda b, h, q: (b, h, 0, 0, 0)),
        pl.BlockSpec((1, hps, n_k, VT_ROWS, K_TILE), lambda b, h, q: (b, h, 0, 0, 0)),
    ]
    return pl.pallas_call(
        functools.partial(_attn_kernel, lambda_init=lambda_init),
        out_shape=jax.ShapeDtypeStruct((batch * seq, DIFF_WIDTH), BF16),
        grid=(batch, DIFF_HEADS // hps, n_q),
        in_specs=in_specs,
        out_specs=pl.BlockSpec((Q_TILE, hps * DIFF_V_DIM), lambda b, h, q: (b * n_q + q, h)),
        scratch_shapes=[pltpu.VMEM((hps, 2, K_TILE, Q_TILE), F32),
                        pltpu.VMEM((hps, 2, 1, Q_TILE), F32),
                        pltpu.VMEM((hps, 2, 1, Q_TILE), F32),
                        pltpu.VMEM((hps, 2, VT_ROWS, Q_TILE), F32)],
        name="diff_attn",
        compiler_params=pltpu.CompilerParams(
            dimension_semantics=("arbitrary", "arbitrary", "arbitrary"),
            vmem_limit_bytes=VMEM_LIMIT),
    )(slope_k, lq1, lk1, lq2, lk2, gain, qt, ka, vt)


def _gla_tile(tril_ref, ind_ref, gain_ref, q_ref, k_ref, v_ref, g_ref, la_ref, o_ref,
              st_ref, b_sc, oi_sc, acc_sc, kf_sc, vf_sc, after_cumsum=None):
    @pl.when(pl.program_id(1) == 0)
    def _():
        st_ref[...] = jnp.zeros(st_ref.shape, F32)

    c_len = GLA_CHUNK
    row = lax.broadcasted_iota(jnp.int32, (c_len, c_len), 0)
    col = lax.broadcasted_iota(jnp.int32, (c_len, c_len), 1)
    causal = col <= row
    nt = (((1,), (1,)), ((), ()))

    def finalize(o, rows, vs):
        g = g_ref[rows, vs].astype(F32)
        return (_rms(o, gain_ref[...]) * (g * jax.nn.sigmoid(g))).astype(BF16)

    n_chunks = GLA_TILE // c_len
    cums = []
    for c in range(n_chunks):
        la = la_ref[c * c_len:(c + 1) * c_len, :]
        hi = la.astype(BF16)
        r1 = la - hi.astype(F32)
        mid = r1.astype(BF16)
        lo = (r1 - mid.astype(F32)).astype(BF16)
        tril = tril_ref[...]
        cums.append(jnp.dot(tril, hi, preferred_element_type=F32)
                    + jnp.dot(tril, mid, preferred_element_type=F32)
                    + jnp.dot(tril, lo, preferred_element_type=F32))

    if after_cumsum is not None:
        after_cumsum()

    overflow = []
    for c in range(n_chunks):
        rows = slice(c * c_len, (c + 1) * c_len)
        b = cums[c]
        b_sc[rows, :] = b
        b_last = b[c_len - 1:c_len, :]
        overflow.append(jnp.max(-b_last) > EXP_CLAMP)
        q = q_ref[rows, :].astype(F32)
        k = k_ref[rows, :].astype(F32)
        q_in = (q * jnp.exp(b)).astype(BF16)
        k_out = (k * jnp.exp(jnp.minimum(-b, EXP_CLAMP))).astype(BF16)
        k_dec = (k * jnp.exp(b_last - b)).astype(BF16)
        d_last = jnp.exp(b_last)
        for h in range(GLA_HEADS):
            ks = slice(h * GLA_K_DIM, (h + 1) * GLA_K_DIM)
            vs = slice(h * GLA_V_DIM, (h + 1) * GLA_V_DIM)
            a = lax.dot_general(q_in[:, ks], k_out[:, ks], nt, preferred_element_type=F32)
            a = jnp.where(causal, a, 0.0).astype(BF16)
            vh = v_ref[rows, vs]
            s_t = st_ref[h]
            o_inter = lax.dot_general(q_in[:, ks], s_t.astype(BF16), nt,
                                      preferred_element_type=F32)
            oi_sc[rows, vs] = o_inter
            o = o_inter + jnp.dot(a, vh, preferred_element_type=F32)
            v_t = vh.astype(F32).T.astype(BF16)
            st_ref[h] = s_t * d_last[:, ks] + jnp.dot(v_t, k_dec[:, ks], preferred_element_type=F32)
            o_ref[rows, vs] = finalize(o, rows, vs)

    def exact_intra(c):
        rows = slice(c * c_len, (c + 1) * c_len)
        q = q_ref[rows, :].astype(F32)
        b = b_sc[rows, :]
        kf_sc[...] = k_ref[rows, :].astype(F32)
        vf_sc[...] = v_ref[rows, :].astype(F32)
        acc_sc[...] = jnp.zeros(acc_sc.shape, F32)
        t_idx = lax.broadcasted_iota(jnp.int32, (c_len, GLA_KW), 0)

        def body(s, carry):
            k_s = kf_sc[pl.ds(s, 1), :]
            b_s = b_sc[pl.ds(c * c_len + s, 1), :]
            v_s = vf_sc[pl.ds(s, 1), :]
            w = q * k_s * jnp.exp(jnp.minimum(b - b_s, 0.0))
            w = jnp.where(t_idx >= s, w, 0.0).astype(BF16)
            a_s = jnp.dot(w, ind_ref[...], preferred_element_type=F32)
            acc_sc[...] += a_s * v_s
            return carry

        lax.fori_loop(0, c_len, body, 0)
        for h in range(GLA_HEADS):
            vs = slice(h * GLA_V_DIM, (h + 1) * GLA_V_DIM)
            o_ref[rows, vs] = finalize(oi_sc[rows, vs] + acc_sc[:, vs], rows, vs)

    return [functools.partial(exact_intra, c) for c in range(n_chunks)], overflow


def _ffn_kernel(x_ref, od_ref, og_ref, wod_ref, wog_ref, gf_ref, wg_ref, wu_ref, wd_ref, o_ref):
    h = (x_ref[...]
         + jnp.dot(od_ref[...], wod_ref[...], preferred_element_type=F32)
         + jnp.dot(og_ref[...], wog_ref[...], preferred_element_type=F32))
    m = _rms(h, gf_ref[...]).astype(BF16)
    o_ref[...] = h
    for c0, c1 in FF_CHUNKS:
        gate = jnp.dot(m, wg_ref[:, c0:c1], preferred_element_type=F32)
        up = jnp.dot(m, wu_ref[:, c0:c1], preferred_element_type=F32)
        act = (gate * jax.nn.sigmoid(gate) * up).astype(BF16)
        o_ref[...] += jnp.dot(act, wd_ref[c0:c1, :], preferred_element_type=F32)


def _ffn_call(xf, od, og, wod, wog, gf, wg, wu, wd):
    n = xf.shape[0]
    row = lambda t: (t, 0)
    in_specs = [
        pl.BlockSpec((ROW_TILE, D_MODEL), row),
        pl.BlockSpec((ROW_TILE, DIFF_WIDTH), row),
        pl.BlockSpec((ROW_TILE, GLA_WIDTH), row),
        _const_spec(wod.shape), _const_spec(wog.shape), _const_spec(gf.shape),
        _const_spec(wg.shape), _const_spec(wu.shape), _const_spec(wd.shape),
    ]
    return pl.pallas_call(
        _ffn_kernel,
        out_shape=jax.ShapeDtypeStruct((n, D_MODEL), F32),
        grid=(n // ROW_TILE,),
        in_specs=in_specs,
        out_specs=pl.BlockSpec((ROW_TILE, D_MODEL), row),
        name="out_ffn",
        compiler_params=pltpu.CompilerParams(
            dimension_semantics=("arbitrary",), vmem_limit_bytes=VMEM_LIMIT),
    )(xf, od, og, wod, wog, gf, wg, wu, wd)


def _bf16_pieces(value, n=3):
    pieces = []
    rest = np.float32(value)
    for _ in range(n):
        piece = np.float32(np.asarray(rest, dtype=BF16))
        pieces.append(float(piece))
        rest = np.float32(rest - piece)
    return pieces


def _alibi_tables():
    assert ROW_TILE == Q_TILE == K_TILE
    pos = np.arange(ROW_TILE)
    a = (pos // ALIBI_SPLIT) * ALIBI_SPLIT
    c = pos % ALIBI_SPLIT
    extra = LANES - DIFF_HEAD_DIM
    cq = np.zeros((DIFF_HEADS, extra, ROW_TILE), np.float32)
    ck = np.zeros((DIFF_HEADS, extra, ROW_TILE), np.float32)
    for h in range(DIFF_HEADS):
        r = 0
        for piece in _bf16_pieces(_slope(h) * LOG2E):
            for q_row, k_row in ((piece, a), (piece, c), (-a, piece), (-c, piece)):
                cq[h, r, :] = q_row
                ck[h, r, :] = k_row
                r += 1
        assert r <= extra
    return jnp.asarray(cq, dtype=BF16), jnp.asarray(ck)


def _slope(h):
    return 2.0 ** (-8.0 * (h + 1) / DIFF_HEADS)


def kernel(x, attn_norm_gain, w_in, q_norm_gain, k_norm_gain, lambda_q1, lambda_k1, lambda_q2,
           lambda_k2, diff_out_norm_gain, w_gla_gate_up, b_gla_gate, gla_out_norm_gain, w_out,
           ffn_norm_gain, w_ffn_gate, w_ffn_up, w_ffn_down):
    batch, seq, _ = x.shape
    depth = w_in.shape[0]
    assert seq % ROW_TILE == 0 and seq % GLA_TILE == 0

    cq, ck = _alibi_tables()
    tril = jnp.asarray(np.tril(np.ones((GLA_CHUNK, GLA_CHUNK), np.float32)), dtype=BF16)
    ind = jnp.asarray((np.arange(GLA_KW)[:, None] // GLA_K_DIM
                       == np.arange(GLA_WIDTH)[None, :] // GLA_V_DIM).astype(np.float32), dtype=BF16)
    slope_k = jnp.asarray(
        [sum(_bf16_pieces(_slope(h) * LOG2E)) * K_TILE for h in range(DIFF_HEADS)], F32)

    hcur = x.reshape(batch * seq, D_MODEL)
    for l in range(depth):
        lambda_init = 0.8 - 0.6 * math.exp(-0.3 * l)
        w_main = w_in[l].T
        wup = w_gla_gate_up[l].astype(BF16)
        row2 = lambda a: a.reshape(1, -1).astype(F32)
        col2 = lambda a: a.reshape(-1, 1).astype(F32)

        qt, ka, vt, o_gla = _proj_call(
            hcur, row2(attn_norm_gain[l]), w_main, col2(q_norm_gain[l]), col2(k_norm_gain[l]),
            cq, ck, wup, row2(b_gla_gate[l]), tril, ind, row2(gla_out_norm_gain[l]), batch, seq)

        o_diff = _attn_call(slope_k, row2(lambda_q1[l]), row2(lambda_k1[l]), row2(lambda_q2[l]),
                            row2(lambda_k2[l]), row2(diff_out_norm_gain[l]), qt, ka, vt,
                            batch, seq, lambda_init)

        wo = w_out[l].astype(BF16)
        hcur = _ffn_call(hcur, o_diff, o_gla, wo[:DIFF_WIDTH], wo[DIFF_WIDTH:],
                         row2(ffn_norm_gain[l]), w_ffn_gate[l].astype(BF16),
                         w_ffn_up[l].astype(BF16), w_ffn_down[l].astype(BF16))
    return hcur.reshape(batch, seq, D_MODEL)
```

```python
import functools
import math

import jax
import jax.numpy as jnp
import numpy as np
from jax import lax
from jax.experimental import pallas as pl
from jax.experimental.pallas import tpu as pltpu

F32 = jnp.float32
BF16 = jnp.bfloat16

D_MODEL = 1024
DIFF_HEADS = 4
DIFF_HEAD_DIM = 64
DIFF_V_DIM = 128
DIFF_WIDTH = DIFF_HEADS * DIFF_V_DIM
GLA_HEADS = 4
GLA_K_DIM = 64
GLA_V_DIM = 128
GLA_WIDTH = GLA_HEADS * GLA_V_DIM
GLA_KW = GLA_HEADS * GLA_K_DIM
GLA_GATE_RANK = 16
GLA_GATE_TEMP = 16.0
D_FF = 2816
EPS = 1e-6
MAIN_WIDTH = 3072
IN_WIDTH = MAIN_WIDTH + GLA_GATE_RANK

LANES = 128
ROW_TILE = 512
Q_TILE = 512
VT_ROWS = 144
K_TILE = 512
LOOP_UNROLL = 2
HEADS_PER_STEP = 2
GLA_TILE = 512
GLA_CHUNK = 256
MXU_DIM = 256
WEIGHT_CAST_COLS = 512
FF_CHUNKS = ((0, 6 * MXU_DIM), (6 * MXU_DIM, D_FF))
ALIBI_SPLIT = 16
NEG_BIG = -1e30
LOG2E = math.log2(math.e)
EXP_CLAMP = 80.0
VMEM_LIMIT = 56 * 1024 * 1024


def _rms(x, gain):
    return x * lax.rsqrt(jnp.mean(x * x, axis=-1, keepdims=True) + EPS) * gain


def _proj_kernel(x_ref, ga_ref, w_ref, gq_ref, gk_ref, cq_ref, ck_ref,
                 wup_ref, bg_ref, tril_ref, ind_ref, ggain_ref,
                 qt_ref, ka_ref, vt_ref, og_ref,
                 wb_sc, gq_out, gk_out, gv_out, go_out, la_out,
                 st_ref, b_sc, oi_sc, acc_sc, kf_sc, vf_sc):
    @pl.when((pl.program_id(0) == 0) & (pl.program_id(1) == 0))
    def _():
        for c0 in range(0, IN_WIDTH, WEIGHT_CAST_COLS):
            c1 = min(c0 + WEIGHT_CAST_COLS, IN_WIDTH)
            wb_sc[:, c0:c1] = w_ref[c0:c1, :].T.astype(BF16)

    nb = _rms(x_ref[...], ga_ref[...]).astype(BF16)

    def proj(c0, c1):
        return jnp.dot(nb, wb_sc[:, c0:c1], preferred_element_type=F32)

    def head_norm_t(x_t, gain_col):
        ms = jnp.mean(x_t * x_t, axis=0, keepdims=True)
        return x_t * lax.rsqrt(ms + EPS) * gain_col

    gla_cols = proj(1536, IN_WIDTH)
    gq_out[...] = (gla_cols[:, 0:256] * (GLA_K_DIM ** -0.5)).astype(BF16)
    gk_out[...] = gla_cols[:, 256:512].astype(BF16)
    gv_out[...] = gla_cols[:, 512:1024].astype(BF16)
    go_out[...] = gla_cols[:, 1024:1536].astype(BF16)
    glr = gla_cols[:, 1536:]
    z = jnp.dot(glr.astype(BF16), wup_ref[...], preferred_element_type=F32) + bg_ref[...]
    log_sig = jnp.minimum(z, 0.0) - jnp.log(1.0 + jnp.exp(-jnp.abs(z)))
    la_out[...] = log_sig * (1.0 / GLA_GATE_TEMP)

    qkv = []

    def project_qkv():
        full = proj(0, 1536)
        qkv.extend([full[:, 0:512], full[:, 512:1024], full[:, 1024:1536]])

    redo, overflow = _gla_tile(tril_ref, ind_ref, ggain_ref, gq_out, gk_out, gv_out, go_out, la_out,
                               og_ref, st_ref, b_sc, oi_sc, acc_sc, kf_sc, vf_sc,
                               after_cumsum=project_qkv)
    dq, dk, dv = qkv

    d = DIFF_HEAD_DIM
    q_scale = d ** -0.5 * LOG2E
    ones_rows = (lax.broadcasted_iota(jnp.int32, (VT_ROWS - DIFF_V_DIM, ROW_TILE), 0) == 0)
    for h in range(DIFF_HEADS):
        q_t = dq[:, h * LANES:(h + 1) * LANES].T
        k_t = dk[:, h * LANES:(h + 1) * LANES].T
        for j in range(2):
            yq = head_norm_t(q_t[j * d:(j + 1) * d], gq_ref[...]) * q_scale
            yk = head_norm_t(k_t[j * d:(j + 1) * d], gk_ref[...])
            qt_ref[0, h, j, 0, :d, :] = yq.astype(BF16)
            qt_ref[0, h, j, 0, d:, :] = cq_ref[h]
            ka_ref[0, h, j] = jnp.concatenate([yk, ck_ref[h]], axis=0).T.astype(BF16)
        vt_ref[0, h, 0, :DIFF_V_DIM, :] = dv[:, h * LANES:(h + 1) * LANES].T.astype(BF16)
        vt_ref[0, h, 0, DIFF_V_DIM:, :] = ones_rows.astype(BF16)

    for fn, flag in zip(redo, overflow):
        pl.when(flag)(fn)


def _const_spec(shape):
    nd = len(shape)
    return pl.BlockSpec(shape, lambda *_: (0,) * nd, pipeline_mode=pl.Buffered(1))


def _proj_call(xf, ga, w_main, gq, gk, cq, ck, wup, bg, tril, ind, ggain, batch, seq):
    assert ROW_TILE == GLA_TILE
    n_t = seq // ROW_TILE
    n = batch * seq
    row = lambda b, t: (b * n_t + t, 0)
    out_shape = (
        jax.ShapeDtypeStruct((batch, DIFF_HEADS, 2, seq // Q_TILE, LANES, Q_TILE), BF16),
        jax.ShapeDtypeStruct((batch, DIFF_HEADS, 2, seq, LANES), BF16),
        jax.ShapeDtypeStruct((batch, DIFF_HEADS, seq // K_TILE, VT_ROWS, K_TILE), BF16),
        jax.ShapeDtypeStruct((n, GLA_WIDTH), BF16),
    )
    out_specs = (
        pl.BlockSpec((1, DIFF_HEADS, 2, 1, LANES, Q_TILE), lambda b, t: (b, 0, 0, t, 0, 0)),
        pl.BlockSpec((1, DIFF_HEADS, 2, ROW_TILE, LANES), lambda b, t: (b, 0, 0, t, 0)),
        pl.BlockSpec((1, DIFF_HEADS, 1, VT_ROWS, K_TILE), lambda b, t: (b, 0, t, 0, 0)),
        pl.BlockSpec((ROW_TILE, GLA_WIDTH), row),
    )
    consts = (ga, w_main, gq, gk, cq, ck, wup, bg, tril, ind, ggain)
    in_specs = [pl.BlockSpec((ROW_TILE, D_MODEL), row)] + [_const_spec(a.shape) for a in consts]
    scratch_shapes = [
        pltpu.VMEM((D_MODEL, IN_WIDTH), BF16),
        pltpu.VMEM((ROW_TILE, GLA_KW), BF16),
        pltpu.VMEM((ROW_TILE, GLA_KW), BF16),
        pltpu.VMEM((ROW_TILE, GLA_WIDTH), BF16),
        pltpu.VMEM((ROW_TILE, GLA_WIDTH), BF16),
        pltpu.VMEM((ROW_TILE, GLA_KW), F32),
        pltpu.VMEM((GLA_HEADS, GLA_V_DIM, GLA_K_DIM), F32),
        pltpu.VMEM((GLA_TILE, GLA_KW), F32),
        pltpu.VMEM((GLA_TILE, GLA_WIDTH), F32),
        pltpu.VMEM((GLA_CHUNK, GLA_WIDTH), F32),
        pltpu.VMEM((GLA_CHUNK, GLA_KW), F32),
        pltpu.VMEM((GLA_CHUNK, GLA_WIDTH), F32),
    ]
    return pl.pallas_call(
        _proj_kernel, out_shape=out_shape, grid=(batch, n_t),
        in_specs=in_specs, out_specs=out_specs, name="proj_gla",
        scratch_shapes=scratch_shapes,
        compiler_params=pltpu.CompilerParams(
            dimension_semantics=("arbitrary", "arbitrary"), vmem_limit_bytes=VMEM_LIMIT),
    )(xf, *consts)


def _attn_kernel(slope_ref, lq1_ref, lk1_ref, lq2_ref, lk2_ref, gain_ref, qt_ref, qn_ref, k_ref,
                 vt_ref, o_ref, s_sc, cm_sc, m_sc, acc_sc, *, lambda_init):
    hg = pl.program_id(1)
    qi = pl.program_id(2)
    m_sc[...] = jnp.full(m_sc.shape, NEG_BIG, F32)
    acc_sc[...] = jnp.zeros(acc_sc.shape, F32)
    group = range(HEADS_PER_STEP)

    def score1(g, j, kb, masked, q_src=qt_ref):
        rows = pl.ds(pl.multiple_of(kb * K_TILE, K_TILE), K_TILE)
        s = jnp.dot(k_ref[0, g, j, rows, :], q_src[0, g, j, 0], preferred_element_type=F32)
        if masked:
            key = lax.broadcasted_iota(jnp.int32, s.shape, 0)
            qry = lax.broadcasted_iota(jnp.int32, s.shape, 1)
            s = jnp.where(key <= qry, s, NEG_BIG)
        s_sc[g, j] = s
        cm_sc[g, j] = jnp.max(s, axis=0, keepdims=True)

    def absorb1(g, j, kb):
        off = slope_ref[hg * HEADS_PER_STEP + g] * (kb - qi).astype(F32)
        m_prev = m_sc[g, j]
        m_new = jnp.maximum(m_prev, cm_sc[g, j] + off)
        p = jnp.exp2(s_sc[g, j] - (m_new - off))
        alpha = jnp.exp2(m_prev - m_new)
        acc_sc[g, j] = alpha * acc_sc[g, j] + jnp.dot(vt_ref[0, g, kb], p.astype(BF16),
                                                      preferred_element_type=F32)
        m_sc[g, j] = m_new

    def score(j, kb, masked):
        for g in group:
            score1(g, j, kb, masked)

    def absorb(j, kb):
        for g in group:
            absorb1(g, j, kb)

    def unit(kb):
        score(1, kb, False)
        absorb(0, kb)
        score(0, kb + 1, False)
        absorb(1, kb)

    def body(kp, carry):
        for u in range(LOOP_UNROLL):
            unit(LOOP_UNROLL * kp + u)
        return carry

    n_full = jnp.maximum(qi - 1, 0)
    lax.fori_loop(0, n_full // LOOP_UNROLL, body, 0)

    done = (n_full // LOOP_UNROLL) * LOOP_UNROLL
    size = LOOP_UNROLL // 2
    while size >= 1:
        take = (n_full - done) >= size

        @pl.when(take)
        def _(done=done, size=size):
            for u in range(size):
                unit(done + u)

        done = done + jnp.where(take, size, 0)
        size //= 2

    def finish():
        score(1, qi, True)
        absorb(0, qi)
        absorb(1, qi)
        for g in group:
            score1(g, 0, 0, False, q_src=qn_ref)
        lam = (jnp.exp(jnp.sum(lq1_ref[...] * lk1_ref[...], axis=1, keepdims=True))
               - jnp.exp(jnp.sum(lq2_ref[...] * lk2_ref[...], axis=1, keepdims=True))
               + lambda_init)
        for g in group:
            a1 = acc_sc[g, 0]
            a2 = acc_sc[g, 1]
            o_t = (a1[:DIFF_V_DIM] / a1[DIFF_V_DIM:DIFF_V_DIM + 1]
                   - lam * (a2[:DIFF_V_DIM] / a2[DIFF_V_DIM:DIFF_V_DIM + 1]))
            o_ref[:, g * DIFF_V_DIM:(g + 1) * DIFF_V_DIM] = (
                _rms(o_t.T, gain_ref[...]) * (1.0 - lambda_init)).astype(BF16)

    @pl.when(qi > 0)
    def _():
        score(1, qi - 1, False)
        absorb(0, qi - 1)
        score(0, qi, True)
        absorb(1, qi - 1)
        finish()

    @pl.when(qi == 0)
    def _():
        score(0, 0, True)
        finish()


def _attn_call(slope_k, lq1, lk1, lq2, lk2, gain, qt, ka, vt, batch, seq, lambda_init):
    n_q = seq // Q_TILE
    n_k = seq // K_TILE
    hps = HEADS_PER_STEP
    smem = pl.BlockSpec(memory_space=pltpu.SMEM)
    in_specs = [
        smem,
        _const_spec(lq1.shape), _const_spec(lk1.shape), _const_spec(lq2.shape),
        _const_spec(lk2.shape), _const_spec(gain.shape),
        pl.BlockSpec((1, hps, 2, 1, LANES, Q_TILE), lambda b, h, q: (b, h, 0, q, 0, 0)),
        pl.BlockSpec((1, hps, 2, 1, LANES, Q_TILE),
                     lambda b, h, q: (b, h, 0, jnp.minimum(q + 1, n_q - 1), 0, 0)),
        pl.BlockSpec((1, hps, 2, seq, LANES), lambda b, h, q: (b, h, 0, 0, 0)),
        pl.BlockSpec((1, hps, n_k, VT_ROWS, K_TILE), lambda b, h, q: (b, h, 0, 0, 0)),
    ]
    return pl.pallas_call(
        functools.partial(_attn_kernel, lambda_init=lambda_init),
        out_shape=jax.ShapeDtypeStruct((batch * seq, DIFF_WIDTH), BF16),
        grid=(batch, DIFF_HEADS // hps, n_q),
        in_specs=in_specs,
        out_specs=pl.BlockSpec((Q_TILE, hps * DIFF_V_DIM), lambda b, h, q: (b * n_q + q, h)),
        scratch_shapes=[pltpu.VMEM((hps, 2, K_TILE, Q_TILE), F32),
                        pltpu.VMEM((hps, 2, 1, Q_TILE), F32),
                        pltpu.VMEM((hps, 2, 1, Q_TILE), F32),
                        pltpu.VMEM((hps, 2, VT_ROWS, Q_TILE), F32)],
        name="diff_attn",
        compiler_params=pltpu.CompilerParams(
            dimension_semantics=("arbitrary", "arbitrary", "arbitrary"),
            vmem_limit_bytes=VMEM_LIMIT),
    )(slope_k, lq1, lk1, lq2, lk2, gain, qt, qt, ka, vt)


def _gla_tile(tril_ref, ind_ref, gain_ref, q_ref, k_ref, v_ref, g_ref, la_ref, o_ref,
              st_ref, b_sc, oi_sc, acc_sc, kf_sc, vf_sc, after_cumsum=None):
    @pl.when(pl.program_id(1) == 0)
    def _():
        st_ref[...] = jnp.zeros(st_ref.shape, F32)

    c_len = GLA_CHUNK
    row = lax.broadcasted_iota(jnp.int32, (c_len, c_len), 0)
    col = lax.broadcasted_iota(jnp.int32, (c_len, c_len), 1)
    causal = col <= row
    nt = (((1,), (1,)), ((), ()))

    def finalize(o, rows, vs):
        g = g_ref[rows, vs].astype(F32)
        return (_rms(o, gain_ref[...]) * (g * jax.nn.sigmoid(g))).astype(BF16)

    n_chunks = GLA_TILE // c_len
    cums = []
    for c in range(n_chunks):
        la = la_ref[c * c_len:(c + 1) * c_len, :]
        hi = la.astype(BF16)
        r1 = la - hi.astype(F32)
        mid = r1.astype(BF16)
        lo = (r1 - mid.astype(F32)).astype(BF16)
        tril = tril_ref[...]
        cums.append(jnp.dot(tril, hi, preferred_element_type=F32)
                    + jnp.dot(tril, mid, preferred_element_type=F32)
                    + jnp.dot(tril, lo, preferred_element_type=F32))

    if after_cumsum is not None:
        after_cumsum()

    overflow = []
    for c in range(n_chunks):
        rows = slice(c * c_len, (c + 1) * c_len)
        b = cums[c]
        b_sc[rows, :] = b
        b_last = b[c_len - 1:c_len, :]
        overflow.append(jnp.max(-b_last) > EXP_CLAMP)
        q = q_ref[rows, :].astype(F32)
        k = k_ref[rows, :].astype(F32)
        q_in = (q * jnp.exp(b)).astype(BF16)
        k_out = (k * jnp.exp(jnp.minimum(-b, EXP_CLAMP))).astype(BF16)
        k_dec = (k * jnp.exp(b_last - b)).astype(BF16)
        d_last = jnp.exp(b_last)
        for h in range(GLA_HEADS):
            ks = slice(h * GLA_K_DIM, (h + 1) * GLA_K_DIM)
            vs = slice(h * GLA_V_DIM, (h + 1) * GLA_V_DIM)
            a = lax.dot_general(q_in[:, ks], k_out[:, ks], nt, preferred_element_type=F32)
            a = jnp.where(causal, a, 0.0).astype(BF16)
            vh = v_ref[rows, vs]
            s_t = st_ref[h]
            o_inter = lax.dot_general(q_in[:, ks], s_t.astype(BF16), nt,
                                      preferred_element_type=F32)
            oi_sc[rows, vs] = o_inter
            o = o_inter + jnp.dot(a, vh, preferred_element_type=F32)
            v_t = vh.astype(F32).T.astype(BF16)
            st_ref[h] = s_t * d_last[:, ks] + jnp.dot(v_t, k_dec[:, ks], preferred_element_type=F32)
            o_ref[rows, vs] = finalize(o, rows, vs)

    def exact_intra(c):
        rows = slice(c * c_len, (c + 1) * c_len)
        q = q_ref[rows, :].astype(F32)
        b = b_sc[rows, :]
        kf_sc[...] = k_ref[rows, :].astype(F32)
        vf_sc[...] = v_ref[rows, :].astype(F32)
        acc_sc[...] = jnp.zeros(acc_sc.shape, F32)
        t_idx = lax.broadcasted_iota(jnp.int32, (c_len, GLA_KW), 0)

        def body(s, carry):
            k_s = kf_sc[pl.ds(s, 1), :]
            b_s = b_sc[pl.ds(c * c_len + s, 1), :]
            v_s = vf_sc[pl.ds(s, 1), :]
            w = q * k_s * jnp.exp(jnp.minimum(b - b_s, 0.0))
            w = jnp.where(t_idx >= s, w, 0.0).astype(BF16)
            a_s = jnp.dot(w, ind_ref[...], preferred_element_type=F32)
            acc_sc[...] += a_s * v_s
            return carry

        lax.fori_loop(0, c_len, body, 0)
        for h in range(GLA_HEADS):
            vs = slice(h * GLA_V_DIM, (h + 1) * GLA_V_DIM)
            o_ref[rows, vs] = finalize(oi_sc[rows, vs] + acc_sc[:, vs], rows, vs)

    return [functools.partial(exact_intra, c) for c in range(n_chunks)], overflow


def _ffn_kernel(x_ref, od_ref, og_ref, wod_ref, wog_ref, gf_ref, wg_ref, wu_ref, wd_ref, o_ref):
    h = (x_ref[...]
         + jnp.dot(od_ref[...], wod_ref[...], preferred_element_type=F32)
         + jnp.dot(og_ref[...], wog_ref[...], preferred_element_type=F32))
    m = _rms(h, gf_ref[...]).astype(BF16)
    o_ref[...] = h
    for c0, c1 in FF_CHUNKS:
        gate = jnp.dot(m, wg_ref[:, c0:c1], preferred_element_type=F32)
        up = jnp.dot(m, wu_ref[:, c0:c1], preferred_element_type=F32)
        act = (gate * jax.nn.sigmoid(gate) * up).astype(BF16)
        o_ref[...] += jnp.dot(act, wd_ref[c0:c1, :], preferred_element_type=F32)


def _ffn_call(xf, od, og, wod, wog, gf, wg, wu, wd):
    n = xf.shape[0]
    row = lambda t: (t, 0)
    in_specs = [
        pl.BlockSpec((ROW_TILE, D_MODEL), row),
        pl.BlockSpec((ROW_TILE, DIFF_WIDTH), row),
        pl.BlockSpec((ROW_TILE, GLA_WIDTH), row),
        _const_spec(wod.shape), _const_spec(wog.shape), _const_spec(gf.shape),
        _const_spec(wg.shape), _const_spec(wu.shape), _const_spec(wd.shape),
    ]
    return pl.pallas_call(
        _ffn_kernel,
        out_shape=jax.ShapeDtypeStruct((n, D_MODEL), F32),
        grid=(n // ROW_TILE,),
        in_specs=in_specs,
        out_specs=pl.BlockSpec((ROW_TILE, D_MODEL), row),
        name="out_ffn",
        compiler_params=pltpu.CompilerParams(
            dimension_semantics=("arbitrary",), vmem_limit_bytes=VMEM_LIMIT),
    )(xf, od, og, wod, wog, gf, wg, wu, wd)


def _bf16_pieces(value, n=3):
    pieces = []
    rest = np.float32(value)
    for _ in range(n):
        piece = np.float32(np.asarray(rest, dtype=BF16))
        pieces.append(float(piece))
        rest = np.float32(rest - piece)
    return pieces


def _alibi_tables():
    assert ROW_TILE == Q_TILE == K_TILE
    pos = np.arange(ROW_TILE)
    a = (pos // ALIBI_SPLIT) * ALIBI_SPLIT
    c = pos % ALIBI_SPLIT
    extra = LANES - DIFF_HEAD_DIM
    cq = np.zeros((DIFF_HEADS, extra, ROW_TILE), np.float32)
    ck = np.zeros((DIFF_HEADS, extra, ROW_TILE), np.float32)
    for h in range(DIFF_HEADS):
        r = 0
        for piece in _bf16_pieces(_slope(h) * LOG2E):
            for q_row, k_row in ((piece, a), (piece, c), (-a, piece), (-c, piece)):
                cq[h, r, :] = q_row
                ck[h, r, :] = k_row
                r += 1
        assert r <= extra
    return jnp.asarray(cq, dtype=BF16), jnp.asarray(ck)


def _slope(h):
    return 2.0 ** (-8.0 * (h + 1) / DIFF_HEADS)


def kernel(x, attn_norm_gain, w_in, q_norm_gain, k_norm_gain, lambda_q1, lambda_k1, lambda_q2,
           lambda_k2, diff_out_norm_gain, w_gla_gate_up, b_gla_gate, gla_out_norm_gain, w_out,
           ffn_norm_gain, w_ffn_gate, w_ffn_up, w_ffn_down):
    batch, seq, _ = x.shape
    depth = w_in.shape[0]
    assert seq % ROW_TILE == 0 and seq % GLA_TILE == 0

    cq, ck = _alibi_tables()
    tril = jnp.asarray(np.tril(np.ones((GLA_CHUNK, GLA_CHUNK), np.float32)), dtype=BF16)
    ind = jnp.asarray((np.arange(GLA_KW)[:, None] // GLA_K_DIM
                       == np.arange(GLA_WIDTH)[None, :] // GLA_V_DIM).astype(np.float32), dtype=BF16)
    slope_k = jnp.asarray(
        [sum(_bf16_pieces(_slope(h) * LOG2E)) * K_TILE for h in range(DIFF_HEADS)], F32)

    hcur = x.reshape(batch * seq, D_MODEL)
    for l in range(depth):
        lambda_init = 0.8 - 0.6 * math.exp(-0.3 * l)
        w_main = w_in[l].T
        wup = w_gla_gate_up[l].astype(BF16)
        row2 = lambda a: a.reshape(1, -1).astype(F32)
        col2 = lambda a: a.reshape(-1, 1).astype(F32)

        qt, ka, vt, o_gla = _proj_call(
            hcur, row2(attn_norm_gain[l]), w_main, col2(q_norm_gain[l]), col2(k_norm_gain[l]),
            cq, ck, wup, row2(b_gla_gate[l]), tril, ind, row2(gla_out_norm_gain[l]), batch, seq)

        o_diff = _attn_call(slope_k, row2(lambda_q1[l]), row2(lambda_k1[l]), row2(lambda_q2[l]),
                            row2(lambda_k2[l]), row2(diff_out_norm_gain[l]), qt, ka, vt,
                            batch, seq, lambda_init)

        wo = w_out[l].astype(BF16)
        hcur = _ffn_call(hcur, o_diff, o_gla, wo[:DIFF_WIDTH], wo[DIFF_WIDTH:],
                         row2(ffn_norm_gain[l]), w_ffn_gate[l].astype(BF16),
                         w_ffn_up[l].astype(BF16), w_ffn_down[l].astype(BF16))
    return hcur.reshape(batch, seq, D_MODEL)
```

```python
import functools
import math

import jax
import jax.numpy as jnp
import numpy as np
from jax import lax
from jax.experimental import pallas as pl
from jax.experimental.pallas import tpu as pltpu

F32 = jnp.float32
BF16 = jnp.bfloat16

D_MODEL = 1024
DIFF_HEADS = 4
DIFF_HEAD_DIM = 64
DIFF_V_DIM = 128
DIFF_WIDTH = DIFF_HEADS * DIFF_V_DIM
GLA_HEADS = 4
GLA_K_DIM = 64
GLA_V_DIM = 128
GLA_WIDTH = GLA_HEADS * GLA_V_DIM
GLA_KW = GLA_HEADS * GLA_K_DIM
GLA_GATE_RANK = 16
GLA_GATE_TEMP = 16.0
D_FF = 2816
EPS = 1e-6
MAIN_WIDTH = 3072
IN_WIDTH = MAIN_WIDTH + GLA_GATE_RANK

LANES = 128
ROW_TILE = 512
Q_TILE = 512
VT_ROWS = 144
K_TILE = 512
LOOP_UNROLL = 4
HEADS_PER_STEP = 2
GLA_TILE = 512
GLA_CHUNK = 256
MXU_DIM = 256
WEIGHT_CAST_COLS = 512
FFN_WARM = 8
FF_CHUNKS = ((0, 6 * MXU_DIM), (6 * MXU_DIM, D_FF))
ALIBI_SPLIT = 16
NEG_BIG = -1e30
LOG2E = math.log2(math.e)
EXP_CLAMP = 80.0
VMEM_LIMIT = 56 * 1024 * 1024


def _rms(x, gain):
    return x * lax.rsqrt(jnp.mean(x * x, axis=-1, keepdims=True) + EPS) * gain


def _proj_kernel(x_ref, ga_ref, w_ref, gq_ref, gk_ref, cq_ref, ck_ref,
                 wup_ref, bg_ref, tril_ref, ind_ref, ggain_ref,
                 qt_ref, ka_ref, vt_ref, og_ref,
                 wb_sc, gq_out, gk_out, gv_out, go_out, la_out,
                 st_ref, b_sc, oi_sc, acc_sc, kf_sc, vf_sc):
    @pl.when((pl.program_id(0) == 0) & (pl.program_id(1) == 0))
    def _():
        for c0 in range(0, IN_WIDTH, WEIGHT_CAST_COLS):
            c1 = min(c0 + WEIGHT_CAST_COLS, IN_WIDTH)
            wb_sc[:, c0:c1] = w_ref[c0:c1, :].T.astype(BF16)

    nb = _rms(x_ref[...], ga_ref[...]).astype(BF16)

    def proj(c0, c1):
        return jnp.dot(nb, wb_sc[:, c0:c1], preferred_element_type=F32)

    def head_norm_t(x_t, gain_col):
        ms = jnp.mean(x_t * x_t, axis=0, keepdims=True)
        return x_t * lax.rsqrt(ms + EPS) * gain_col

    gla_cols = proj(1536, IN_WIDTH)
    gq_out[...] = (gla_cols[:, 0:256] * (GLA_K_DIM ** -0.5)).astype(BF16)
    gk_out[...] = gla_cols[:, 256:512].astype(BF16)
    gv_out[...] = gla_cols[:, 512:1024].astype(BF16)
    go_out[...] = gla_cols[:, 1024:1536].astype(BF16)
    glr = gla_cols[:, 1536:]
    z = jnp.dot(glr.astype(BF16), wup_ref[...], preferred_element_type=F32) + bg_ref[...]
    log_sig = jnp.minimum(z, 0.0) - jnp.log(1.0 + jnp.exp(-jnp.abs(z)))
    la_out[...] = log_sig * (1.0 / GLA_GATE_TEMP)

    qkv = []

    def project_qkv():
        full = proj(0, 1536)
        qkv.extend([full[:, 0:512], full[:, 512:1024], full[:, 1024:1536]])

    redo, overflow = _gla_tile(tril_ref, ind_ref, ggain_ref, gq_out, gk_out, gv_out, go_out, la_out,
                               og_ref, st_ref, b_sc, oi_sc, acc_sc, kf_sc, vf_sc,
                               after_cumsum=project_qkv)
    dq, dk, dv = qkv

    d = DIFF_HEAD_DIM
    q_scale = d ** -0.5 * LOG2E
    ones_rows = (lax.broadcasted_iota(jnp.int32, (VT_ROWS - DIFF_V_DIM, ROW_TILE), 0) == 0)
    for h in range(DIFF_HEADS):
        q_t = dq[:, h * LANES:(h + 1) * LANES].T
        k_t = dk[:, h * LANES:(h + 1) * LANES].T
        for j in range(2):
            yq = head_norm_t(q_t[j * d:(j + 1) * d], gq_ref[...]) * q_scale
            yk = head_norm_t(k_t[j * d:(j + 1) * d], gk_ref[...])
            qt_ref[0, h, j, 0, :d, :] = yq.astype(BF16)
            qt_ref[0, h, j, 0, d:, :] = cq_ref[h]
            ka_ref[0, h, j] = jnp.concatenate([yk, ck_ref[h]], axis=0).T.astype(BF16)
        vt_ref[0, h, 0, :DIFF_V_DIM, :] = dv[:, h * LANES:(h + 1) * LANES].T.astype(BF16)
        vt_ref[0, h, 0, DIFF_V_DIM:, :] = ones_rows.astype(BF16)

    for fn, flag in zip(redo, overflow):
        pl.when(flag)(fn)


def _const_spec(shape):
    nd = len(shape)
    return pl.BlockSpec(shape, lambda *_: (0,) * nd, pipeline_mode=pl.Buffered(1))


def _proj_call(xf, ga, w_main, gq, gk, cq, ck, wup, bg, tril, ind, ggain, batch, seq):
    assert ROW_TILE == GLA_TILE
    n_t = seq // ROW_TILE
    n = batch * seq
    row = lambda b, t: (b * n_t + t, 0)
    out_shape = (
        jax.ShapeDtypeStruct((batch, DIFF_HEADS, 2, seq // Q_TILE, LANES, Q_TILE), BF16),
        jax.ShapeDtypeStruct((batch, DIFF_HEADS, 2, seq, LANES), BF16),
        jax.ShapeDtypeStruct((batch, DIFF_HEADS, seq // K_TILE, VT_ROWS, K_TILE), BF16),
        jax.ShapeDtypeStruct((n, GLA_WIDTH), BF16),
    )
    out_specs = (
        pl.BlockSpec((1, DIFF_HEADS, 2, 1, LANES, Q_TILE), lambda b, t: (b, 0, 0, t, 0, 0)),
        pl.BlockSpec((1, DIFF_HEADS, 2, ROW_TILE, LANES), lambda b, t: (b, 0, 0, t, 0)),
        pl.BlockSpec((1, DIFF_HEADS, 1, VT_ROWS, K_TILE), lambda b, t: (b, 0, t, 0, 0)),
        pl.BlockSpec((ROW_TILE, GLA_WIDTH), row),
    )
    consts = (ga, w_main, gq, gk, cq, ck, wup, bg, tril, ind, ggain)
    in_specs = [pl.BlockSpec((ROW_TILE, D_MODEL), row)] + [_const_spec(a.shape) for a in consts]
    scratch_shapes = [
        pltpu.VMEM((D_MODEL, IN_WIDTH), BF16),
        pltpu.VMEM((ROW_TILE, GLA_KW), BF16),
        pltpu.VMEM((ROW_TILE, GLA_KW), BF16),
        pltpu.VMEM((ROW_TILE, GLA_WIDTH), BF16),
        pltpu.VMEM((ROW_TILE, GLA_WIDTH), BF16),
        pltpu.VMEM((ROW_TILE, GLA_KW), F32),
        pltpu.VMEM((GLA_HEADS, GLA_V_DIM, GLA_K_DIM), F32),
        pltpu.VMEM((GLA_TILE, GLA_KW), F32),
        pltpu.VMEM((GLA_TILE, GLA_WIDTH), F32),
        pltpu.VMEM((GLA_CHUNK, GLA_WIDTH), F32),
        pltpu.VMEM((GLA_CHUNK, GLA_KW), F32),
        pltpu.VMEM((GLA_CHUNK, GLA_WIDTH), F32),
    ]
    return pl.pallas_call(
        _proj_kernel, out_shape=out_shape, grid=(batch, n_t),
        in_specs=in_specs, out_specs=out_specs, name="proj_gla",
        scratch_shapes=scratch_shapes,
        compiler_params=pltpu.CompilerParams(
            dimension_semantics=("arbitrary", "arbitrary"), vmem_limit_bytes=VMEM_LIMIT),
    )(xf, *consts)


def _attn_kernel(slope_ref, lq1_ref, lk1_ref, lq2_ref, lk2_ref, gain_ref, qt_ref, qn_ref, k_ref,
                 vt_ref, o_ref, s_sc, cm_sc, m_sc, acc_sc, *, lambda_init):
    hg = pl.program_id(1)
    qi = pl.program_id(2)
    m_sc[...] = jnp.full(m_sc.shape, NEG_BIG, F32)
    acc_sc[...] = jnp.zeros(acc_sc.shape, F32)
    group = range(HEADS_PER_STEP)

    def score1(g, j, kb, masked, q_src=qt_ref):
        rows = pl.ds(pl.multiple_of(kb * K_TILE, K_TILE), K_TILE)
        s = jnp.dot(k_ref[0, g, j, rows, :], q_src[0, g, j, 0], preferred_element_type=F32)
        if masked:
            key = lax.broadcasted_iota(jnp.int32, s.shape, 0)
            qry = lax.broadcasted_iota(jnp.int32, s.shape, 1)
            s = jnp.where(key <= qry, s, NEG_BIG)
        s_sc[g, j] = s
        cm_sc[g, j] = jnp.max(s, axis=0, keepdims=True)

    def absorb1(g, j, kb):
        off = slope_ref[hg * HEADS_PER_STEP + g] * (kb - qi).astype(F32)
        m_prev = m_sc[g, j]
        m_new = jnp.maximum(m_prev, cm_sc[g, j] + off)
        p = jnp.exp2(s_sc[g, j] - (m_new - off))
        alpha = jnp.exp2(m_prev - m_new)
        acc_sc[g, j] = alpha * acc_sc[g, j] + jnp.dot(vt_ref[0, g, kb], p.astype(BF16),
                                                      preferred_element_type=F32)
        m_sc[g, j] = m_new

    def score(j, kb, masked):
        for g in group:
            score1(g, j, kb, masked)

    def absorb(j, kb):
        for g in group:
            absorb1(g, j, kb)

    def unit(kb):
        score(1, kb, False)
        absorb(0, kb)
        score(0, kb + 1, False)
        absorb(1, kb)

    def body(kp, carry):
        for u in range(LOOP_UNROLL):
            unit(LOOP_UNROLL * kp + u)
        return carry

    n_full = jnp.maximum(qi - 1, 0)
    lax.fori_loop(0, n_full // LOOP_UNROLL, body, 0)

    done = (n_full // LOOP_UNROLL) * LOOP_UNROLL
    size = LOOP_UNROLL // 2
    while size >= 1:
        take = (n_full - done) >= size

        @pl.when(take)
        def _(done=done, size=size):
            for u in range(size):
                unit(done + u)

        done = done + jnp.where(take, size, 0)
        size //= 2

    def finish():
        score(1, qi, True)
        absorb(0, qi)
        absorb(1, qi)
        for g in group:
            score1(g, 0, 0, False, q_src=qn_ref)
        lam = (jnp.exp(jnp.sum(lq1_ref[...] * lk1_ref[...], axis=1, keepdims=True))
               - jnp.exp(jnp.sum(lq2_ref[...] * lk2_ref[...], axis=1, keepdims=True))
               + lambda_init)
        for g in group:
            a1 = acc_sc[g, 0]
            a2 = acc_sc[g, 1]
            o_t = (a1[:DIFF_V_DIM] / a1[DIFF_V_DIM:DIFF_V_DIM + 1]
                   - lam * (a2[:DIFF_V_DIM] / a2[DIFF_V_DIM:DIFF_V_DIM + 1]))
            o_ref[:, g * DIFF_V_DIM:(g + 1) * DIFF_V_DIM] = (
                _rms(o_t.T, gain_ref[...]) * (1.0 - lambda_init)).astype(BF16)

    @pl.when(qi > 0)
    def _():
        score(1, qi - 1, False)
        absorb(0, qi - 1)
        score(0, qi, True)
        absorb(1, qi - 1)
        finish()

    @pl.when(qi == 0)
    def _():
        score(0, 0, True)
        finish()


def _attn_call(slope_k, lq1, lk1, lq2, lk2, gain, qt, ka, vt, batch, seq, lambda_init):
    n_q = seq // Q_TILE
    n_k = seq // K_TILE
    hps = HEADS_PER_STEP
    smem = pl.BlockSpec(memory_space=pltpu.SMEM)
    in_specs = [
        smem,
        _const_spec(lq1.shape), _const_spec(lk1.shape), _const_spec(lq2.shape),
        _const_spec(lk2.shape), _const_spec(gain.shape),
        pl.BlockSpec((1, hps, 2, 1, LANES, Q_TILE), lambda b, h, q: (b, h, 0, q, 0, 0)),
        pl.BlockSpec((1, hps, 2, 1, LANES, Q_TILE),
                     lambda b, h, q: (b, h, 0, jnp.minimum(q + 1, n_q - 1), 0, 0)),
        pl.BlockSpec((1, hps, 2, seq, LANES), lambda b, h, q: (b, h, 0, 0, 0)),
        pl.BlockSpec((1, hps, n_k, VT_ROWS, K_TILE), lambda b, h, q: (b, h, 0, 0, 0)),
    ]
    return pl.pallas_call(
        functools.partial(_attn_kernel, lambda_init=lambda_init),
        out_shape=jax.ShapeDtypeStruct((batch * seq, DIFF_WIDTH), BF16),
        grid=(batch, DIFF_HEADS // hps, n_q),
        in_specs=in_specs,
        out_specs=pl.BlockSpec((Q_TILE, hps * DIFF_V_DIM), lambda b, h, q: (b * n_q + q, h)),
        scratch_shapes=[pltpu.VMEM((hps, 2, K_TILE, Q_TILE), F32),
                        pltpu.VMEM((hps, 2, 1, Q_TILE), F32),
                        pltpu.VMEM((hps, 2, 1, Q_TILE), F32),
                        pltpu.VMEM((hps, 2, VT_ROWS, Q_TILE), F32)],
        name="diff_attn",
        compiler_params=pltpu.CompilerParams(
            dimension_semantics=("arbitrary", "arbitrary", "arbitrary"),
            vmem_limit_bytes=VMEM_LIMIT),
    )(slope_k, lq1, lk1, lq2, lk2, gain, qt, qt, ka, vt)


def _gla_tile(tril_ref, ind_ref, gain_ref, q_ref, k_ref, v_ref, g_ref, la_ref, o_ref,
              st_ref, b_sc, oi_sc, acc_sc, kf_sc, vf_sc, after_cumsum=None):
    @pl.when(pl.program_id(1) == 0)
    def _():
        st_ref[...] = jnp.zeros(st_ref.shape, F32)

    c_len = GLA_CHUNK
    row = lax.broadcasted_iota(jnp.int32, (c_len, c_len), 0)
    col = lax.broadcasted_iota(jnp.int32, (c_len, c_len), 1)
    causal = col <= row
    nt = (((1,), (1,)), ((), ()))

    def finalize(o, rows, vs):
        g = g_ref[rows, vs].astype(F32)
        return (_rms(o, gain_ref[...]) * (g * jax.nn.sigmoid(g))).astype(BF16)

    n_chunks = GLA_TILE // c_len
    cums = []
    for c in range(n_chunks):
        la = la_ref[c * c_len:(c + 1) * c_len, :]
        hi = la.astype(BF16)
        r1 = la - hi.astype(F32)
        mid = r1.astype(BF16)
        lo = (r1 - mid.astype(F32)).astype(BF16)
        tril = tril_ref[...]
        cums.append(jnp.dot(tril, hi, preferred_element_type=F32)
                    + jnp.dot(tril, mid, preferred_element_type=F32)
                    + jnp.dot(tril, lo, preferred_element_type=F32))

    if after_cumsum is not None:
        after_cumsum()

    overflow = []
    for c in range(n_chunks):
        rows = slice(c * c_len, (c + 1) * c_len)
        b = cums[c]
        b_sc[rows, :] = b
        b_last = b[c_len - 1:c_len, :]
        overflow.append(jnp.max(-b_last) > EXP_CLAMP)
        q = q_ref[rows, :].astype(F32)
        k = k_ref[rows, :].astype(F32)
        q_in = (q * jnp.exp(b)).astype(BF16)
        k_out = (k * jnp.exp(jnp.minimum(-b, EXP_CLAMP))).astype(BF16)
        k_dec = (k * jnp.exp(b_last - b)).astype(BF16)
        d_last = jnp.exp(b_last)
        for h in range(GLA_HEADS):
            ks = slice(h * GLA_K_DIM, (h + 1) * GLA_K_DIM)
            vs = slice(h * GLA_V_DIM, (h + 1) * GLA_V_DIM)
            a = lax.dot_general(q_in[:, ks], k_out[:, ks], nt, preferred_element_type=F32)
            a = jnp.where(causal, a, 0.0).astype(BF16)
            vh = v_ref[rows, vs]
            s_t = st_ref[h]
            o_inter = lax.dot_general(q_in[:, ks], s_t.astype(BF16), nt,
                                      preferred_element_type=F32)
            oi_sc[rows, vs] = o_inter
            o = o_inter + jnp.dot(a, vh, preferred_element_type=F32)
            v_t = vh.astype(F32).T.astype(BF16)
            st_ref[h] = s_t * d_last[:, ks] + jnp.dot(v_t, k_dec[:, ks], preferred_element_type=F32)
            o_ref[rows, vs] = finalize(o, rows, vs)

    def exact_intra(c):
        rows = slice(c * c_len, (c + 1) * c_len)
        q = q_ref[rows, :].astype(F32)
        b = b_sc[rows, :]
        kf_sc[...] = k_ref[rows, :].astype(F32)
        vf_sc[...] = v_ref[rows, :].astype(F32)
        acc_sc[...] = jnp.zeros(acc_sc.shape, F32)
        t_idx = lax.broadcasted_iota(jnp.int32, (c_len, GLA_KW), 0)

        def body(s, carry):
            k_s = kf_sc[pl.ds(s, 1), :]
            b_s = b_sc[pl.ds(c * c_len + s, 1), :]
            v_s = vf_sc[pl.ds(s, 1), :]
            w = q * k_s * jnp.exp(jnp.minimum(b - b_s, 0.0))
            w = jnp.where(t_idx >= s, w, 0.0).astype(BF16)
            a_s = jnp.dot(w, ind_ref[...], preferred_element_type=F32)
            acc_sc[...] += a_s * v_s
            return carry

        lax.fori_loop(0, c_len, body, 0)
        for h in range(GLA_HEADS):
            vs = slice(h * GLA_V_DIM, (h + 1) * GLA_V_DIM)
            o_ref[rows, vs] = finalize(oi_sc[rows, vs] + acc_sc[:, vs], rows, vs)

    return [functools.partial(exact_intra, c) for c in range(n_chunks)], overflow


def _ffn_kernel(x_ref, od_ref, og_ref, gf_ref, wo_ref, wg_ref, wu_ref, wd_ref, o_ref,
                wo_sc, wg_sc, wu_sc, wd_sc):
    i = pl.program_id(0)

    @pl.when(i < FFN_WARM)
    def _():
        for src, dst in ((wo_ref, wo_sc), (wg_ref, wg_sc), (wu_ref, wu_sc), (wd_ref, wd_sc)):
            rows = src.shape[0]
            dst[pl.ds(pl.multiple_of(i * rows, rows), rows), :] = src[...].astype(BF16)

    @pl.when(i >= FFN_WARM)
    def _():
        h = (x_ref[...]
             + jnp.dot(od_ref[...], wo_sc[:DIFF_WIDTH, :], preferred_element_type=F32)
             + jnp.dot(og_ref[...], wo_sc[DIFF_WIDTH:, :], preferred_element_type=F32))
        m = _rms(h, gf_ref[...]).astype(BF16)
        o_ref[...] = h
        for c0, c1 in FF_CHUNKS:
            gate = jnp.dot(m, wg_sc[:, c0:c1], preferred_element_type=F32)
            up = jnp.dot(m, wu_sc[:, c0:c1], preferred_element_type=F32)
            act = (gate * jax.nn.sigmoid(gate) * up).astype(BF16)
            o_ref[...] += jnp.dot(act, wd_sc[c0:c1, :], preferred_element_type=F32)


def _ffn_call(xf, od, og, gf, wo, wg, wu, wd):
    n = xf.shape[0]
    row = lambda i: (jnp.maximum(i - FFN_WARM, 0), 0)
    chunk = lambda i: (jnp.minimum(i, FFN_WARM - 1), 0)

    def chunk_spec(w):
        assert w.shape[0] % (FFN_WARM * 16) == 0
        return pl.BlockSpec((w.shape[0] // FFN_WARM, w.shape[1]), chunk)

    in_specs = [
        pl.BlockSpec((ROW_TILE, D_MODEL), row),
        pl.BlockSpec((ROW_TILE, DIFF_WIDTH), row),
        pl.BlockSpec((ROW_TILE, GLA_WIDTH), row),
        _const_spec(gf.shape),
        chunk_spec(wo), chunk_spec(wg), chunk_spec(wu), chunk_spec(wd),
    ]
    return pl.pallas_call(
        _ffn_kernel,
        out_shape=jax.ShapeDtypeStruct((n, D_MODEL), F32),
        grid=(FFN_WARM + n // ROW_TILE,),
        in_specs=in_specs,
        out_specs=pl.BlockSpec((ROW_TILE, D_MODEL), row),
        scratch_shapes=[pltpu.VMEM(w.shape, BF16) for w in (wo, wg, wu, wd)],
        name="out_ffn",
        compiler_params=pltpu.CompilerParams(
            dimension_semantics=("arbitrary",), vmem_limit_bytes=VMEM_LIMIT),
    )(xf, od, og, gf, wo, wg, wu, wd)


def _bf16_pieces(value, n=3):
    pieces = []
    rest = np.float32(value)
    for _ in range(n):
        piece = np.float32(np.asarray(rest, dtype=BF16))
        pieces.append(float(piece))
        rest = np.float32(rest - piece)
    return pieces


def _alibi_tables():
    assert ROW_TILE == Q_TILE == K_TILE
    pos = np.arange(ROW_TILE)
    a = (pos // ALIBI_SPLIT) * ALIBI_SPLIT
    c = pos % ALIBI_SPLIT
    extra = LANES - DIFF_HEAD_DIM
    cq = np.zeros((DIFF_HEADS, extra, ROW_TILE), np.float32)
    ck = np.zeros((DIFF_HEADS, extra, ROW_TILE), np.float32)
    for h in range(DIFF_HEADS):
        r = 0
        for piece in _bf16_pieces(_slope(h) * LOG2E):
            for q_row, k_row in ((piece, a), (piece, c), (-a, piece), (-c, piece)):
                cq[h, r, :] = q_row
                ck[h, r, :] = k_row
                r += 1
        assert r <= extra
    return jnp.asarray(cq, dtype=BF16), jnp.asarray(ck)


def _slope(h):
    return 2.0 ** (-8.0 * (h + 1) / DIFF_HEADS)


def kernel(x, attn_norm_gain, w_in, q_norm_gain, k_norm_gain, lambda_q1, lambda_k1, lambda_q2,
           lambda_k2, diff_out_norm_gain, w_gla_gate_up, b_gla_gate, gla_out_norm_gain, w_out,
           ffn_norm_gain, w_ffn_gate, w_ffn_up, w_ffn_down):
    batch, seq, _ = x.shape
    depth = w_in.shape[0]
    assert seq % ROW_TILE == 0 and seq % GLA_TILE == 0

    cq, ck = _alibi_tables()
    tril = jnp.asarray(np.tril(np.ones((GLA_CHUNK, GLA_CHUNK), np.float32)), dtype=BF16)
    ind = jnp.asarray((np.arange(GLA_KW)[:, None] // GLA_K_DIM
                       == np.arange(GLA_WIDTH)[None, :] // GLA_V_DIM).astype(np.float32), dtype=BF16)
    slope_k = jnp.asarray(
        [sum(_bf16_pieces(_slope(h) * LOG2E)) * K_TILE for h in range(DIFF_HEADS)], F32)

    hcur = x.reshape(batch * seq, D_MODEL)
    for l in range(depth):
        lambda_init = 0.8 - 0.6 * math.exp(-0.3 * l)
        w_main = w_in[l].T
        wup = w_gla_gate_up[l].astype(BF16)
        row2 = lambda a: a.reshape(1, -1).astype(F32)
        col2 = lambda a: a.reshape(-1, 1).astype(F32)

        qt, ka, vt, o_gla = _proj_call(
            hcur, row2(attn_norm_gain[l]), w_main, col2(q_norm_gain[l]), col2(k_norm_gain[l]),
            cq, ck, wup, row2(b_gla_gate[l]), tril, ind, row2(gla_out_norm_gain[l]), batch, seq)

        o_diff = _attn_call(slope_k, row2(lambda_q1[l]), row2(lambda_k1[l]), row2(lambda_q2[l]),
                            row2(lambda_k2[l]), row2(diff_out_norm_gain[l]), qt, ka, vt,
                            batch, seq, lambda_init)

        hcur = _ffn_call(hcur, o_diff, o_gla, row2(ffn_norm_gain[l]), w_out[l], w_ffn_gate[l],
                         w_ffn_up[l], w_ffn_down[l])
    return hcur.reshape(batch, seq, D_MODEL)
```

```python
import functools
import math

import jax
import jax.numpy as jnp
import numpy as np
from jax import lax
from jax.experimental import pallas as pl
from jax.experimental.pallas import tpu as pltpu

F32 = jnp.float32
BF16 = jnp.bfloat16

D_MODEL = 1024
DIFF_HEADS = 4
DIFF_HEAD_DIM = 64
DIFF_V_DIM = 128
DIFF_WIDTH = DIFF_HEADS * DIFF_V_DIM
GLA_HEADS = 4
GLA_K_DIM = 64
GLA_V_DIM = 128
GLA_WIDTH = GLA_HEADS * GLA_V_DIM
GLA_KW = GLA_HEADS * GLA_K_DIM
GLA_GATE_RANK = 16
GLA_GATE_TEMP = 16.0
D_FF = 2816
EPS = 1e-6
MAIN_WIDTH = 3072
IN_WIDTH = MAIN_WIDTH + GLA_GATE_RANK

LANES = 128
ROW_TILE = 512
Q_TILE = 512
VT_ROWS = 144
K_TILE = 512
LOOP_UNROLL = 4
HEADS_PER_STEP = 2
GLA_TILE = 512
GLA_CHUNK = 256
MXU_DIM = 256
WEIGHT_CAST_COLS = 512
FFN_WARM = 8
FF_CHUNKS = ((0, 6 * MXU_DIM), (6 * MXU_DIM, D_FF))
ALIBI_SPLIT = 16
NEG_BIG = -1e30
LOG2E = math.log2(math.e)
EXP_CLAMP = 80.0
VMEM_LIMIT = 56 * 1024 * 1024


def _rms(x, gain):
    return x * lax.rsqrt(jnp.mean(x * x, axis=-1, keepdims=True) + EPS) * gain


def _proj_kernel(x_ref, ga_ref, w_ref, gq_ref, gk_ref, cq_ref, ck_ref,
                 wup_ref, bg_ref, tril_ref, ind_ref, ggain_ref,
                 qt_ref, ka_ref, vt_ref, og_ref,
                 wb_sc, gq_out, gk_out, gv_out, go_out, la_out,
                 st_ref, b_sc, oi_sc, acc_sc, kf_sc, vf_sc):
    @pl.when((pl.program_id(0) == 0) & (pl.program_id(1) == 0))
    def _():
        for c0 in range(0, IN_WIDTH, WEIGHT_CAST_COLS):
            c1 = min(c0 + WEIGHT_CAST_COLS, IN_WIDTH)
            wb_sc[:, c0:c1] = w_ref[c0:c1, :].T.astype(BF16)

    nb = _rms(x_ref[...], ga_ref[...]).astype(BF16)

    def proj(c0, c1):
        return jnp.dot(nb, wb_sc[:, c0:c1], preferred_element_type=F32)

    def head_norm_t(x_t, gain_col):
        ms = jnp.mean(x_t * x_t, axis=0, keepdims=True)
        return x_t * lax.rsqrt(ms + EPS) * gain_col

    gla_cols = proj(1536, IN_WIDTH)
    gq_out[...] = (gla_cols[:, 0:256] * (GLA_K_DIM ** -0.5)).astype(BF16)
    gk_out[...] = gla_cols[:, 256:512].astype(BF16)
    gv_out[...] = gla_cols[:, 512:1024].astype(BF16)
    go_out[...] = gla_cols[:, 1024:1536].astype(BF16)
    glr = gla_cols[:, 1536:]
    dq = proj(0, 512)
    z = jnp.dot(glr.astype(BF16), wup_ref[...], preferred_element_type=F32) + bg_ref[...]
    log_sig = jnp.minimum(z, 0.0) - jnp.log(1.0 + jnp.exp(-jnp.abs(z)))
    la_out[...] = log_sig * (1.0 / GLA_GATE_TEMP)

    kv = []

    def project_kv():
        kv.extend([proj(512, 1024), proj(1024, 1536)])

    redo, overflow = _gla_tile(tril_ref, ind_ref, ggain_ref, gq_out, gk_out, gv_out, go_out, la_out,
                               og_ref, st_ref, b_sc, oi_sc, acc_sc, kf_sc, vf_sc,
                               after_cumsum=project_kv)
    dk, dv = kv

    d = DIFF_HEAD_DIM
    q_scale = d ** -0.5 * LOG2E
    ones_rows = (lax.broadcasted_iota(jnp.int32, (VT_ROWS - DIFF_V_DIM, ROW_TILE), 0) == 0)
    for h in range(DIFF_HEADS):
        q_t = dq[:, h * LANES:(h + 1) * LANES].T
        k_t = dk[:, h * LANES:(h + 1) * LANES].T
        for j in range(2):
            yq = head_norm_t(q_t[j * d:(j + 1) * d], gq_ref[...]) * q_scale
            yk = head_norm_t(k_t[j * d:(j + 1) * d], gk_ref[...])
            qt_ref[0, h, j, 0, :d, :] = yq.astype(BF16)
            qt_ref[0, h, j, 0, d:, :] = cq_ref[h]
            ka_ref[0, h, j] = jnp.concatenate([yk, ck_ref[h]], axis=0).T.astype(BF16)
        vt_ref[0, h, 0, :DIFF_V_DIM, :] = dv[:, h * LANES:(h + 1) * LANES].T.astype(BF16)
        vt_ref[0, h, 0, DIFF_V_DIM:, :] = ones_rows.astype(BF16)

    for fn, flag in zip(redo, overflow):
        pl.when(flag)(fn)


def _const_spec(shape):
    nd = len(shape)
    return pl.BlockSpec(shape, lambda *_: (0,) * nd, pipeline_mode=pl.Buffered(1))


def _proj_call(xf, ga, w_main, gq, gk, cq, ck, wup, bg, tril, ind, ggain, batch, seq):
    assert ROW_TILE == GLA_TILE
    n_t = seq // ROW_TILE
    n = batch * seq
    row = lambda b, t: (b * n_t + t, 0)
    out_shape = (
        jax.ShapeDtypeStruct((batch, DIFF_HEADS, 2, seq // Q_TILE, LANES, Q_TILE), BF16),
        jax.ShapeDtypeStruct((batch, DIFF_HEADS, 2, seq, LANES), BF16),
        jax.ShapeDtypeStruct((batch, DIFF_HEADS, seq // K_TILE, VT_ROWS, K_TILE), BF16),
        jax.ShapeDtypeStruct((n, GLA_WIDTH), BF16),
    )
    out_specs = (
        pl.BlockSpec((1, DIFF_HEADS, 2, 1, LANES, Q_TILE), lambda b, t: (b, 0, 0, t, 0, 0)),
        pl.BlockSpec((1, DIFF_HEADS, 2, ROW_TILE, LANES), lambda b, t: (b, 0, 0, t, 0)),
        pl.BlockSpec((1, DIFF_HEADS, 1, VT_ROWS, K_TILE), lambda b, t: (b, 0, t, 0, 0)),
        pl.BlockSpec((ROW_TILE, GLA_WIDTH), row),
    )
    consts = (ga, w_main, gq, gk, cq, ck, wup, bg, tril, ind, ggain)
    in_specs = [pl.BlockSpec((ROW_TILE, D_MODEL), row)] + [_const_spec(a.shape) for a in consts]
    scratch_shapes = [
        pltpu.VMEM((D_MODEL, IN_WIDTH), BF16),
        pltpu.VMEM((ROW_TILE, GLA_KW), BF16),
        pltpu.VMEM((ROW_TILE, GLA_KW), BF16),
        pltpu.VMEM((ROW_TILE, GLA_WIDTH), BF16),
        pltpu.VMEM((ROW_TILE, GLA_WIDTH), BF16),
        pltpu.VMEM((ROW_TILE, GLA_KW), F32),
        pltpu.VMEM((GLA_HEADS, GLA_V_DIM, GLA_K_DIM), F32),
        pltpu.VMEM((GLA_TILE, GLA_KW), F32),
        pltpu.VMEM((GLA_TILE, GLA_WIDTH), F32),
        pltpu.VMEM((GLA_CHUNK, GLA_WIDTH), F32),
        pltpu.VMEM((GLA_CHUNK, GLA_KW), F32),
        pltpu.VMEM((GLA_CHUNK, GLA_WIDTH), F32),
    ]
    return pl.pallas_call(
        _proj_kernel, out_shape=out_shape, grid=(batch, n_t),
        in_specs=in_specs, out_specs=out_specs, name="proj_gla",
        scratch_shapes=scratch_shapes,
        compiler_params=pltpu.CompilerParams(
            dimension_semantics=("arbitrary", "arbitrary"), vmem_limit_bytes=VMEM_LIMIT),
    )(xf, *consts)


def _attn_kernel(slope_ref, lq1_ref, lk1_ref, lq2_ref, lk2_ref, gain_ref, qt_ref, qn_ref, k_ref,
                 vt_ref, o_ref, s_sc, cm_sc, m_sc, acc_sc, *, lambda_init):
    hg = pl.program_id(1)
    qi = pl.program_id(2)
    m_sc[...] = jnp.full(m_sc.shape, NEG_BIG, F32)
    acc_sc[...] = jnp.zeros(acc_sc.shape, F32)
    group = range(HEADS_PER_STEP)

    half = K_TILE // 2

    def score1(g, j, kb, q_src=qt_ref):
        rows = pl.ds(pl.multiple_of(kb * K_TILE, K_TILE), K_TILE)
        s = jnp.dot(k_ref[0, g, j, rows, :], q_src[0, g, j, 0], preferred_element_type=F32)
        s_sc[g, j] = s
        cm_sc[g, j] = jnp.max(s, axis=0, keepdims=True)

    def absorb1(g, j, kb):
        off = slope_ref[hg * HEADS_PER_STEP + g] * (kb - qi).astype(F32)
        m_prev = m_sc[g, j]
        m_new = jnp.maximum(m_prev, cm_sc[g, j] + off)
        p = jnp.exp2(s_sc[g, j] - (m_new - off))
        alpha = jnp.exp2(m_prev - m_new)
        acc_sc[g, j] = alpha * acc_sc[g, j] + jnp.dot(vt_ref[0, g, kb], p.astype(BF16),
                                                      preferred_element_type=F32)
        m_sc[g, j] = m_new

    def score_diag1(g, j):
        r0 = pl.multiple_of(qi * K_TILE, K_TILE)
        q_t = qt_ref[0, g, j, 0]
        s_a = jnp.dot(k_ref[0, g, j, pl.ds(r0, half), :], q_t, preferred_element_type=F32)
        s_b = jnp.dot(k_ref[0, g, j, pl.ds(r0 + half, half), :], q_t[:, half:],
                      preferred_element_type=F32)
        s_a = jnp.where(lax.broadcasted_iota(jnp.int32, s_a.shape, 0)
                        <= lax.broadcasted_iota(jnp.int32, s_a.shape, 1), s_a, NEG_BIG)
        s_b = jnp.where(lax.broadcasted_iota(jnp.int32, s_b.shape, 0)
                        <= lax.broadcasted_iota(jnp.int32, s_b.shape, 1), s_b, NEG_BIG)
        s_sc[g, j, :half, :] = s_a
        s_sc[g, j, half:, half:] = s_b
        cm_sc[g, j] = jnp.max(s_a, axis=0, keepdims=True)
        cm_sc[g, j, :, half:] = jnp.maximum(cm_sc[g, j, :, half:],
                                            jnp.max(s_b, axis=0, keepdims=True))

    def absorb_diag1(g, j):
        m_prev = m_sc[g, j]
        m_new = jnp.maximum(m_prev, cm_sc[g, j])
        m_sc[g, j] = m_new
        p_a = jnp.exp2(s_sc[g, j, :half, :] - m_new)
        p_b = jnp.exp2(s_sc[g, j, half:, half:] - m_sc[g, j, :, half:])
        alpha = jnp.exp2(m_prev - m_new)
        pv_a = jnp.dot(vt_ref[0, g, qi, :, :half], p_a.astype(BF16), preferred_element_type=F32)
        pv_b = jnp.dot(vt_ref[0, g, qi, :, half:], p_b.astype(BF16), preferred_element_type=F32)
        acc_sc[g, j] = alpha * acc_sc[g, j] + pv_a
        acc_sc[g, j, :, half:] += pv_b

    def score(j, kb):
        for g in group:
            score1(g, j, kb)

    def absorb(j, kb):
        for g in group:
            absorb1(g, j, kb)

    def score_diag(j):
        for g in group:
            score_diag1(g, j)

    def absorb_diag(j):
        for g in group:
            absorb_diag1(g, j)

    def unit(kb):
        score(1, kb)
        absorb(0, kb)
        score(0, kb + 1)
        absorb(1, kb)

    def body(kp, carry):
        for u in range(LOOP_UNROLL):
            unit(LOOP_UNROLL * kp + u)
        return carry

    n_full = jnp.maximum(qi - 1, 0)
    lax.fori_loop(0, n_full // LOOP_UNROLL, body, 0)

    done = (n_full // LOOP_UNROLL) * LOOP_UNROLL
    size = LOOP_UNROLL // 2
    while size >= 1:
        take = (n_full - done) >= size

        @pl.when(take)
        def _(done=done, size=size):
            for u in range(size):
                unit(done + u)

        done = done + jnp.where(take, size, 0)
        size //= 2

    def finish():
        score_diag(1)
        absorb_diag(0)
        absorb_diag(1)
        for g in group:
            score1(g, 0, 0, q_src=qn_ref)
        lam = (jnp.exp(jnp.sum(lq1_ref[...] * lk1_ref[...], axis=1, keepdims=True))
               - jnp.exp(jnp.sum(lq2_ref[...] * lk2_ref[...], axis=1, keepdims=True))
               + lambda_init)
        for g in group:
            a1 = acc_sc[g, 0]
            a2 = acc_sc[g, 1]
            o_t = (a1[:DIFF_V_DIM] / a1[DIFF_V_DIM:DIFF_V_DIM + 1]
                   - lam * (a2[:DIFF_V_DIM] / a2[DIFF_V_DIM:DIFF_V_DIM + 1]))
            o_ref[:, g * DIFF_V_DIM:(g + 1) * DIFF_V_DIM] = (
                _rms(o_t.T, gain_ref[...]) * (1.0 - lambda_init)).astype(BF16)

    @pl.when(qi > 0)
    def _():
        score(1, qi - 1)
        absorb(0, qi - 1)
        score_diag(0)
        absorb(1, qi - 1)
        finish()

    @pl.when(qi == 0)
    def _():
        score_diag(0)
        finish()


def _attn_call(slope_k, lq1, lk1, lq2, lk2, gain, qt, ka, vt, batch, seq, lambda_init):
    n_q = seq // Q_TILE
    n_k = seq // K_TILE
    hps = HEADS_PER_STEP
    smem = pl.BlockSpec(memory_space=pltpu.SMEM)
    in_specs = [
        smem,
        _const_spec(lq1.shape), _const_spec(lk1.shape), _const_spec(lq2.shape),
        _const_spec(lk2.shape), _const_spec(gain.shape),
        pl.BlockSpec((1, hps, 2, 1, LANES, Q_TILE), lambda b, h, q: (b, h, 0, q, 0, 0)),
        pl.BlockSpec((1, hps, 2, 1, LANES, Q_TILE),
                     lambda b, h, q: (b, h, 0, jnp.minimum(q + 1, n_q - 1), 0, 0)),
        pl.BlockSpec((1, hps, 2, seq, LANES), lambda b, h, q: (b, h, 0, 0, 0)),
        pl.BlockSpec((1, hps, n_k, VT_ROWS, K_TILE), lambda b, h, q: (b, h, 0, 0, 0)),
    ]
    return pl.pallas_call(
        functools.partial(_attn_kernel, lambda_init=lambda_init),
        out_shape=jax.ShapeDtypeStruct((batch * seq, DIFF_WIDTH), BF16),
        grid=(batch, DIFF_HEADS // hps, n_q),
        in_specs=in_specs,
        out_specs=pl.BlockSpec((Q_TILE, hps * DIFF_V_DIM), lambda b, h, q: (b * n_q + q, h)),
        scratch_shapes=[pltpu.VMEM((hps, 2, K_TILE, Q_TILE), F32),
                        pltpu.VMEM((hps, 2, 1, Q_TILE), F32),
                        pltpu.VMEM((hps, 2, 1, Q_TILE), F32),
                        pltpu.VMEM((hps, 2, VT_ROWS, Q_TILE), F32)],
        name="diff_attn",
        compiler_params=pltpu.CompilerParams(
            dimension_semantics=("arbitrary", "arbitrary", "arbitrary"),
            vmem_limit_bytes=VMEM_LIMIT),
    )(slope_k, lq1, lk1, lq2, lk2, gain, qt, qt, ka, vt)


def _gla_tile(tril_ref, ind_ref, gain_ref, q_ref, k_ref, v_ref, g_ref, la_ref, o_ref,
              st_ref, b_sc, oi_sc, acc_sc, kf_sc, vf_sc, after_cumsum=None):
    @pl.when(pl.program_id(1) == 0)
    def _():
        st_ref[...] = jnp.zeros(st_ref.shape, F32)

    c_len = GLA_CHUNK
    row = lax.broadcasted_iota(jnp.int32, (c_len, c_len), 0)
    col = lax.broadcasted_iota(jnp.int32, (c_len, c_len), 1)
    causal = col <= row
    nt = (((1,), (1,)), ((), ()))

    def finalize(o, rows, vs):
        g = g_ref[rows, vs].astype(F32)
        return (_rms(o, gain_ref[...]) * (g * jax.nn.sigmoid(g))).astype(BF16)

    n_chunks = GLA_TILE // c_len
    cums = []
    for c in range(n_chunks):
        la = la_ref[c * c_len:(c + 1) * c_len, :]
        hi = la.astype(BF16)
        r1 = la - hi.astype(F32)
        mid = r1.astype(BF16)
        lo = (r1 - mid.astype(F32)).astype(BF16)
        tril = tril_ref[...]
        cums.append(jnp.dot(tril, hi, preferred_element_type=F32)
                    + jnp.dot(tril, mid, preferred_element_type=F32)
                    + jnp.dot(tril, lo, preferred_element_type=F32))

    if after_cumsum is not None:
        after_cumsum()

    overflow = []
    for c in range(n_chunks):
        rows = slice(c * c_len, (c + 1) * c_len)
        b = cums[c]
        b_sc[rows, :] = b
        b_last = b[c_len - 1:c_len, :]
        overflow.append(jnp.max(-b_last) > EXP_CLAMP)
        q = q_ref[rows, :].astype(F32)
        k = k_ref[rows, :].astype(F32)
        q_in = (q * jnp.exp(b)).astype(BF16)
        k_out = (k * jnp.exp(jnp.minimum(-b, EXP_CLAMP))).astype(BF16)
        k_dec = (k * jnp.exp(b_last - b)).astype(BF16)
        d_last = jnp.exp(b_last)
        for h in range(GLA_HEADS):
            ks = slice(h * GLA_K_DIM, (h + 1) * GLA_K_DIM)
            vs = slice(h * GLA_V_DIM, (h + 1) * GLA_V_DIM)
            a = lax.dot_general(q_in[:, ks], k_out[:, ks], nt, preferred_element_type=F32)
            a = jnp.where(causal, a, 0.0).astype(BF16)
            vh = v_ref[rows, vs]
            s_t = st_ref[h]
            o_inter = lax.dot_general(q_in[:, ks], s_t.astype(BF16), nt,
                                      preferred_element_type=F32)
            oi_sc[rows, vs] = o_inter
            o = o_inter + jnp.dot(a, vh, preferred_element_type=F32)
            v_t = vh.astype(F32).T.astype(BF16)
            st_ref[h] = s_t * d_last[:, ks] + jnp.dot(v_t, k_dec[:, ks], preferred_element_type=F32)
            o_ref[rows, vs] = finalize(o, rows, vs)

    def exact_intra(c):
        rows = slice(c * c_len, (c + 1) * c_len)
        q = q_ref[rows, :].astype(F32)
        b = b_sc[rows, :]
        kf_sc[...] = k_ref[rows, :].astype(F32)
        vf_sc[...] = v_ref[rows, :].astype(F32)
        acc_sc[...] = jnp.zeros(acc_sc.shape, F32)
        t_idx = lax.broadcasted_iota(jnp.int32, (c_len, GLA_KW), 0)

        def body(s, carry):
            k_s = kf_sc[pl.ds(s, 1), :]
            b_s = b_sc[pl.ds(c * c_len + s, 1), :]
            v_s = vf_sc[pl.ds(s, 1), :]
            w = q * k_s * jnp.exp(jnp.minimum(b - b_s, 0.0))
            w = jnp.where(t_idx >= s, w, 0.0).astype(BF16)
            a_s = jnp.dot(w, ind_ref[...], preferred_element_type=F32)
            acc_sc[...] += a_s * v_s
            return carry

        lax.fori_loop(0, c_len, body, 0)
        for h in range(GLA_HEADS):
            vs = slice(h * GLA_V_DIM, (h + 1) * GLA_V_DIM)
            o_ref[rows, vs] = finalize(oi_sc[rows, vs] + acc_sc[:, vs], rows, vs)

    return [functools.partial(exact_intra, c) for c in range(n_chunks)], overflow


def _ffn_kernel(x_ref, od_ref, og_ref, gf_ref, wo_ref, wg_ref, wu_ref, wd_ref, o_ref,
                wo_sc, wg_sc, wu_sc, wd_sc):
    i = pl.program_id(0)

    @pl.when(i < FFN_WARM)
    def _():
        for src, dst in ((wo_ref, wo_sc), (wg_ref, wg_sc), (wu_ref, wu_sc), (wd_ref, wd_sc)):
            rows = src.shape[0]
            dst[pl.ds(pl.multiple_of(i * rows, rows), rows), :] = src[...].astype(BF16)

    @pl.when(i >= FFN_WARM)
    def _():
        h = (x_ref[...]
             + jnp.dot(od_ref[...], wo_sc[:DIFF_WIDTH, :], preferred_element_type=F32)
             + jnp.dot(og_ref[...], wo_sc[DIFF_WIDTH:, :], preferred_element_type=F32))
        m = _rms(h, gf_ref[...]).astype(BF16)
        o_ref[...] = h
        for c0, c1 in FF_CHUNKS:
            gate = jnp.dot(m, wg_sc[:, c0:c1], preferred_element_type=F32)
            up = jnp.dot(m, wu_sc[:, c0:c1], preferred_element_type=F32)
            act = (gate * jax.nn.sigmoid(gate) * up).astype(BF16)
            o_ref[...] += jnp.dot(act, wd_sc[c0:c1, :], preferred_element_type=F32)


def _ffn_call(xf, od, og, gf, wo, wg, wu, wd):
    n = xf.shape[0]
    row = lambda i: (jnp.maximum(i - FFN_WARM, 0), 0)
    chunk = lambda i: (jnp.minimum(i, FFN_WARM - 1), 0)

    def chunk_spec(w):
        assert w.shape[0] % (FFN_WARM * 16) == 0
        return pl.BlockSpec((w.shape[0] // FFN_WARM, w.shape[1]), chunk)

    in_specs = [
        pl.BlockSpec((ROW_TILE, D_MODEL), row),
        pl.BlockSpec((ROW_TILE, DIFF_WIDTH), row),
        pl.BlockSpec((ROW_TILE, GLA_WIDTH), row),
        _const_spec(gf.shape),
        chunk_spec(wo), chunk_spec(wg), chunk_spec(wu), chunk_spec(wd),
    ]
    return pl.pallas_call(
        _ffn_kernel,
        out_shape=jax.ShapeDtypeStruct((n, D_MODEL), F32),
        grid=(FFN_WARM + n // ROW_TILE,),
        in_specs=in_specs,
        out_specs=pl.BlockSpec((ROW_TILE, D_MODEL), row),
        scratch_shapes=[pltpu.VMEM(w.shape, BF16) for w in (wo, wg, wu, wd)],
        name="out_ffn",
        compiler_params=pltpu.CompilerParams(
            dimension_semantics=("arbitrary",), vmem_limit_bytes=VMEM_LIMIT),
    )(xf, od, og, gf, wo, wg, wu, wd)


def _bf16_pieces(value, n=3):
    pieces = []
    rest = np.float32(value)
    for _ in range(n):
        piece = np.float32(np.asarray(rest, dtype=BF16))
        pieces.append(float(piece))
        rest = np.float32(rest - piece)
    return pieces


def _alibi_tables():
    assert ROW_TILE == Q_TILE == K_TILE
    pos = np.arange(ROW_TILE)
    a = (pos // ALIBI_SPLIT) * ALIBI_SPLIT
    c = pos % ALIBI_SPLIT
    extra = LANES - DIFF_HEAD_DIM
    cq = np.zeros((DIFF_HEADS, extra, ROW_TILE), np.float32)
    ck = np.zeros((DIFF_HEADS, extra, ROW_TILE), np.float32)
    for h in range(DIFF_HEADS):
        r = 0
        for piece in _bf16_pieces(_slope(h) * LOG2E):
            for q_row, k_row in ((piece, a), (piece, c), (-a, piece), (-c, piece)):
                cq[h, r, :] = q_row
                ck[h, r, :] = k_row
                r += 1
        assert r <= extra
    return jnp.asarray(cq, dtype=BF16), jnp.asarray(ck)


def _slope(h):
    return 2.0 ** (-8.0 * (h + 1) / DIFF_HEADS)


def kernel(x, attn_norm_gain, w_in, q_norm_gain, k_norm_gain, lambda_q1, lambda_k1, lambda_q2,
           lambda_k2, diff_out_norm_gain, w_gla_gate_up, b_gla_gate, gla_out_norm_gain, w_out,
           ffn_norm_gain, w_ffn_gate, w_ffn_up, w_ffn_down):
    batch, seq, _ = x.shape
    depth = w_in.shape[0]
    assert seq % ROW_TILE == 0 and seq % GLA_TILE == 0

    cq, ck = _alibi_tables()
    tril = jnp.asarray(np.tril(np.ones((GLA_CHUNK, GLA_CHUNK), np.float32)), dtype=BF16)
    ind = jnp.asarray((np.arange(GLA_KW)[:, None] // GLA_K_DIM
                       == np.arange(GLA_WIDTH)[None, :] // GLA_V_DIM).astype(np.float32), dtype=BF16)
    slope_k = jnp.asarray(
        [sum(_bf16_pieces(_slope(h) * LOG2E)) * K_TILE for h in range(DIFF_HEADS)], F32)

    hcur = x.reshape(batch * seq, D_MODEL)
    for l in range(depth):
        lambda_init = 0.8 - 0.6 * math.exp(-0.3 * l)
        w_main = w_in[l].T
        wup = w_gla_gate_up[l].astype(BF16)
        row2 = lambda a: a.reshape(1, -1).astype(F32)
        col2 = lambda a: a.reshape(-1, 1).astype(F32)

        qt, ka, vt, o_gla = _proj_call(
            hcur, row2(attn_norm_gain[l]), w_main, col2(q_norm_gain[l]), col2(k_norm_gain[l]),
            cq, ck, wup, row2(b_gla_gate[l]), tril, ind, row2(gla_out_norm_gain[l]), batch, seq)

        o_diff = _attn_call(slope_k, row2(lambda_q1[l]), row2(lambda_k1[l]), row2(lambda_q2[l]),
                            row2(lambda_k2[l]), row2(diff_out_norm_gain[l]), qt, ka, vt,
                            batch, seq, lambda_init)

        hcur = _ffn_call(hcur, o_diff, o_gla, row2(ffn_norm_gain[l]), w_out[l], w_ffn_gate[l],
                         w_ffn_up[l], w_ffn_down[l])
    return hcur.reshape(batch, seq, D_MODEL)
```

```python
import functools
import math

import jax
import jax.numpy as jnp
import numpy as np
from jax import lax
from jax.experimental import pallas as pl
from jax.experimental.pallas import tpu as pltpu

F32 = jnp.float32
BF16 = jnp.bfloat16

D_MODEL = 1024
DIFF_HEADS = 4
DIFF_HEAD_DIM = 64
DIFF_V_DIM = 128
DIFF_WIDTH = DIFF_HEADS * DIFF_V_DIM
GLA_HEADS = 4
GLA_K_DIM = 64
GLA_V_DIM = 128
GLA_WIDTH = GLA_HEADS * GLA_V_DIM
GLA_KW = GLA_HEADS * GLA_K_DIM
GLA_GATE_RANK = 16
GLA_GATE_TEMP = 16.0
D_FF = 2816
EPS = 1e-6
COL_DQ = 0
COL_DK = COL_DQ + DIFF_HEADS * 2 * DIFF_HEAD_DIM
COL_DV = COL_DK + DIFF_HEADS * 2 * DIFF_HEAD_DIM
COL_GQ = COL_DV + DIFF_WIDTH
COL_GK = COL_GQ + GLA_KW
COL_GV = COL_GK + GLA_KW
COL_GO = COL_GV + GLA_WIDTH
MAIN_WIDTH = COL_GO + GLA_WIDTH
IN_WIDTH = MAIN_WIDTH + GLA_GATE_RANK

LANES = 128
ROW_TILE = 512
Q_TILE = 512
VT_ROWS = 144
K_TILE = 512
LOOP_UNROLL = 4
HEADS_PER_STEP = 2
GLA_TILE = 512
GLA_CHUNK = 256
MXU_DIM = 256
WEIGHT_CAST_COLS = 512
FFN_WARM = 8
FF_CHUNKS = ((0, 6 * MXU_DIM), (6 * MXU_DIM, D_FF))
ALIBI_SPLIT = 16
NEG_BIG = -1e30
LOG2E = math.log2(math.e)
EXP_CLAMP = 80.0
VMEM_LIMIT = 56 * 1024 * 1024


def _rms(x, gain):
    return x * lax.rsqrt(jnp.mean(x * x, axis=-1, keepdims=True) + EPS) * gain


def _proj_kernel(x_ref, ga_ref, w_ref, gq_ref, gk_ref, cq_ref, ck_ref,
                 wup_ref, bg_ref, tril_ref, ind_ref, ggain_ref,
                 qt_ref, ka_ref, vt_ref, og_ref,
                 wb_sc, gq_out, gk_out, gv_out, go_out, la_out,
                 st_ref, b_sc, oi_sc, acc_sc, kf_sc, vf_sc):
    @pl.when((pl.program_id(0) == 0) & (pl.program_id(1) == 0))
    def _():
        for c0 in range(0, IN_WIDTH, WEIGHT_CAST_COLS):
            c1 = min(c0 + WEIGHT_CAST_COLS, IN_WIDTH)
            wb_sc[:, c0:c1] = w_ref[c0:c1, :].T.astype(BF16)

    nb = _rms(x_ref[...], ga_ref[...]).astype(BF16)

    def proj(c0, c1):
        return jnp.dot(nb, wb_sc[:, c0:c1], preferred_element_type=F32)

    def head_norm_t(x_t, gain_col):
        ms = jnp.mean(x_t * x_t, axis=0, keepdims=True)
        return x_t * lax.rsqrt(ms + EPS) * gain_col

    d = DIFF_HEAD_DIM
    q_scale = d ** -0.5 * LOG2E
    ones_rows = (lax.broadcasted_iota(jnp.int32, (VT_ROWS - DIFF_V_DIM, ROW_TILE), 0) == 0)

    def emit_qkv(dq, dk, dv):
        for h in range(DIFF_HEADS):
            q_t = dq[:, h * LANES:(h + 1) * LANES].T
            k_t = dk[:, h * LANES:(h + 1) * LANES].T
            for j in range(2):
                yq = head_norm_t(q_t[j * d:(j + 1) * d], gq_ref[...]) * q_scale
                yk = head_norm_t(k_t[j * d:(j + 1) * d], gk_ref[...])
                qt_ref[0, h, j, 0, :d, :] = yq.astype(BF16)
                qt_ref[0, h, j, 0, d:, :] = cq_ref[h]
                ka_ref[0, h, j] = jnp.concatenate([yk, ck_ref[h]], axis=0).T.astype(BF16)
            vt_ref[0, h, 0, :DIFF_V_DIM, :] = dv[:, h * LANES:(h + 1) * LANES].T.astype(BF16)
            vt_ref[0, h, 0, DIFF_V_DIM:, :] = ones_rows.astype(BF16)

    gla_cols = proj(COL_GQ, IN_WIDTH)
    gq_out[...] = (gla_cols[:, :GLA_KW] * (GLA_K_DIM ** -0.5)).astype(BF16)
    gk_out[...] = gla_cols[:, COL_GK - COL_GQ:COL_GV - COL_GQ].astype(BF16)
    gv_out[...] = gla_cols[:, COL_GV - COL_GQ:COL_GO - COL_GQ].astype(BF16)
    go_out[...] = gla_cols[:, COL_GO - COL_GQ:MAIN_WIDTH - COL_GQ].astype(BF16)
    glr = gla_cols[:, MAIN_WIDTH - COL_GQ:]
    dq = proj(COL_DQ, COL_DK)
    z = jnp.dot(glr.astype(BF16), wup_ref[...], preferred_element_type=F32) + bg_ref[...]
    log_sig = jnp.minimum(z, 0.0) - jnp.log(1.0 + jnp.exp(-jnp.abs(z)))
    la_out[...] = log_sig * (1.0 / GLA_GATE_TEMP)

    kv = []

    def project_kv():
        kv.extend([proj(COL_DK, COL_DV), proj(COL_DV, COL_GQ)])

    redo, overflow = _gla_tile(tril_ref, ind_ref, ggain_ref, gq_out, gk_out, gv_out, go_out, la_out,
                               og_ref, st_ref, b_sc, oi_sc, acc_sc, kf_sc, vf_sc,
                               after_cumsum=project_kv)
    emit_qkv(dq, *kv)
    for fn, flag in zip(redo, overflow):
        pl.when(flag)(fn)


def _const_spec(shape):
    nd = len(shape)
    return pl.BlockSpec(shape, lambda *_: (0,) * nd, pipeline_mode=pl.Buffered(1))


def _proj_call(xf, ga, w_main, gq, gk, cq, ck, wup, bg, tril, ind, ggain, batch, seq):
    assert ROW_TILE == GLA_TILE
    n_t = seq // ROW_TILE
    n = batch * seq
    row = lambda b, t: (b * n_t + t, 0)
    out_shape = (
        jax.ShapeDtypeStruct((batch, DIFF_HEADS, 2, seq // Q_TILE, LANES, Q_TILE), BF16),
        jax.ShapeDtypeStruct((batch, DIFF_HEADS, 2, seq, LANES), BF16),
        jax.ShapeDtypeStruct((batch, DIFF_HEADS, seq // K_TILE, VT_ROWS, K_TILE), BF16),
        jax.ShapeDtypeStruct((n, GLA_WIDTH), BF16),
    )
    out_specs = (
        pl.BlockSpec((1, DIFF_HEADS, 2, 1, LANES, Q_TILE), lambda b, t: (b, 0, 0, t, 0, 0)),
        pl.BlockSpec((1, DIFF_HEADS, 2, ROW_TILE, LANES), lambda b, t: (b, 0, 0, t, 0)),
        pl.BlockSpec((1, DIFF_HEADS, 1, VT_ROWS, K_TILE), lambda b, t: (b, 0, t, 0, 0)),
        pl.BlockSpec((ROW_TILE, GLA_WIDTH), row),
    )
    consts = (ga, w_main, gq, gk, cq, ck, wup, bg, tril, ind, ggain)
    in_specs = [pl.BlockSpec((ROW_TILE, D_MODEL), row)] + [_const_spec(a.shape) for a in consts]
    scratch_shapes = [
        pltpu.VMEM((D_MODEL, IN_WIDTH), BF16),
        pltpu.VMEM((ROW_TILE, GLA_KW), BF16),
        pltpu.VMEM((ROW_TILE, GLA_KW), BF16),
        pltpu.VMEM((ROW_TILE, GLA_WIDTH), BF16),
        pltpu.VMEM((ROW_TILE, GLA_WIDTH), BF16),
        pltpu.VMEM((ROW_TILE, GLA_KW), F32),
        pltpu.VMEM((GLA_HEADS, GLA_V_DIM, GLA_K_DIM), F32),
        pltpu.VMEM((GLA_TILE, GLA_KW), F32),
        pltpu.VMEM((GLA_TILE, GLA_WIDTH), F32),
        pltpu.VMEM((GLA_CHUNK, GLA_WIDTH), F32),
        pltpu.VMEM((GLA_CHUNK, GLA_KW), F32),
        pltpu.VMEM((GLA_CHUNK, GLA_WIDTH), F32),
    ]
    return pl.pallas_call(
        _proj_kernel, out_shape=out_shape, grid=(batch, n_t),
        in_specs=in_specs, out_specs=out_specs, name="proj_gla",
        scratch_shapes=scratch_shapes,
        compiler_params=pltpu.CompilerParams(
            dimension_semantics=("arbitrary", "arbitrary"), vmem_limit_bytes=VMEM_LIMIT),
    )(xf, *consts)


def _attn_kernel(slope_ref, lq1_ref, lk1_ref, lq2_ref, lk2_ref, gain_ref, qt_ref, qn_ref, k_ref,
                 vt_ref, o_ref, s_sc, cm_sc, m_sc, acc_sc, *, lambda_init):
    hg = pl.program_id(1)
    qi = pl.program_id(2)
    m_sc[...] = jnp.full(m_sc.shape, NEG_BIG, F32)
    acc_sc[...] = jnp.zeros(acc_sc.shape, F32)
    group = range(HEADS_PER_STEP)

    half = K_TILE // 2

    def score1(g, j, kb, q_src=qt_ref):
        rows = pl.ds(pl.multiple_of(kb * K_TILE, K_TILE), K_TILE)
        s = jnp.dot(k_ref[0, g, j, rows, :], q_src[0, g, j, 0], preferred_element_type=F32)
        s_sc[g, j] = s
        cm_sc[g, j] = jnp.max(s, axis=0, keepdims=True)

    def absorb1(g, j, kb):
        off = slope_ref[hg * HEADS_PER_STEP + g] * (kb - qi).astype(F32)
        m_prev = m_sc[g, j]
        m_new = jnp.maximum(m_prev, cm_sc[g, j] + off)
        p = jnp.exp2(s_sc[g, j] - (m_new - off))
        alpha = jnp.exp2(m_prev - m_new)
        acc_sc[g, j] = alpha * acc_sc[g, j] + jnp.dot(vt_ref[0, g, kb], p.astype(BF16),
                                                      preferred_element_type=F32)
        m_sc[g, j] = m_new

    def score_diag1(g, j):
        r0 = pl.multiple_of(qi * K_TILE, K_TILE)
        q_t = qt_ref[0, g, j, 0]
        s_a = jnp.dot(k_ref[0, g, j, pl.ds(r0, half), :], q_t, preferred_element_type=F32)
        s_b = jnp.dot(k_ref[0, g, j, pl.ds(r0 + half, half), :], q_t[:, half:],
                      preferred_element_type=F32)
        s_a = jnp.where(lax.broadcasted_iota(jnp.int32, s_a.shape, 0)
                        <= lax.broadcasted_iota(jnp.int32, s_a.shape, 1), s_a, NEG_BIG)
        s_b = jnp.where(lax.broadcasted_iota(jnp.int32, s_b.shape, 0)
                        <= lax.broadcasted_iota(jnp.int32, s_b.shape, 1), s_b, NEG_BIG)
        s_sc[g, j, :half, :] = s_a
        s_sc[g, j, half:, half:] = s_b
        cm_sc[g, j] = jnp.max(s_a, axis=0, keepdims=True)
        cm_sc[g, j, :, half:] = jnp.maximum(cm_sc[g, j, :, half:],
                                            jnp.max(s_b, axis=0, keepdims=True))

    def absorb_diag1(g, j):
        m_prev = m_sc[g, j]
        m_new = jnp.maximum(m_prev, cm_sc[g, j])
        m_sc[g, j] = m_new
        p_a = jnp.exp2(s_sc[g, j, :half, :] - m_new)
        p_b = jnp.exp2(s_sc[g, j, half:, half:] - m_sc[g, j, :, half:])
        alpha = jnp.exp2(m_prev - m_new)
        pv_a = jnp.dot(vt_ref[0, g, qi, :, :half], p_a.astype(BF16), preferred_element_type=F32)
        pv_b = jnp.dot(vt_ref[0, g, qi, :, half:], p_b.astype(BF16), preferred_element_type=F32)
        acc_sc[g, j] = alpha * acc_sc[g, j] + pv_a
        acc_sc[g, j, :, half:] += pv_b

    def score(j, kb):
        for g in group:
            score1(g, j, kb)

    def absorb(j, kb):
        for g in group:
            absorb1(g, j, kb)

    def score_diag(j):
        for g in group:
            score_diag1(g, j)

    def absorb_diag(j):
        for g in group:
            absorb_diag1(g, j)

    def unit(kb):
        score(1, kb)
        absorb(0, kb)
        score(0, kb + 1)
        absorb(1, kb)

    def body(kp, carry):
        for u in range(LOOP_UNROLL):
            unit(LOOP_UNROLL * kp + u)
        return carry

    n_full = jnp.maximum(qi - 1, 0)
    lax.fori_loop(0, n_full // LOOP_UNROLL, body, 0)

    done = (n_full // LOOP_UNROLL) * LOOP_UNROLL
    size = LOOP_UNROLL // 2
    while size >= 1:
        take = (n_full - done) >= size

        @pl.when(take)
        def _(done=done, size=size):
            for u in range(size):
                unit(done + u)

        done = done + jnp.where(take, size, 0)
        size //= 2

    def finish():
        score_diag(1)
        absorb_diag(0)
        absorb_diag(1)
        for g in group:
            score1(g, 0, 0, q_src=qn_ref)
        lam = (jnp.exp(jnp.sum(lq1_ref[...] * lk1_ref[...], axis=1, keepdims=True))
               - jnp.exp(jnp.sum(lq2_ref[...] * lk2_ref[...], axis=1, keepdims=True))
               + lambda_init)
        for g in group:
            a1 = acc_sc[g, 0]
            a2 = acc_sc[g, 1]
            o_t = (a1[:DIFF_V_DIM] / a1[DIFF_V_DIM:DIFF_V_DIM + 1]
                   - lam * (a2[:DIFF_V_DIM] / a2[DIFF_V_DIM:DIFF_V_DIM + 1]))
            o_ref[:, g * DIFF_V_DIM:(g + 1) * DIFF_V_DIM] = (
                _rms(o_t.T, gain_ref[...]) * (1.0 - lambda_init)).astype(BF16)

    @pl.when(qi > 0)
    def _():
        score(1, qi - 1)
        absorb(0, qi - 1)
        score_diag(0)
        absorb(1, qi - 1)
        finish()

    @pl.when(qi == 0)
    def _():
        score_diag(0)
        finish()


def _attn_call(slope_k, lq1, lk1, lq2, lk2, gain, qt, ka, vt, batch, seq, lambda_init):
    n_q = seq // Q_TILE
    n_k = seq // K_TILE
    hps = HEADS_PER_STEP
    smem = pl.BlockSpec(memory_space=pltpu.SMEM)
    in_specs = [
        smem,
        _const_spec(lq1.shape), _const_spec(lk1.shape), _const_spec(lq2.shape),
        _const_spec(lk2.shape), _const_spec(gain.shape),
        pl.BlockSpec((1, hps, 2, 1, LANES, Q_TILE), lambda b, h, q: (b, h, 0, q, 0, 0)),
        pl.BlockSpec((1, hps, 2, 1, LANES, Q_TILE),
                     lambda b, h, q: (b, h, 0, jnp.minimum(q + 1, n_q - 1), 0, 0)),
        pl.BlockSpec((1, hps, 2, seq, LANES), lambda b, h, q: (b, h, 0, 0, 0)),
        pl.BlockSpec((1, hps, n_k, VT_ROWS, K_TILE), lambda b, h, q: (b, h, 0, 0, 0)),
    ]
    return pl.pallas_call(
        functools.partial(_attn_kernel, lambda_init=lambda_init),
        out_shape=jax.ShapeDtypeStruct((batch * seq, DIFF_WIDTH), BF16),
        grid=(batch, DIFF_HEADS // hps, n_q),
        in_specs=in_specs,
        out_specs=pl.BlockSpec((Q_TILE, hps * DIFF_V_DIM), lambda b, h, q: (b * n_q + q, h)),
        scratch_shapes=[pltpu.VMEM((hps, 2, K_TILE, Q_TILE), F32),
                        pltpu.VMEM((hps, 2, 1, Q_TILE), F32),
                        pltpu.VMEM((hps, 2, 1, Q_TILE), F32),
                        pltpu.VMEM((hps, 2, VT_ROWS, Q_TILE), F32)],
        name="diff_attn",
        compiler_params=pltpu.CompilerParams(
            dimension_semantics=("arbitrary", "arbitrary", "arbitrary"),
            vmem_limit_bytes=VMEM_LIMIT),
    )(slope_k, lq1, lk1, lq2, lk2, gain, qt, qt, ka, vt)


def _gla_tile(tril_ref, ind_ref, gain_ref, q_ref, k_ref, v_ref, g_ref, la_ref, o_ref,
              st_ref, b_sc, oi_sc, acc_sc, kf_sc, vf_sc, after_cumsum=None):
    @pl.when(pl.program_id(1) == 0)
    def _():
        st_ref[...] = jnp.zeros(st_ref.shape, F32)

    c_len = GLA_CHUNK
    row = lax.broadcasted_iota(jnp.int32, (c_len, c_len), 0)
    col = lax.broadcasted_iota(jnp.int32, (c_len, c_len), 1)
    causal = col <= row
    nt = (((1,), (1,)), ((), ()))

    def finalize(o, rows, vs):
        g = g_ref[rows, vs].astype(F32)
        return (_rms(o, gain_ref[...]) * (g * jax.nn.sigmoid(g))).astype(BF16)

    n_chunks = GLA_TILE // c_len
    cums = []
    for c in range(n_chunks):
        la = la_ref[c * c_len:(c + 1) * c_len, :]
        hi = la.astype(BF16)
        r1 = la - hi.astype(F32)
        mid = r1.astype(BF16)
        lo = (r1 - mid.astype(F32)).astype(BF16)
        tril = tril_ref[...]
        cums.append(jnp.dot(tril, hi, preferred_element_type=F32)
                    + jnp.dot(tril, mid, preferred_element_type=F32)
                    + jnp.dot(tril, lo, preferred_element_type=F32))

    if after_cumsum is not None:
        after_cumsum()

    overflow = []
    for c in range(n_chunks):
        rows = slice(c * c_len, (c + 1) * c_len)
        b = cums[c]
        b_sc[rows, :] = b
        b_last = b[c_len - 1:c_len, :]
        overflow.append(jnp.max(-b_last) > EXP_CLAMP)
        q = q_ref[rows, :].astype(F32)
        k = k_ref[rows, :].astype(F32)
        q_in = (q * jnp.exp(b)).astype(BF16)
        k_out = (k * jnp.exp(jnp.minimum(-b, EXP_CLAMP))).astype(BF16)
        k_dec = (k * jnp.exp(b_last - b)).astype(BF16)
        d_last = jnp.exp(b_last)
        for h in range(GLA_HEADS):
            ks = slice(h * GLA_K_DIM, (h + 1) * GLA_K_DIM)
            vs = slice(h * GLA_V_DIM, (h + 1) * GLA_V_DIM)
            a = lax.dot_general(q_in[:, ks], k_out[:, ks], nt, preferred_element_type=F32)
            a = jnp.where(causal, a, 0.0).astype(BF16)
            vh = v_ref[rows, vs]
            s_t = st_ref[h]
            o_inter = lax.dot_general(q_in[:, ks], s_t.astype(BF16), nt,
                                      preferred_element_type=F32)
            oi_sc[rows, vs] = o_inter
            o = o_inter + jnp.dot(a, vh, preferred_element_type=F32)
            v_t = vh.astype(F32).T.astype(BF16)
            st_ref[h] = s_t * d_last[:, ks] + jnp.dot(v_t, k_dec[:, ks], preferred_element_type=F32)
            o_ref[rows, vs] = finalize(o, rows, vs)

    def exact_intra(c):
        rows = slice(c * c_len, (c + 1) * c_len)
        q = q_ref[rows, :].astype(F32)
        b = b_sc[rows, :]
        kf_sc[...] = k_ref[rows, :].astype(F32)
        vf_sc[...] = v_ref[rows, :].astype(F32)
        acc_sc[...] = jnp.zeros(acc_sc.shape, F32)
        t_idx = lax.broadcasted_iota(jnp.int32, (c_len, GLA_KW), 0)

        def body(s, carry):
            k_s = kf_sc[pl.ds(s, 1), :]
            b_s = b_sc[pl.ds(c * c_len + s, 1), :]
            v_s = vf_sc[pl.ds(s, 1), :]
            w = q * k_s * jnp.exp(jnp.minimum(b - b_s, 0.0))
            w = jnp.where(t_idx >= s, w, 0.0).astype(BF16)
            a_s = jnp.dot(w, ind_ref[...], preferred_element_type=F32)
            acc_sc[...] += a_s * v_s
            return carry

        lax.fori_loop(0, c_len, body, 0)
        for h in range(GLA_HEADS):
            vs = slice(h * GLA_V_DIM, (h + 1) * GLA_V_DIM)
            o_ref[rows, vs] = finalize(oi_sc[rows, vs] + acc_sc[:, vs], rows, vs)

    return [functools.partial(exact_intra, c) for c in range(n_chunks)], overflow


def _ffn_kernel(x_ref, od_ref, og_ref, gf_ref, wo_ref, wg_ref, wu_ref, wd_ref, o_ref,
                wo_sc, wg_sc, wu_sc, wd_sc):
    i = pl.program_id(0)

    @pl.when(i < FFN_WARM)
    def _():
        for src, dst in ((wo_ref, wo_sc), (wg_ref, wg_sc), (wu_ref, wu_sc), (wd_ref, wd_sc)):
            rows = src.shape[0]
            dst[pl.ds(pl.multiple_of(i * rows, rows), rows), :] = src[...].astype(BF16)

    @pl.when(i >= FFN_WARM)
    def _():
        h = (x_ref[...]
             + jnp.dot(od_ref[...], wo_sc[:DIFF_WIDTH, :], preferred_element_type=F32)
             + jnp.dot(og_ref[...], wo_sc[DIFF_WIDTH:, :], preferred_element_type=F32))
        m = _rms(h, gf_ref[...]).astype(BF16)
        o_ref[...] = h
        for c0, c1 in FF_CHUNKS:
            gate = jnp.dot(m, wg_sc[:, c0:c1], preferred_element_type=F32)
            up = jnp.dot(m, wu_sc[:, c0:c1], preferred_element_type=F32)
            act = (gate * jax.nn.sigmoid(gate) * up).astype(BF16)
            o_ref[...] += jnp.dot(act, wd_sc[c0:c1, :], preferred_element_type=F32)


def _ffn_call(xf, od, og, gf, wo, wg, wu, wd):
    n = xf.shape[0]
    row = lambda i: (jnp.maximum(i - FFN_WARM, 0), 0)
    chunk = lambda i: (jnp.minimum(i, FFN_WARM - 1), 0)

    def chunk_spec(w):
        assert w.shape[0] % (FFN_WARM * 16) == 0
        return pl.BlockSpec((w.shape[0] // FFN_WARM, w.shape[1]), chunk)

    in_specs = [
        pl.BlockSpec((ROW_TILE, D_MODEL), row),
        pl.BlockSpec((ROW_TILE, DIFF_WIDTH), row),
        pl.BlockSpec((ROW_TILE, GLA_WIDTH), row),
        _const_spec(gf.shape),
        chunk_spec(wo), chunk_spec(wg), chunk_spec(wu), chunk_spec(wd),
    ]
    return pl.pallas_call(
        _ffn_kernel,
        out_shape=jax.ShapeDtypeStruct((n, D_MODEL), F32),
        grid=(FFN_WARM + n // ROW_TILE,),
        in_specs=in_specs,
        out_specs=pl.BlockSpec((ROW_TILE, D_MODEL), row),
        scratch_shapes=[pltpu.VMEM(w.shape, BF16) for w in (wo, wg, wu, wd)],
        name="out_ffn",
        compiler_params=pltpu.CompilerParams(
            dimension_semantics=("arbitrary",), vmem_limit_bytes=VMEM_LIMIT),
    )(xf, od, og, gf, wo, wg, wu, wd)


def _bf16_pieces(value, n=3):
    pieces = []
    rest = np.float32(value)
    for _ in range(n):
        piece = np.float32(np.asarray(rest, dtype=BF16))
        pieces.append(float(piece))
        rest = np.float32(rest - piece)
    return pieces


def _alibi_tables():
    assert ROW_TILE == Q_TILE == K_TILE
    pos = np.arange(ROW_TILE)
    a = (pos // ALIBI_SPLIT) * ALIBI_SPLIT
    c = pos % ALIBI_SPLIT
    extra = LANES - DIFF_HEAD_DIM
    cq = np.zeros((DIFF_HEADS, extra, ROW_TILE), np.float32)
    ck = np.zeros((DIFF_HEADS, extra, ROW_TILE), np.float32)
    for h in range(DIFF_HEADS):
        r = 0
        for piece in _bf16_pieces(_slope(h) * LOG2E):
            for q_row, k_row in ((piece, a), (piece, c), (-a, piece), (-c, piece)):
                cq[h, r, :] = q_row
                ck[h, r, :] = k_row
                r += 1
        assert r <= extra
    return jnp.asarray(cq, dtype=BF16), jnp.asarray(ck)


def _slope(h):
    return 2.0 ** (-8.0 * (h + 1) / DIFF_HEADS)


def kernel(x, attn_norm_gain, w_in, q_norm_gain, k_norm_gain, lambda_q1, lambda_k1, lambda_q2,
           lambda_k2, diff_out_norm_gain, w_gla_gate_up, b_gla_gate, gla_out_norm_gain, w_out,
           ffn_norm_gain, w_ffn_gate, w_ffn_up, w_ffn_down):
    batch, seq, _ = x.shape
    depth = w_in.shape[0]
    assert seq % ROW_TILE == 0 and seq % GLA_TILE == 0

    cq, ck = _alibi_tables()
    tril = jnp.asarray(np.tril(np.ones((GLA_CHUNK, GLA_CHUNK), np.float32)), dtype=BF16)
    ind = jnp.asarray((np.arange(GLA_KW)[:, None] // GLA_K_DIM
                       == np.arange(GLA_WIDTH)[None, :] // GLA_V_DIM).astype(np.float32), dtype=BF16)
    slope_k = jnp.asarray(
        [sum(_bf16_pieces(_slope(h) * LOG2E)) * K_TILE for h in range(DIFF_HEADS)], F32)

    hcur = x.reshape(batch * seq, D_MODEL)
    for l in range(depth):
        lambda_init = 0.8 - 0.6 * math.exp(-0.3 * l)
        w_main = w_in[l].T
        wup = w_gla_gate_up[l].astype(BF16)
        row2 = lambda a: a.reshape(1, -1).astype(F32)
        col2 = lambda a: a.reshape(-1, 1).astype(F32)

        qt, ka, vt, o_gla = _proj_call(
            hcur, row2(attn_norm_gain[l]), w_main, col2(q_norm_gain[l]), col2(k_norm_gain[l]),
            cq, ck, wup, row2(b_gla_gate[l]), tril, ind, row2(gla_out_norm_gain[l]), batch, seq)

        o_diff = _attn_call(slope_k, row2(lambda_q1[l]), row2(lambda_k1[l]), row2(lambda_q2[l]),
                            row2(lambda_k2[l]), row2(diff_out_norm_gain[l]), qt, ka, vt,
                            batch, seq, lambda_init)

        hcur = _ffn_call(hcur, o_diff, o_gla, row2(ffn_norm_gain[l]), w_out[l], w_ffn_gate[l],
                         w_ffn_up[l], w_ffn_down[l])
    return hcur.reshape(batch, seq, D_MODEL)
```

```python
import functools
import math

import jax
import jax.numpy as jnp
import numpy as np
from jax import lax
from jax.experimental import pallas as pl
from jax.experimental.pallas import tpu as pltpu

F32 = jnp.float32
BF16 = jnp.bfloat16

D_MODEL = 1024
DIFF_HEADS = 4
DIFF_HEAD_DIM = 64
DIFF_V_DIM = 128
DIFF_WIDTH = DIFF_HEADS * DIFF_V_DIM
GLA_HEADS = 4
GLA_K_DIM = 64
GLA_V_DIM = 128
GLA_WIDTH = GLA_HEADS * GLA_V_DIM
GLA_KW = GLA_HEADS * GLA_K_DIM
GLA_GATE_RANK = 16
GLA_GATE_TEMP = 16.0
D_FF = 2816
EPS = 1e-6
COL_DQ = 0
COL_DK = COL_DQ + DIFF_HEADS * 2 * DIFF_HEAD_DIM
COL_DV = COL_DK + DIFF_HEADS * 2 * DIFF_HEAD_DIM
COL_GQ = COL_DV + DIFF_WIDTH
COL_GK = COL_GQ + GLA_KW
COL_GV = COL_GK + GLA_KW
COL_GO = COL_GV + GLA_WIDTH
MAIN_WIDTH = COL_GO + GLA_WIDTH
IN_WIDTH = MAIN_WIDTH + GLA_GATE_RANK

LANES = 128
ROW_TILE = 512
Q_TILE = 512
VT_ROWS = 144
K_TILE = 512
LOOP_UNROLL = 4
HEADS_PER_STEP = 2
GLA_TILE = 512
GLA_CHUNK = 256
MXU_DIM = 256
WEIGHT_CAST_COLS = 512
FFN_WARM = 8
FF_CHUNKS = ((0, D_FF),)
ALIBI_SPLIT = 16
NEG_BIG = -1e30
LOG2E = math.log2(math.e)
EXP_CLAMP = 80.0
VMEM_LIMIT = 56 * 1024 * 1024


def _rms(x, gain):
    return x * lax.rsqrt(jnp.mean(x * x, axis=-1, keepdims=True) + EPS) * gain


def _proj_kernel(x_ref, ga_ref, w_ref, gq_ref, gk_ref, cq_ref, ck_ref,
                 wup_ref, bg_ref, tril_ref, ind_ref, ggain_ref,
                 qt_ref, ka_ref, vt_ref, og_ref,
                 wb_sc, gq_out, gk_out, gv_out, go_out, la_out,
                 st_ref, b_sc, oi_sc, acc_sc, kf_sc, vf_sc):
    @pl.when((pl.program_id(0) == 0) & (pl.program_id(1) == 0))
    def _():
        for c0 in range(0, IN_WIDTH, WEIGHT_CAST_COLS):
            c1 = min(c0 + WEIGHT_CAST_COLS, IN_WIDTH)
            wb_sc[:, c0:c1] = w_ref[c0:c1, :].T.astype(BF16)

    nb = _rms(x_ref[...], ga_ref[...]).astype(BF16)

    def proj(c0, c1):
        return jnp.dot(nb, wb_sc[:, c0:c1], preferred_element_type=F32)

    def head_norm_t(x_t, gain_col):
        ms = jnp.mean(x_t * x_t, axis=0, keepdims=True)
        return x_t * lax.rsqrt(ms + EPS) * gain_col

    d = DIFF_HEAD_DIM
    q_scale = d ** -0.5 * LOG2E
    ones_rows = (lax.broadcasted_iota(jnp.int32, (VT_ROWS - DIFF_V_DIM, ROW_TILE), 0) == 0)

    def emit_qkv(dq, dk, dv):
        for h in range(DIFF_HEADS):
            q_t = dq[:, h * LANES:(h + 1) * LANES].T
            k_t = dk[:, h * LANES:(h + 1) * LANES].T
            for j in range(2):
                yq = head_norm_t(q_t[j * d:(j + 1) * d], gq_ref[...]) * q_scale
                yk = head_norm_t(k_t[j * d:(j + 1) * d], gk_ref[...])
                qt_ref[0, h, j, 0, :d, :] = yq.astype(BF16)
                qt_ref[0, h, j, 0, d:, :] = cq_ref[h]
                ka_ref[0, h, j] = jnp.concatenate([yk, ck_ref[h]], axis=0).T.astype(BF16)
            vt_ref[0, h, 0, :DIFF_V_DIM, :] = dv[:, h * LANES:(h + 1) * LANES].T.astype(BF16)
            vt_ref[0, h, 0, DIFF_V_DIM:, :] = ones_rows.astype(BF16)

    gla_cols = proj(COL_GQ, IN_WIDTH)
    gq_out[...] = (gla_cols[:, :GLA_KW] * (GLA_K_DIM ** -0.5)).astype(BF16)
    gk_out[...] = gla_cols[:, COL_GK - COL_GQ:COL_GV - COL_GQ].astype(BF16)
    gv_out[...] = gla_cols[:, COL_GV - COL_GQ:COL_GO - COL_GQ].astype(BF16)
    go_out[...] = gla_cols[:, COL_GO - COL_GQ:MAIN_WIDTH - COL_GQ].astype(BF16)
    glr = gla_cols[:, MAIN_WIDTH - COL_GQ:]
    dq = proj(COL_DQ, COL_DK)
    z = jnp.dot(glr.astype(BF16), wup_ref[...], preferred_element_type=F32) + bg_ref[...]
    log_sig = jnp.minimum(z, 0.0) - jnp.log(1.0 + jnp.exp(-jnp.abs(z)))
    la_out[...] = log_sig * (1.0 / GLA_GATE_TEMP)

    kv = []

    def project_kv():
        kv.extend([proj(COL_DK, COL_DV), proj(COL_DV, COL_GQ)])

    redo, overflow = _gla_tile(tril_ref, ind_ref, ggain_ref, gq_out, gk_out, gv_out, go_out, la_out,
                               og_ref, st_ref, b_sc, oi_sc, acc_sc, kf_sc, vf_sc,
                               after_cumsum=project_kv)
    emit_qkv(dq, *kv)
    for fn, flag in zip(redo, overflow):
        pl.when(flag)(fn)


def _const_spec(shape):
    nd = len(shape)
    return pl.BlockSpec(shape, lambda *_: (0,) * nd, pipeline_mode=pl.Buffered(1))


def _proj_call(xf, ga, w_main, gq, gk, cq, ck, wup, bg, tril, ind, ggain, batch, seq):
    assert ROW_TILE == GLA_TILE
    n_t = seq // ROW_TILE
    n = batch * seq
    row = lambda b, t: (b * n_t + t, 0)
    out_shape = (
        jax.ShapeDtypeStruct((batch, DIFF_HEADS, 2, seq // Q_TILE, LANES, Q_TILE), BF16),
        jax.ShapeDtypeStruct((batch, DIFF_HEADS, 2, seq, LANES), BF16),
        jax.ShapeDtypeStruct((batch, DIFF_HEADS, seq // K_TILE, VT_ROWS, K_TILE), BF16),
        jax.ShapeDtypeStruct((n, GLA_WIDTH), BF16),
    )
    out_specs = (
        pl.BlockSpec((1, DIFF_HEADS, 2, 1, LANES, Q_TILE), lambda b, t: (b, 0, 0, t, 0, 0)),
        pl.BlockSpec((1, DIFF_HEADS, 2, ROW_TILE, LANES), lambda b, t: (b, 0, 0, t, 0)),
        pl.BlockSpec((1, DIFF_HEADS, 1, VT_ROWS, K_TILE), lambda b, t: (b, 0, t, 0, 0)),
        pl.BlockSpec((ROW_TILE, GLA_WIDTH), row),
    )
    consts = (ga, w_main, gq, gk, cq, ck, wup, bg, tril, ind, ggain)
    in_specs = [pl.BlockSpec((ROW_TILE, D_MODEL), row)] + [_const_spec(a.shape) for a in consts]
    scratch_shapes = [
        pltpu.VMEM((D_MODEL, IN_WIDTH), BF16),
        pltpu.VMEM((ROW_TILE, GLA_KW), BF16),
        pltpu.VMEM((ROW_TILE, GLA_KW), BF16),
        pltpu.VMEM((ROW_TILE, GLA_WIDTH), BF16),
        pltpu.VMEM((ROW_TILE, GLA_WIDTH), BF16),
        pltpu.VMEM((ROW_TILE, GLA_KW), F32),
        pltpu.VMEM((GLA_HEADS, GLA_V_DIM, GLA_K_DIM), F32),
        pltpu.VMEM((GLA_TILE, GLA_KW), F32),
        pltpu.VMEM((GLA_TILE, GLA_WIDTH), F32),
        pltpu.VMEM((GLA_CHUNK, GLA_WIDTH), F32),
        pltpu.VMEM((GLA_CHUNK, GLA_KW), F32),
        pltpu.VMEM((GLA_CHUNK, GLA_WIDTH), F32),
    ]
    return pl.pallas_call(
        _proj_kernel, out_shape=out_shape, grid=(batch, n_t),
        in_specs=in_specs, out_specs=out_specs, name="proj_gla",
        scratch_shapes=scratch_shapes,
        compiler_params=pltpu.CompilerParams(
            dimension_semantics=("arbitrary", "arbitrary"), vmem_limit_bytes=VMEM_LIMIT),
    )(xf, *consts)


def _attn_kernel(slope_ref, lq1_ref, lk1_ref, lq2_ref, lk2_ref, gain_ref, qt_ref, qn_ref, k_ref,
                 vt_ref, o_ref, s_sc, cm_sc, m_sc, acc_sc, *, lambda_init):
    hg = pl.program_id(1)
    qi = pl.program_id(2)
    m_sc[...] = jnp.full(m_sc.shape, NEG_BIG, F32)
    acc_sc[...] = jnp.zeros(acc_sc.shape, F32)
    group = range(HEADS_PER_STEP)

    half = K_TILE // 2

    def score1(g, j, kb, q_src=qt_ref):
        rows = pl.ds(pl.multiple_of(kb * K_TILE, K_TILE), K_TILE)
        s = jnp.dot(k_ref[0, g, j, rows, :], q_src[0, g, j, 0], preferred_element_type=F32)
        s_sc[g, j] = s
        cm_sc[g, j] = jnp.max(s, axis=0, keepdims=True)

    def absorb1(g, j, kb):
        off = slope_ref[hg * HEADS_PER_STEP + g] * (kb - qi).astype(F32)
        m_prev = m_sc[g, j]
        m_new = jnp.maximum(m_prev, cm_sc[g, j] + off)
        p = jnp.exp2(s_sc[g, j] - (m_new - off))
        alpha = jnp.exp2(m_prev - m_new)
        acc_sc[g, j] = alpha * acc_sc[g, j] + jnp.dot(vt_ref[0, g, kb], p.astype(BF16),
                                                      preferred_element_type=F32)
        m_sc[g, j] = m_new

    def score_diag1(g, j):
        r0 = pl.multiple_of(qi * K_TILE, K_TILE)
        q_t = qt_ref[0, g, j, 0]
        s_a = jnp.dot(k_ref[0, g, j, pl.ds(r0, half), :], q_t, preferred_element_type=F32)
        s_b = jnp.dot(k_ref[0, g, j, pl.ds(r0 + half, half), :], q_t[:, half:],
                      preferred_element_type=F32)
        s_a = jnp.where(lax.broadcasted_iota(jnp.int32, s_a.shape, 0)
                        <= lax.broadcasted_iota(jnp.int32, s_a.shape, 1), s_a, NEG_BIG)
        s_b = jnp.where(lax.broadcasted_iota(jnp.int32, s_b.shape, 0)
                        <= lax.broadcasted_iota(jnp.int32, s_b.shape, 1), s_b, NEG_BIG)
        s_sc[g, j, :half, :] = s_a
        s_sc[g, j, half:, half:] = s_b
        cm_sc[g, j] = jnp.max(s_a, axis=0, keepdims=True)
        cm_sc[g, j, :, half:] = jnp.maximum(cm_sc[g, j, :, half:],
                                            jnp.max(s_b, axis=0, keepdims=True))

    def absorb_diag1(g, j):
        m_prev = m_sc[g, j]
        m_new = jnp.maximum(m_prev, cm_sc[g, j])
        m_sc[g, j] = m_new
        p_a = jnp.exp2(s_sc[g, j, :half, :] - m_new)
        p_b = jnp.exp2(s_sc[g, j, half:, half:] - m_sc[g, j, :, half:])
        alpha = jnp.exp2(m_prev - m_new)
        pv_a = jnp.dot(vt_ref[0, g, qi, :, :half], p_a.astype(BF16), preferred_element_type=F32)
        pv_b = jnp.dot(vt_ref[0, g, qi, :, half:], p_b.astype(BF16), preferred_element_type=F32)
        acc_sc[g, j] = alpha * acc_sc[g, j] + pv_a
        acc_sc[g, j, :, half:] += pv_b

    def score(j, kb):
        for g in group:
            score1(g, j, kb)

    def absorb(j, kb):
        for g in group:
            absorb1(g, j, kb)

    def score_diag(j):
        for g in group:
            score_diag1(g, j)

    def absorb_diag(j):
        for g in group:
            absorb_diag1(g, j)

    def unit(kb):
        for g in group:
            score1(g, 1, kb)
            absorb1(g, 0, kb)
        for g in group:
            score1(g, 0, kb + 1)
            absorb1(g, 1, kb)

    def body(kp, carry):
        for u in range(LOOP_UNROLL):
            unit(LOOP_UNROLL * kp + u)
        return carry

    n_full = jnp.maximum(qi - 1, 0)
    lax.fori_loop(0, n_full // LOOP_UNROLL, body, 0)

    done = (n_full // LOOP_UNROLL) * LOOP_UNROLL
    size = LOOP_UNROLL // 2
    while size >= 1:
        take = (n_full - done) >= size

        @pl.when(take)
        def _(done=done, size=size):
            for u in range(size):
                unit(done + u)

        done = done + jnp.where(take, size, 0)
        size //= 2

    def finish():
        score_diag(1)
        absorb_diag(0)
        absorb_diag(1)
        for g in group:
            score1(g, 0, 0, q_src=qn_ref)
        lam = (jnp.exp(jnp.sum(lq1_ref[...] * lk1_ref[...], axis=1, keepdims=True))
               - jnp.exp(jnp.sum(lq2_ref[...] * lk2_ref[...], axis=1, keepdims=True))
               + lambda_init)
        for g in group:
            a1 = acc_sc[g, 0]
            a2 = acc_sc[g, 1]
            o_t = (a1[:DIFF_V_DIM] / a1[DIFF_V_DIM:DIFF_V_DIM + 1]
                   - lam * (a2[:DIFF_V_DIM] / a2[DIFF_V_DIM:DIFF_V_DIM + 1]))
            o_ref[:, g * DIFF_V_DIM:(g + 1) * DIFF_V_DIM] = (
                _rms(o_t.T, gain_ref[...]) * (1.0 - lambda_init)).astype(BF16)

    @pl.when(qi > 0)
    def _():
        score(1, qi - 1)
        absorb(0, qi - 1)
        score_diag(0)
        absorb(1, qi - 1)
        finish()

    @pl.when(qi == 0)
    def _():
        score_diag(0)
        finish()


def _attn_call(slope_k, lq1, lk1, lq2, lk2, gain, qt, ka, vt, batch, seq, lambda_init):
    n_q = seq // Q_TILE
    n_k = seq // K_TILE
    hps = HEADS_PER_STEP
    smem = pl.BlockSpec(memory_space=pltpu.SMEM)
    in_specs = [
        smem,
        _const_spec(lq1.shape), _const_spec(lk1.shape), _const_spec(lq2.shape),
        _const_spec(lk2.shape), _const_spec(gain.shape),
        pl.BlockSpec((1, hps, 2, 1, LANES, Q_TILE), lambda b, h, q: (b, h, 0, q, 0, 0)),
        pl.BlockSpec((1, hps, 2, 1, LANES, Q_TILE),
                     lambda b, h, q: (b, h, 0, jnp.minimum(q + 1, n_q - 1), 0, 0)),
        pl.BlockSpec((1, hps, 2, seq, LANES), lambda b, h, q: (b, h, 0, 0, 0)),
        pl.BlockSpec((1, hps, n_k, VT_ROWS, K_TILE), lambda b, h, q: (b, h, 0, 0, 0)),
    ]
    return pl.pallas_call(
        functools.partial(_attn_kernel, lambda_init=lambda_init),
        out_shape=jax.ShapeDtypeStruct((batch * seq, DIFF_WIDTH), BF16),
        grid=(batch, DIFF_HEADS // hps, n_q),
        in_specs=in_specs,
        out_specs=pl.BlockSpec((Q_TILE, hps * DIFF_V_DIM), lambda b, h, q: (b * n_q + q, h)),
        scratch_shapes=[pltpu.VMEM((hps, 2, K_TILE, Q_TILE), F32),
                        pltpu.VMEM((hps, 2, 1, Q_TILE), F32),
                        pltpu.VMEM((hps, 2, 1, Q_TILE), F32),
                        pltpu.VMEM((hps, 2, VT_ROWS, Q_TILE), F32)],
        name="diff_attn",
        compiler_params=pltpu.CompilerParams(
            dimension_semantics=("arbitrary", "arbitrary", "arbitrary"),
            vmem_limit_bytes=VMEM_LIMIT),
    )(slope_k, lq1, lk1, lq2, lk2, gain, qt, qt, ka, vt)


def _gla_tile(tril_ref, ind_ref, gain_ref, q_ref, k_ref, v_ref, g_ref, la_ref, o_ref,
              st_ref, b_sc, oi_sc, acc_sc, kf_sc, vf_sc, after_cumsum=None):
    @pl.when(pl.program_id(1) == 0)
    def _():
        st_ref[...] = jnp.zeros(st_ref.shape, F32)

    c_len = GLA_CHUNK
    row = lax.broadcasted_iota(jnp.int32, (c_len, c_len), 0)
    col = lax.broadcasted_iota(jnp.int32, (c_len, c_len), 1)
    causal = col <= row
    nt = (((1,), (1,)), ((), ()))

    def finalize(o, rows, vs):
        g = g_ref[rows, vs].astype(F32)
        return (_rms(o, gain_ref[...]) * (g * jax.nn.sigmoid(g))).astype(BF16)

    n_chunks = GLA_TILE // c_len
    cums = []
    for c in range(n_chunks):
        la = la_ref[c * c_len:(c + 1) * c_len, :]
        hi = la.astype(BF16)
        r1 = la - hi.astype(F32)
        mid = r1.astype(BF16)
        lo = (r1 - mid.astype(F32)).astype(BF16)
        tril = tril_ref[...]
        cums.append(jnp.dot(tril, hi, preferred_element_type=F32)
                    + jnp.dot(tril, mid, preferred_element_type=F32)
                    + jnp.dot(tril, lo, preferred_element_type=F32))

    if after_cumsum is not None:
        after_cumsum()

    overflow = []
    for c in range(n_chunks):
        rows = slice(c * c_len, (c + 1) * c_len)
        b = cums[c]
        b_sc[rows, :] = b
        b_last = b[c_len - 1:c_len, :]
        overflow.append(jnp.max(-b_last) > EXP_CLAMP)
        q = q_ref[rows, :].astype(F32)
        k = k_ref[rows, :].astype(F32)
        q_in = (q * jnp.exp(b)).astype(BF16)
        k_out = (k * jnp.exp(jnp.minimum(-b, EXP_CLAMP))).astype(BF16)
        k_dec = (k * jnp.exp(b_last - b)).astype(BF16)
        d_last = jnp.exp(b_last)
        for h in range(GLA_HEADS):
            ks = slice(h * GLA_K_DIM, (h + 1) * GLA_K_DIM)
            vs = slice(h * GLA_V_DIM, (h + 1) * GLA_V_DIM)
            a = lax.dot_general(q_in[:, ks], k_out[:, ks], nt, preferred_element_type=F32)
            a = jnp.where(causal, a, 0.0).astype(BF16)
            vh = v_ref[rows, vs]
            s_t = st_ref[h]
            o_inter = lax.dot_general(q_in[:, ks], s_t.astype(BF16), nt,
                                      preferred_element_type=F32)
            oi_sc[rows, vs] = o_inter
            o = o_inter + jnp.dot(a, vh, preferred_element_type=F32)
            v_t = vh.astype(F32).T.astype(BF16)
            st_ref[h] = s_t * d_last[:, ks] + jnp.dot(v_t, k_dec[:, ks], preferred_element_type=F32)
            o_ref[rows, vs] = finalize(o, rows, vs)

    def exact_intra(c):
        rows = slice(c * c_len, (c + 1) * c_len)
        q = q_ref[rows, :].astype(F32)
        b = b_sc[rows, :]
        kf_sc[...] = k_ref[rows, :].astype(F32)
        vf_sc[...] = v_ref[rows, :].astype(F32)
        acc_sc[...] = jnp.zeros(acc_sc.shape, F32)
        t_idx = lax.broadcasted_iota(jnp.int32, (c_len, GLA_KW), 0)

        def body(s, carry):
            k_s = kf_sc[pl.ds(s, 1), :]
            b_s = b_sc[pl.ds(c * c_len + s, 1), :]
            v_s = vf_sc[pl.ds(s, 1), :]
            w = q * k_s * jnp.exp(jnp.minimum(b - b_s, 0.0))
            w = jnp.where(t_idx >= s, w, 0.0).astype(BF16)
            a_s = jnp.dot(w, ind_ref[...], preferred_element_type=F32)
            acc_sc[...] += a_s * v_s
            return carry

        lax.fori_loop(0, c_len, body, 0)
        for h in range(GLA_HEADS):
            vs = slice(h * GLA_V_DIM, (h + 1) * GLA_V_DIM)
            o_ref[rows, vs] = finalize(oi_sc[rows, vs] + acc_sc[:, vs], rows, vs)

    return [functools.partial(exact_intra, c) for c in range(n_chunks)], overflow


def _ffn_kernel(x_ref, od_ref, og_ref, gf_ref, wo_ref, wg_ref, wu_ref, wd_ref, o_ref,
                wo_sc, wg_sc, wu_sc, wd_sc):
    i = pl.program_id(0)

    @pl.when(i < FFN_WARM)
    def _():
        for src, dst in ((wo_ref, wo_sc), (wg_ref, wg_sc), (wu_ref, wu_sc), (wd_ref, wd_sc)):
            rows = src.shape[0]
            dst[pl.ds(pl.multiple_of(i * rows, rows), rows), :] = src[...].astype(BF16)

    @pl.when(i >= FFN_WARM)
    def _():
        h = (x_ref[...]
             + jnp.dot(od_ref[...], wo_sc[:DIFF_WIDTH, :], preferred_element_type=F32)
             + jnp.dot(og_ref[...], wo_sc[DIFF_WIDTH:, :], preferred_element_type=F32))
        m = _rms(h, gf_ref[...]).astype(BF16)
        o_ref[...] = h
        for c0, c1 in FF_CHUNKS:
            gate = jnp.dot(m, wg_sc[:, c0:c1], preferred_element_type=F32)
            up = jnp.dot(m, wu_sc[:, c0:c1], preferred_element_type=F32)
            act = (gate * jax.nn.sigmoid(gate) * up).astype(BF16)
            o_ref[...] += jnp.dot(act, wd_sc[c0:c1, :], preferred_element_type=F32)


def _ffn_call(xf, od, og, gf, wo, wg, wu, wd):
    n = xf.shape[0]
    row = lambda i: (jnp.maximum(i - FFN_WARM, 0), 0)
    chunk = lambda i: (jnp.minimum(i, FFN_WARM - 1), 0)

    def chunk_spec(w):
        assert w.shape[0] % (FFN_WARM * 16) == 0
        return pl.BlockSpec((w.shape[0] // FFN_WARM, w.shape[1]), chunk)

    in_specs = [
        pl.BlockSpec((ROW_TILE, D_MODEL), row),
        pl.BlockSpec((ROW_TILE, DIFF_WIDTH), row),
        pl.BlockSpec((ROW_TILE, GLA_WIDTH), row),
        _const_spec(gf.shape),
        chunk_spec(wo), chunk_spec(wg), chunk_spec(wu), chunk_spec(wd),
    ]
    return pl.pallas_call(
        _ffn_kernel,
        out_shape=jax.ShapeDtypeStruct((n, D_MODEL), F32),
        grid=(FFN_WARM + n // ROW_TILE,),
        in_specs=in_specs,
        out_specs=pl.BlockSpec((ROW_TILE, D_MODEL), row),
        scratch_shapes=[pltpu.VMEM(w.shape, BF16) for w in (wo, wg, wu, wd)],
        name="out_ffn",
        compiler_params=pltpu.CompilerParams(
            dimension_semantics=("arbitrary",), vmem_limit_bytes=VMEM_LIMIT),
    )(xf, od, og, gf, wo, wg, wu, wd)


def _bf16_pieces(value, n=3):
    pieces = []
    rest = np.float32(value)
    for _ in range(n):
        piece = np.float32(np.asarray(rest, dtype=BF16))
        pieces.append(float(piece))
        rest = np.float32(rest - piece)
    return pieces


def _alibi_tables():
    assert ROW_TILE == Q_TILE == K_TILE
    pos = np.arange(ROW_TILE)
    a = (pos // ALIBI_SPLIT) * ALIBI_SPLIT
    c = pos % ALIBI_SPLIT
    extra = LANES - DIFF_HEAD_DIM
    cq = np.zeros((DIFF_HEADS, extra, ROW_TILE), np.float32)
    ck = np.zeros((DIFF_HEADS, extra, ROW_TILE), np.float32)
    for h in range(DIFF_HEADS):
        r = 0
        for piece in _bf16_pieces(_slope(h) * LOG2E):
            for q_row, k_row in ((piece, a), (piece, c), (-a, piece), (-c, piece)):
                cq[h, r, :] = q_row
                ck[h, r, :] = k_row
                r += 1
        assert r <= extra
    return jnp.asarray(cq, dtype=BF16), jnp.asarray(ck)


def _slope(h):
    return 2.0 ** (-8.0 * (h + 1) / DIFF_HEADS)


def kernel(x, attn_norm_gain, w_in, q_norm_gain, k_norm_gain, lambda_q1, lambda_k1, lambda_q2,
           lambda_k2, diff_out_norm_gain, w_gla_gate_up, b_gla_gate, gla_out_norm_gain, w_out,
           ffn_norm_gain, w_ffn_gate, w_ffn_up, w_ffn_down):
    batch, seq, _ = x.shape
    depth = w_in.shape[0]
    assert seq % ROW_TILE == 0 and seq % GLA_TILE == 0

    cq, ck = _alibi_tables()
    tril = jnp.asarray(np.tril(np.ones((GLA_CHUNK, GLA_CHUNK), np.float32)), dtype=BF16)
    ind = jnp.asarray((np.arange(GLA_KW)[:, None] // GLA_K_DIM
                       == np.arange(GLA_WIDTH)[None, :] // GLA_V_DIM).astype(np.float32), dtype=BF16)
    slope_k = jnp.asarray(
        [sum(_bf16_pieces(_slope(h) * LOG2E)) * K_TILE for h in range(DIFF_HEADS)], F32)

    hcur = x.reshape(batch * seq, D_MODEL)
    for l in range(depth):
        lambda_init = 0.8 - 0.6 * math.exp(-0.3 * l)
        w_main = w_in[l].T
        wup = w_gla_gate_up[l].astype(BF16)
        row2 = lambda a: a.reshape(1, -1).astype(F32)
        col2 = lambda a: a.reshape(-1, 1).astype(F32)

        qt, ka, vt, o_gla = _proj_call(
            hcur, row2(attn_norm_gain[l]), w_main, col2(q_norm_gain[l]), col2(k_norm_gain[l]),
            cq, ck, wup, row2(b_gla_gate[l]), tril, ind, row2(gla_out_norm_gain[l]), batch, seq)

        o_diff = _attn_call(slope_k, row2(lambda_q1[l]), row2(lambda_k1[l]), row2(lambda_q2[l]),
                            row2(lambda_k2[l]), row2(diff_out_norm_gain[l]), qt, ka, vt,
                            batch, seq, lambda_init)

        hcur = _ffn_call(hcur, o_diff, o_gla, row2(ffn_norm_gain[l]), w_out[l], w_ffn_gate[l],
                         w_ffn_up[l], w_ffn_down[l])
    return hcur.reshape(batch, seq, D_MODEL)
```

```python
import functools
import math

import jax
import jax.numpy as jnp
import numpy as np
from jax import lax
from jax.experimental import pallas as pl
from jax.experimental.pallas import tpu as pltpu

F32 = jnp.float32
BF16 = jnp.bfloat16

D_MODEL = 1024
DIFF_HEADS = 4
DIFF_HEAD_DIM = 64
DIFF_V_DIM = 128
DIFF_WIDTH = DIFF_HEADS * DIFF_V_DIM
GLA_HEADS = 4
GLA_K_DIM = 64
GLA_V_DIM = 128
GLA_WIDTH = GLA_HEADS * GLA_V_DIM
GLA_KW = GLA_HEADS * GLA_K_DIM
GLA_GATE_RANK = 16
GLA_GATE_TEMP = 16.0
D_FF = 2816
EPS = 1e-6
COL_DQ = 0
COL_DK = COL_DQ + DIFF_HEADS * 2 * DIFF_HEAD_DIM
COL_DV = COL_DK + DIFF_HEADS * 2 * DIFF_HEAD_DIM
COL_GQ = COL_DV + DIFF_WIDTH
COL_GK = COL_GQ + GLA_KW
COL_GV = COL_GK + GLA_KW
COL_GO = COL_GV + GLA_WIDTH
MAIN_WIDTH = COL_GO + GLA_WIDTH
IN_WIDTH = MAIN_WIDTH + GLA_GATE_RANK

LANES = 128
ROW_TILE = 512
Q_TILE = 512
VT_ROWS = 144
K_TILE = 512
LOOP_UNROLL = 4
HEADS_PER_STEP = 2
GLA_TILE = 512
GLA_CHUNK = 256
MXU_DIM = 256
WEIGHT_CAST_COLS = 512
FFN_WARM = 8
FF_CHUNKS = ((0, D_FF),)
ALIBI_SPLIT = 16
NEG_BIG = -1e30
LOG2E = math.log2(math.e)
EXP_CLAMP = 80.0
VMEM_LIMIT = 56 * 1024 * 1024


def _rms(x, gain):
    return x * lax.rsqrt(jnp.mean(x * x, axis=-1, keepdims=True) + EPS) * gain


def _proj_kernel(x_ref, ga_ref, w_ref, gq_ref, gk_ref, cq_ref, ck_ref,
                 wup_ref, bg_ref, tril_ref, ind_ref, ggain_ref,
                 qt_ref, ka_ref, vt_ref, og_ref,
                 wb_sc, gq_out, gk_out, gv_out, go_out, la_out,
                 st_ref, b_sc, oi_sc, acc_sc, kf_sc, vf_sc):
    @pl.when((pl.program_id(0) == 0) & (pl.program_id(1) == 0))
    def _():
        for c0 in range(0, IN_WIDTH, WEIGHT_CAST_COLS):
            c1 = min(c0 + WEIGHT_CAST_COLS, IN_WIDTH)
            wb_sc[:, c0:c1] = w_ref[c0:c1, :].T.astype(BF16)

    nb = _rms(x_ref[...], ga_ref[...]).astype(BF16)

    def proj(c0, c1):
        return jnp.dot(nb, wb_sc[:, c0:c1], preferred_element_type=F32)

    def head_norm_t(x_t, gain_col):
        ms = jnp.mean(x_t * x_t, axis=0, keepdims=True)
        return x_t * lax.rsqrt(ms + EPS) * gain_col

    d = DIFF_HEAD_DIM
    q_scale = d ** -0.5 * LOG2E
    ones_rows = (lax.broadcasted_iota(jnp.int32, (VT_ROWS - DIFF_V_DIM, ROW_TILE), 0) == 0)

    def emit_qkv(dq, dk, dv):
        for h in range(DIFF_HEADS):
            q_t = dq[:, h * LANES:(h + 1) * LANES].T
            k_t = dk[:, h * LANES:(h + 1) * LANES].T
            for j in range(2):
                yq = head_norm_t(q_t[j * d:(j + 1) * d], gq_ref[...]) * q_scale
                yk = head_norm_t(k_t[j * d:(j + 1) * d], gk_ref[...])
                qt_ref[0, h, j, 0, :d, :] = yq.astype(BF16)
                qt_ref[0, h, j, 0, d:, :] = cq_ref[h]
                ka_ref[0, h, j] = jnp.concatenate([yk, ck_ref[h]], axis=0).T.astype(BF16)
            vt_ref[0, h, 0, :DIFF_V_DIM, :] = dv[:, h * LANES:(h + 1) * LANES].T.astype(BF16)
            vt_ref[0, h, 0, DIFF_V_DIM:, :] = ones_rows.astype(BF16)

    gla_cols = proj(COL_GQ, IN_WIDTH)
    gq_out[...] = (gla_cols[:, :GLA_KW] * (GLA_K_DIM ** -0.5)).astype(BF16)
    gk_out[...] = gla_cols[:, COL_GK - COL_GQ:COL_GV - COL_GQ].astype(BF16)
    gv_out[...] = gla_cols[:, COL_GV - COL_GQ:COL_GO - COL_GQ].astype(BF16)
    go_out[...] = gla_cols[:, COL_GO - COL_GQ:MAIN_WIDTH - COL_GQ].astype(BF16)
    glr = gla_cols[:, MAIN_WIDTH - COL_GQ:]
    dq = proj(COL_DQ, COL_DK)
    z = jnp.dot(glr.astype(BF16), wup_ref[...], preferred_element_type=F32) + bg_ref[...]
    log_sig = jnp.minimum(z, 0.0) - jnp.log(1.0 + jnp.exp(-jnp.abs(z)))
    la_out[...] = log_sig * (1.0 / GLA_GATE_TEMP)

    kv = []

    def project_kv():
        kv.extend([proj(COL_DK, COL_DV), proj(COL_DV, COL_GQ)])

    redo, overflow = _gla_tile(tril_ref, ind_ref, ggain_ref, gq_out, gk_out, gv_out, go_out, la_out,
                               og_ref, st_ref, b_sc, oi_sc, acc_sc, kf_sc, vf_sc,
                               after_cumsum=project_kv)
    emit_qkv(dq, *kv)
    for fn, flag in zip(redo, overflow):
        pl.when(flag)(fn)


def _const_spec(shape):
    nd = len(shape)
    return pl.BlockSpec(shape, lambda *_: (0,) * nd, pipeline_mode=pl.Buffered(1))


def _proj_call(xf, ga, w_main, gq, gk, cq, ck, wup, bg, tril, ind, ggain, batch, seq):
    assert ROW_TILE == GLA_TILE
    n_t = seq // ROW_TILE
    n = batch * seq
    row = lambda b, t: (b * n_t + t, 0)
    out_shape = (
        jax.ShapeDtypeStruct((batch, DIFF_HEADS, 2, seq // Q_TILE, LANES, Q_TILE), BF16),
        jax.ShapeDtypeStruct((batch, DIFF_HEADS, 2, seq, LANES), BF16),
        jax.ShapeDtypeStruct((batch, DIFF_HEADS, seq // K_TILE, VT_ROWS, K_TILE), BF16),
        jax.ShapeDtypeStruct((n, GLA_WIDTH), BF16),
    )
    out_specs = (
        pl.BlockSpec((1, DIFF_HEADS, 2, 1, LANES, Q_TILE), lambda b, t: (b, 0, 0, t, 0, 0)),
        pl.BlockSpec((1, DIFF_HEADS, 2, ROW_TILE, LANES), lambda b, t: (b, 0, 0, t, 0)),
        pl.BlockSpec((1, DIFF_HEADS, 1, VT_ROWS, K_TILE), lambda b, t: (b, 0, t, 0, 0)),
        pl.BlockSpec((ROW_TILE, GLA_WIDTH), row),
    )
    consts = (ga, w_main, gq, gk, cq, ck, wup, bg, tril, ind, ggain)
    in_specs = [pl.BlockSpec((ROW_TILE, D_MODEL), row)] + [_const_spec(a.shape) for a in consts]
    scratch_shapes = [
        pltpu.VMEM((D_MODEL, IN_WIDTH), BF16),
        pltpu.VMEM((ROW_TILE, GLA_KW), BF16),
        pltpu.VMEM((ROW_TILE, GLA_KW), BF16),
        pltpu.VMEM((ROW_TILE, GLA_WIDTH), BF16),
        pltpu.VMEM((ROW_TILE, GLA_WIDTH), BF16),
        pltpu.VMEM((ROW_TILE, GLA_KW), F32),
        pltpu.VMEM((GLA_HEADS, GLA_V_DIM, GLA_K_DIM), F32),
        pltpu.VMEM((GLA_TILE, GLA_KW), F32),
        pltpu.VMEM((GLA_TILE, GLA_WIDTH), F32),
        pltpu.VMEM((GLA_CHUNK, GLA_WIDTH), F32),
        pltpu.VMEM((GLA_CHUNK, GLA_KW), F32),
        pltpu.VMEM((GLA_CHUNK, GLA_WIDTH), F32),
    ]
    return pl.pallas_call(
        _proj_kernel, out_shape=out_shape, grid=(batch, n_t),
        in_specs=in_specs, out_specs=out_specs, name="proj_gla",
        scratch_shapes=scratch_shapes,
        compiler_params=pltpu.CompilerParams(
            dimension_semantics=("arbitrary", "arbitrary"), vmem_limit_bytes=VMEM_LIMIT),
    )(xf, *consts)


def _attn_kernel(slope_ref, lq1_ref, lk1_ref, lq2_ref, lk2_ref, gain_ref, qt_ref, qn_ref, k_ref,
                 vt_ref, o_ref, s_sc, cm_sc, m_sc, acc_sc, *, lambda_init):
    hg = pl.program_id(1)
    qi = pl.program_id(2)
    m_sc[...] = jnp.full(m_sc.shape, NEG_BIG, F32)
    acc_sc[...] = jnp.zeros(acc_sc.shape, F32)
    group = range(HEADS_PER_STEP)

    half = K_TILE // 2

    def score1(g, j, kb, q_src=qt_ref):
        rows = pl.ds(pl.multiple_of(kb * K_TILE, K_TILE), K_TILE)
        s = jnp.dot(k_ref[0, g, j, rows, :], q_src[0, g, j, 0], preferred_element_type=F32)
        s_sc[g, j] = s
        cm_sc[g, j] = jnp.max(s, axis=0, keepdims=True)

    def absorb1(g, j, kb):
        off = slope_ref[hg * HEADS_PER_STEP + g] * (kb - qi).astype(F32)
        m_prev = m_sc[g, j]
        m_new = jnp.maximum(m_prev, cm_sc[g, j] + off)
        p = jnp.exp2(s_sc[g, j] - (m_new - off))
        alpha = jnp.exp2(m_prev - m_new)
        acc_sc[g, j] = alpha * acc_sc[g, j] + jnp.dot(vt_ref[0, g, kb], p.astype(BF16),
                                                      preferred_element_type=F32)
        m_sc[g, j] = m_new

    def score_diag1(g, j):
        r0 = pl.multiple_of(qi * K_TILE, K_TILE)
        q_t = qt_ref[0, g, j, 0]
        s_a = jnp.dot(k_ref[0, g, j, pl.ds(r0, half), :], q_t, preferred_element_type=F32)
        s_b = jnp.dot(k_ref[0, g, j, pl.ds(r0 + half, half), :], q_t[:, half:],
                      preferred_element_type=F32)
        s_a = jnp.where(lax.broadcasted_iota(jnp.int32, s_a.shape, 0)
                        <= lax.broadcasted_iota(jnp.int32, s_a.shape, 1), s_a, NEG_BIG)
        s_b = jnp.where(lax.broadcasted_iota(jnp.int32, s_b.shape, 0)
                        <= lax.broadcasted_iota(jnp.int32, s_b.shape, 1), s_b, NEG_BIG)
        s_sc[g, j, :half, :] = s_a
        s_sc[g, j, half:, half:] = s_b
        cm_sc[g, j] = jnp.max(s_a, axis=0, keepdims=True)
        cm_sc[g, j, :, half:] = jnp.maximum(cm_sc[g, j, :, half:],
                                            jnp.max(s_b, axis=0, keepdims=True))

    def absorb_diag1(g, j):
        m_prev = m_sc[g, j]
        m_new = jnp.maximum(m_prev, cm_sc[g, j])
        m_sc[g, j] = m_new
        p_a = jnp.exp2(s_sc[g, j, :half, :] - m_new)
        p_b = jnp.exp2(s_sc[g, j, half:, half:] - m_sc[g, j, :, half:])
        alpha = jnp.exp2(m_prev - m_new)
        pv_a = jnp.dot(vt_ref[0, g, qi, :, :half], p_a.astype(BF16), preferred_element_type=F32)
        pv_b = jnp.dot(vt_ref[0, g, qi, :, half:], p_b.astype(BF16), preferred_element_type=F32)
        acc_sc[g, j] = alpha * acc_sc[g, j] + pv_a
        acc_sc[g, j, :, half:] += pv_b

    def score(j, kb):
        for g in group:
            score1(g, j, kb)

    def absorb(j, kb):
        for g in group:
            absorb1(g, j, kb)

    def score_diag(j):
        for g in group:
            score_diag1(g, j)

    def absorb_diag(j):
        for g in group:
            absorb_diag1(g, j)

    def unit(kb):
        for g in group:
            score1(g, 1, kb)
            absorb1(g, 0, kb)
        for g in group:
            score1(g, 0, kb + 1)
            absorb1(g, 1, kb)

    def body(kp, carry):
        for u in range(LOOP_UNROLL):
            unit(LOOP_UNROLL * kp + u)
        return carry

    n_full = jnp.maximum(qi - 1, 0)
    lax.fori_loop(0, n_full // LOOP_UNROLL, body, 0)

    done = (n_full // LOOP_UNROLL) * LOOP_UNROLL
    size = LOOP_UNROLL // 2
    while size >= 1:
        take = (n_full - done) >= size

        @pl.when(take)
        def _(done=done, size=size):
            for u in range(size):
                unit(done + u)

        done = done + jnp.where(take, size, 0)
        size //= 2

    def finish():
        score_diag(1)
        absorb_diag(0)
        absorb_diag(1)
        for g in group:
            score1(g, 0, 0, q_src=qn_ref)
        lam = (jnp.exp(jnp.sum(lq1_ref[...] * lk1_ref[...], axis=1, keepdims=True))
               - jnp.exp(jnp.sum(lq2_ref[...] * lk2_ref[...], axis=1, keepdims=True))
               + lambda_init)
        for g in group:
            a1 = acc_sc[g, 0]
            a2 = acc_sc[g, 1]
            o_t = (a1[:DIFF_V_DIM] / a1[DIFF_V_DIM:DIFF_V_DIM + 1]
                   - lam * (a2[:DIFF_V_DIM] / a2[DIFF_V_DIM:DIFF_V_DIM + 1]))
            o_ref[:, g * DIFF_V_DIM:(g + 1) * DIFF_V_DIM] = (
                _rms(o_t.T, gain_ref[...]) * (1.0 - lambda_init)).astype(BF16)

    @pl.when(qi > 0)
    def _():
        for g in group:
            score1(g, 1, qi - 1)
            absorb1(g, 0, qi - 1)
        for g in group:
            score_diag1(g, 0)
            absorb1(g, 1, qi - 1)
        finish()

    @pl.when(qi == 0)
    def _():
        score_diag(0)
        finish()


def _attn_call(slope_k, lq1, lk1, lq2, lk2, gain, qt, ka, vt, batch, seq, lambda_init):
    n_q = seq // Q_TILE
    n_k = seq // K_TILE
    hps = HEADS_PER_STEP
    smem = pl.BlockSpec(memory_space=pltpu.SMEM)
    in_specs = [
        smem,
        _const_spec(lq1.shape), _const_spec(lk1.shape), _const_spec(lq2.shape),
        _const_spec(lk2.shape), _const_spec(gain.shape),
        pl.BlockSpec((1, hps, 2, 1, LANES, Q_TILE), lambda b, h, q: (b, h, 0, q, 0, 0)),
        pl.BlockSpec((1, hps, 2, 1, LANES, Q_TILE),
                     lambda b, h, q: (b, h, 0, jnp.minimum(q + 1, n_q - 1), 0, 0)),
        pl.BlockSpec((1, hps, 2, seq, LANES), lambda b, h, q: (b, h, 0, 0, 0)),
        pl.BlockSpec((1, hps, n_k, VT_ROWS, K_TILE), lambda b, h, q: (b, h, 0, 0, 0)),
    ]
    return pl.pallas_call(
        functools.partial(_attn_kernel, lambda_init=lambda_init),
        out_shape=jax.ShapeDtypeStruct((batch * seq, DIFF_WIDTH), BF16),
        grid=(batch, DIFF_HEADS // hps, n_q),
        in_specs=in_specs,
        out_specs=pl.BlockSpec((Q_TILE, hps * DIFF_V_DIM), lambda b, h, q: (b * n_q + q, h)),
        scratch_shapes=[pltpu.VMEM((hps, 2, K_TILE, Q_TILE), F32),
                        pltpu.VMEM((hps, 2, 1, Q_TILE), F32),
                        pltpu.VMEM((hps, 2, 1, Q_TILE), F32),
                        pltpu.VMEM((hps, 2, VT_ROWS, Q_TILE), F32)],
        name="diff_attn",
        compiler_params=pltpu.CompilerParams(
            dimension_semantics=("arbitrary", "arbitrary", "arbitrary"),
            vmem_limit_bytes=VMEM_LIMIT),
    )(slope_k, lq1, lk1, lq2, lk2, gain, qt, qt, ka, vt)


def _gla_tile(tril_ref, ind_ref, gain_ref, q_ref, k_ref, v_ref, g_ref, la_ref, o_ref,
              st_ref, b_sc, oi_sc, acc_sc, kf_sc, vf_sc, after_cumsum=None):
    @pl.when(pl.program_id(1) == 0)
    def _():
        st_ref[...] = jnp.zeros(st_ref.shape, F32)

    c_len = GLA_CHUNK
    row = lax.broadcasted_iota(jnp.int32, (c_len, c_len), 0)
    col = lax.broadcasted_iota(jnp.int32, (c_len, c_len), 1)
    causal = col <= row
    nt = (((1,), (1,)), ((), ()))

    def finalize(o, rows, vs):
        g = g_ref[rows, vs].astype(F32)
        return (_rms(o, gain_ref[...]) * (g * jax.nn.sigmoid(g))).astype(BF16)

    n_chunks = GLA_TILE // c_len
    cums = []
    for c in range(n_chunks):
        la = la_ref[c * c_len:(c + 1) * c_len, :]
        hi = la.astype(BF16)
        r1 = la - hi.astype(F32)
        mid = r1.astype(BF16)
        lo = (r1 - mid.astype(F32)).astype(BF16)
        tril = tril_ref[...]
        cums.append(jnp.dot(tril, hi, preferred_element_type=F32)
                    + jnp.dot(tril, mid, preferred_element_type=F32)
                    + jnp.dot(tril, lo, preferred_element_type=F32))

    if after_cumsum is not None:
        after_cumsum()

    overflow = []
    for c in range(n_chunks):
        rows = slice(c * c_len, (c + 1) * c_len)
        b = cums[c]
        b_sc[rows, :] = b
        b_last = b[c_len - 1:c_len, :]
        overflow.append(jnp.max(-b_last) > EXP_CLAMP)
        q = q_ref[rows, :].astype(F32)
        k = k_ref[rows, :].astype(F32)
        q_in = (q * jnp.exp(b)).astype(BF16)
        k_out = (k * jnp.exp(jnp.minimum(-b, EXP_CLAMP))).astype(BF16)
        k_dec = (k * jnp.exp(b_last - b)).astype(BF16)
        d_last = jnp.exp(b_last)
        for h in range(GLA_HEADS):
            ks = slice(h * GLA_K_DIM, (h + 1) * GLA_K_DIM)
            vs = slice(h * GLA_V_DIM, (h + 1) * GLA_V_DIM)
            a = lax.dot_general(q_in[:, ks], k_out[:, ks], nt, preferred_element_type=F32)
            a = jnp.where(causal, a, 0.0).astype(BF16)
            vh = v_ref[rows, vs]
            s_t = st_ref[h]
            o_inter = lax.dot_general(q_in[:, ks], s_t.astype(BF16), nt,
                                      preferred_element_type=F32)
            oi_sc[rows, vs] = o_inter
            o = o_inter + jnp.dot(a, vh, preferred_element_type=F32)
            v_t = vh.astype(F32).T.astype(BF16)
            st_ref[h] = s_t * d_last[:, ks] + jnp.dot(v_t, k_dec[:, ks], preferred_element_type=F32)
            o_ref[rows, vs] = finalize(o, rows, vs)

    def exact_intra(c):
        rows = slice(c * c_len, (c + 1) * c_len)
        q = q_ref[rows, :].astype(F32)
        b = b_sc[rows, :]
        kf_sc[...] = k_ref[rows, :].astype(F32)
        vf_sc[...] = v_ref[rows, :].astype(F32)
        acc_sc[...] = jnp.zeros(acc_sc.shape, F32)
        t_idx = lax.broadcasted_iota(jnp.int32, (c_len, GLA_KW), 0)

        def body(s, carry):
            k_s = kf_sc[pl.ds(s, 1), :]
            b_s = b_sc[pl.ds(c * c_len + s, 1), :]
            v_s = vf_sc[pl.ds(s, 1), :]
            w = q * k_s * jnp.exp(jnp.minimum(b - b_s, 0.0))
            w = jnp.where(t_idx >= s, w, 0.0).astype(BF16)
            a_s = jnp.dot(w, ind_ref[...], preferred_element_type=F32)
            acc_sc[...] += a_s * v_s
            return carry

        lax.fori_loop(0, c_len, body, 0)
        for h in range(GLA_HEADS):
            vs = slice(h * GLA_V_DIM, (h + 1) * GLA_V_DIM)
            o_ref[rows, vs] = finalize(oi_sc[rows, vs] + acc_sc[:, vs], rows, vs)

    return [functools.partial(exact_intra, c) for c in range(n_chunks)], overflow


def _ffn_kernel(x_ref, od_ref, og_ref, gf_ref, wo_ref, wg_ref, wu_ref, wd_ref, o_ref,
                wo_sc, wg_sc, wu_sc, wd_sc):
    i = pl.program_id(0)

    @pl.when(i < FFN_WARM)
    def _():
        for src, dst in ((wo_ref, wo_sc), (wg_ref, wg_sc), (wu_ref, wu_sc), (wd_ref, wd_sc)):
            rows = src.shape[0]
            dst[pl.ds(pl.multiple_of(i * rows, rows), rows), :] = src[...].astype(BF16)

    @pl.when(i >= FFN_WARM)
    def _():
        h = (x_ref[...]
             + jnp.dot(od_ref[...], wo_sc[:DIFF_WIDTH, :], preferred_element_type=F32)
             + jnp.dot(og_ref[...], wo_sc[DIFF_WIDTH:, :], preferred_element_type=F32))
        m = _rms(h, gf_ref[...]).astype(BF16)
        o_ref[...] = h
        for c0, c1 in FF_CHUNKS:
            gate = jnp.dot(m, wg_sc[:, c0:c1], preferred_element_type=F32)
            up = jnp.dot(m, wu_sc[:, c0:c1], preferred_element_type=F32)
            act = (gate * jax.nn.sigmoid(gate) * up).astype(BF16)
            o_ref[...] += jnp.dot(act, wd_sc[c0:c1, :], preferred_element_type=F32)


def _ffn_call(xf, od, og, gf, wo, wg, wu, wd):
    n = xf.shape[0]
    row = lambda i: (jnp.maximum(i - FFN_WARM, 0), 0)
    chunk = lambda i: (jnp.minimum(i, FFN_WARM - 1), 0)

    def chunk_spec(w):
        assert w.shape[0] % (FFN_WARM * 16) == 0
        return pl.BlockSpec((w.shape[0] // FFN_WARM, w.shape[1]), chunk)

    in_specs = [
        pl.BlockSpec((ROW_TILE, D_MODEL), row),
        pl.BlockSpec((ROW_TILE, DIFF_WIDTH), row),
        pl.BlockSpec((ROW_TILE, GLA_WIDTH), row),
        _const_spec(gf.shape),
        chunk_spec(wo), chunk_spec(wg), chunk_spec(wu), chunk_spec(wd),
    ]
    return pl.pallas_call(
        _ffn_kernel,
        out_shape=jax.ShapeDtypeStruct((n, D_MODEL), F32),
        grid=(FFN_WARM + n // ROW_TILE,),
        in_specs=in_specs,
        out_specs=pl.BlockSpec((ROW_TILE, D_MODEL), row),
        scratch_shapes=[pltpu.VMEM(w.shape, BF16) for w in (wo, wg, wu, wd)],
        name="out_ffn",
        compiler_params=pltpu.CompilerParams(
            dimension_semantics=("arbitrary",), vmem_limit_bytes=VMEM_LIMIT),
    )(xf, od, og, gf, wo, wg, wu, wd)


def _bf16_pieces(value, n=3):
    pieces = []
    rest = np.float32(value)
    for _ in range(n):
        piece = np.float32(np.asarray(rest, dtype=BF16))
        pieces.append(float(piece))
        rest = np.float32(rest - piece)
    return pieces


def _alibi_tables():
    assert ROW_TILE == Q_TILE == K_TILE
    pos = np.arange(ROW_TILE)
    a = (pos // ALIBI_SPLIT) * ALIBI_SPLIT
    c = pos % ALIBI_SPLIT
    extra = LANES - DIFF_HEAD_DIM
    cq = np.zeros((DIFF_HEADS, extra, ROW_TILE), np.float32)
    ck = np.zeros((DIFF_HEADS, extra, ROW_TILE), np.float32)
    for h in range(DIFF_HEADS):
        r = 0
        for piece in _bf16_pieces(_slope(h) * LOG2E):
            for q_row, k_row in ((piece, a), (piece, c), (-a, piece), (-c, piece)):
                cq[h, r, :] = q_row
                ck[h, r, :] = k_row
                r += 1
        assert r <= extra
    return jnp.asarray(cq, dtype=BF16), jnp.asarray(ck)


def _slope(h):
    return 2.0 ** (-8.0 * (h + 1) / DIFF_HEADS)


def kernel(x, attn_norm_gain, w_in, q_norm_gain, k_norm_gain, lambda_q1, lambda_k1, lambda_q2,
           lambda_k2, diff_out_norm_gain, w_gla_gate_up, b_gla_gate, gla_out_norm_gain, w_out,
           ffn_norm_gain, w_ffn_gate, w_ffn_up, w_ffn_down):
    batch, seq, _ = x.shape
    depth = w_in.shape[0]
    assert seq % ROW_TILE == 0 and seq % GLA_TILE == 0

    cq, ck = _alibi_tables()
    tril = jnp.asarray(np.tril(np.ones((GLA_CHUNK, GLA_CHUNK), np.float32)), dtype=BF16)
    ind = jnp.asarray((np.arange(GLA_KW)[:, None] // GLA_K_DIM
                       == np.arange(GLA_WIDTH)[None, :] // GLA_V_DIM).astype(np.float32), dtype=BF16)
    slope_k = jnp.asarray(
        [sum(_bf16_pieces(_slope(h) * LOG2E)) * K_TILE for h in range(DIFF_HEADS)], F32)

    hcur = x.reshape(batch * seq, D_MODEL)
    for l in range(depth):
        lambda_init = 0.8 - 0.6 * math.exp(-0.3 * l)
        w_main = w_in[l].T
        wup = w_gla_gate_up[l].astype(BF16)
        row2 = lambda a: a.reshape(1, -1).astype(F32)
        col2 = lambda a: a.reshape(-1, 1).astype(F32)

        qt, ka, vt, o_gla = _proj_call(
            hcur, row2(attn_norm_gain[l]), w_main, col2(q_norm_gain[l]), col2(k_norm_gain[l]),
            cq, ck, wup, row2(b_gla_gate[l]), tril, ind, row2(gla_out_norm_gain[l]), batch, seq)

        o_diff = _attn_call(slope_k, row2(lambda_q1[l]), row2(lambda_k1[l]), row2(lambda_q2[l]),
                            row2(lambda_k2[l]), row2(diff_out_norm_gain[l]), qt, ka, vt,
                            batch, seq, lambda_init)

        hcur = _ffn_call(hcur, o_diff, o_gla, row2(ffn_norm_gain[l]), w_out[l], w_ffn_gate[l],
                         w_ffn_up[l], w_ffn_down[l])
    return hcur.reshape(batch, seq, D_MODEL)
```

```python
import functools
import math

import jax
import jax.numpy as jnp
import numpy as np
from jax import lax
from jax.experimental import pallas as pl
from jax.experimental.pallas import tpu as pltpu

F32 = jnp.float32
BF16 = jnp.bfloat16

D_MODEL = 1024
DIFF_HEADS = 4
DIFF_HEAD_DIM = 64
DIFF_V_DIM = 128
DIFF_WIDTH = DIFF_HEADS * DIFF_V_DIM
GLA_HEADS = 4
GLA_K_DIM = 64
GLA_V_DIM = 128
GLA_WIDTH = GLA_HEADS * GLA_V_DIM
GLA_KW = GLA_HEADS * GLA_K_DIM
GLA_GATE_RANK = 16
GLA_GATE_TEMP = 16.0
D_FF = 2816
EPS = 1e-6
COL_DQ = 0
COL_DK = COL_DQ + DIFF_HEADS * 2 * DIFF_HEAD_DIM
COL_DV = COL_DK + DIFF_HEADS * 2 * DIFF_HEAD_DIM
COL_GQ = COL_DV + DIFF_WIDTH
COL_GK = COL_GQ + GLA_KW
COL_GV = COL_GK + GLA_KW
COL_GO = COL_GV + GLA_WIDTH
MAIN_WIDTH = COL_GO + GLA_WIDTH
IN_WIDTH = MAIN_WIDTH + GLA_GATE_RANK

LANES = 128
ROW_TILE = 512
Q_TILE = 512
VT_ROWS = 144
K_TILE = 512
LOOP_UNROLL = 4
HEADS_PER_STEP = 2
GLA_TILE = 512
GLA_CHUNK = 256
MXU_DIM = 256
WEIGHT_CAST_COLS = 512
FFN_WARM = 8
FF_CHUNKS = ((0, D_FF),)
ALIBI_SPLIT = 16
NEG_BIG = -1e30
LOG2E = math.log2(math.e)
EXP_CLAMP = 80.0
VMEM_LIMIT = 56 * 1024 * 1024


def _rms(x, gain):
    return x * lax.rsqrt(jnp.mean(x * x, axis=-1, keepdims=True) + EPS) * gain


def _proj_kernel(x_ref, ga_ref, w_ref, gq_ref, gk_ref, cq_ref, ck_ref,
                 wup_ref, bg_ref, tril_ref, ind_ref, ggain_ref,
                 qt_ref, ka_ref, vt_ref, og_ref,
                 wb_sc, gq_out, gk_out, gv_out, go_out, la_out,
                 st_ref, b_sc, oi_sc, acc_sc, kf_sc, vf_sc):
    @pl.when((pl.program_id(0) == 0) & (pl.program_id(1) == 0))
    def _():
        for c0 in range(0, IN_WIDTH, WEIGHT_CAST_COLS):
            c1 = min(c0 + WEIGHT_CAST_COLS, IN_WIDTH)
            wb_sc[:, c0:c1] = w_ref[c0:c1, :].T.astype(BF16)

    def head_norm_t(x_t, gain_col):
        ms = jnp.mean(x_t * x_t, axis=0, keepdims=True)
        return x_t * lax.rsqrt(ms + EPS) * gain_col

    d = DIFF_HEAD_DIM
    q_scale = d ** -0.5 * LOG2E
    sub = GLA_CHUNK
    ones_rows = (lax.broadcasted_iota(jnp.int32, (VT_ROWS - DIFF_V_DIM, sub), 0) == 0)

    def one_pass(r):
        tok = slice(r * sub, (r + 1) * sub)
        nb = _rms(x_ref[tok, :], ga_ref[...]).astype(BF16)

        def proj(c0, c1):
            return jnp.dot(nb, wb_sc[:, c0:c1], preferred_element_type=F32)

        def emit_qkv(dq, dk, dv):
            for h in range(DIFF_HEADS):
                q_t = dq[:, h * LANES:(h + 1) * LANES].T
                k_t = dk[:, h * LANES:(h + 1) * LANES].T
                for j in range(2):
                    yq = head_norm_t(q_t[j * d:(j + 1) * d], gq_ref[...]) * q_scale
                    yk = head_norm_t(k_t[j * d:(j + 1) * d], gk_ref[...])
                    qt_ref[0, h, j, 0, :d, tok] = yq.astype(BF16)
                    qt_ref[0, h, j, 0, d:, tok] = cq_ref[h, :, tok]
                    ka_ref[0, h, j, tok, :] = jnp.concatenate(
                        [yk, ck_ref[h, :, tok]], axis=0).T.astype(BF16)
                vt_ref[0, h, 0, :DIFF_V_DIM, tok] = dv[:, h * LANES:(h + 1) * LANES].T.astype(BF16)
                vt_ref[0, h, 0, DIFF_V_DIM:, tok] = ones_rows.astype(BF16)

        gla_cols = proj(COL_GQ, IN_WIDTH)
        gq_out[tok, :] = (gla_cols[:, :GLA_KW] * (GLA_K_DIM ** -0.5)).astype(BF16)
        gk_out[tok, :] = gla_cols[:, COL_GK - COL_GQ:COL_GV - COL_GQ].astype(BF16)
        gv_out[tok, :] = gla_cols[:, COL_GV - COL_GQ:COL_GO - COL_GQ].astype(BF16)
        go_out[tok, :] = gla_cols[:, COL_GO - COL_GQ:MAIN_WIDTH - COL_GQ].astype(BF16)
        glr = gla_cols[:, MAIN_WIDTH - COL_GQ:]
        dq = proj(COL_DQ, COL_DK)
        z = jnp.dot(glr.astype(BF16), wup_ref[...], preferred_element_type=F32) + bg_ref[...]
        log_sig = jnp.minimum(z, 0.0) - jnp.log(1.0 + jnp.exp(-jnp.abs(z)))
        la_out[tok, :] = log_sig * (1.0 / GLA_GATE_TEMP)

        kv = []

        def project_kv():
            kv.extend([proj(COL_DK, COL_DV), proj(COL_DV, COL_GQ)])

        redo, overflow = _gla_tile(tril_ref, ind_ref, ggain_ref, gq_out, gk_out, gv_out, go_out,
                                   la_out, og_ref, st_ref, b_sc, oi_sc, acc_sc, kf_sc, vf_sc,
                                   after_cumsum=project_kv, chunk_ids=(r,))
        emit_qkv(dq, *kv)
        return redo, overflow

    pending = [one_pass(r) for r in range(ROW_TILE // sub)]
    for redo, overflow in pending:
        for fn, flag in zip(redo, overflow):
            pl.when(flag)(fn)


def _const_spec(shape):
    nd = len(shape)
    return pl.BlockSpec(shape, lambda *_: (0,) * nd, pipeline_mode=pl.Buffered(1))


def _proj_call(xf, ga, w_main, gq, gk, cq, ck, wup, bg, tril, ind, ggain, batch, seq):
    assert ROW_TILE == GLA_TILE
    n_t = seq // ROW_TILE
    n = batch * seq
    row = lambda b, t: (b * n_t + t, 0)
    out_shape = (
        jax.ShapeDtypeStruct((batch, DIFF_HEADS, 2, seq // Q_TILE, LANES, Q_TILE), BF16),
        jax.ShapeDtypeStruct((batch, DIFF_HEADS, 2, seq, LANES), BF16),
        jax.ShapeDtypeStruct((batch, DIFF_HEADS, seq // K_TILE, VT_ROWS, K_TILE), BF16),
        jax.ShapeDtypeStruct((n, GLA_WIDTH), BF16),
    )
    out_specs = (
        pl.BlockSpec((1, DIFF_HEADS, 2, 1, LANES, Q_TILE), lambda b, t: (b, 0, 0, t, 0, 0)),
        pl.BlockSpec((1, DIFF_HEADS, 2, ROW_TILE, LANES), lambda b, t: (b, 0, 0, t, 0)),
        pl.BlockSpec((1, DIFF_HEADS, 1, VT_ROWS, K_TILE), lambda b, t: (b, 0, t, 0, 0)),
        pl.BlockSpec((ROW_TILE, GLA_WIDTH), row),
    )
    consts = (ga, w_main, gq, gk, cq, ck, wup, bg, tril, ind, ggain)
    in_specs = [pl.BlockSpec((ROW_TILE, D_MODEL), row)] + [_const_spec(a.shape) for a in consts]
    scratch_shapes = [
        pltpu.VMEM((D_MODEL, IN_WIDTH), BF16),
        pltpu.VMEM((ROW_TILE, GLA_KW), BF16),
        pltpu.VMEM((ROW_TILE, GLA_KW), BF16),
        pltpu.VMEM((ROW_TILE, GLA_WIDTH), BF16),
        pltpu.VMEM((ROW_TILE, GLA_WIDTH), BF16),
        pltpu.VMEM((ROW_TILE, GLA_KW), F32),
        pltpu.VMEM((GLA_HEADS, GLA_V_DIM, GLA_K_DIM), F32),
        pltpu.VMEM((GLA_TILE, GLA_KW), F32),
        pltpu.VMEM((GLA_TILE, GLA_WIDTH), F32),
        pltpu.VMEM((GLA_CHUNK, GLA_WIDTH), F32),
        pltpu.VMEM((GLA_CHUNK, GLA_KW), F32),
        pltpu.VMEM((GLA_CHUNK, GLA_WIDTH), F32),
    ]
    return pl.pallas_call(
        _proj_kernel, out_shape=out_shape, grid=(batch, n_t),
        in_specs=in_specs, out_specs=out_specs, name="proj_gla",
        scratch_shapes=scratch_shapes,
        compiler_params=pltpu.CompilerParams(
            dimension_semantics=("arbitrary", "arbitrary"), vmem_limit_bytes=VMEM_LIMIT),
    )(xf, *consts)


def _attn_kernel(slope_ref, lq1_ref, lk1_ref, lq2_ref, lk2_ref, gain_ref, qt_ref, qn_ref, k_ref,
                 vt_ref, o_ref, s_sc, cm_sc, m_sc, acc_sc, *, lambda_init):
    hg = pl.program_id(1)
    qi = pl.program_id(2)
    m_sc[...] = jnp.full(m_sc.shape, NEG_BIG, F32)
    acc_sc[...] = jnp.zeros(acc_sc.shape, F32)
    group = range(HEADS_PER_STEP)

    half = K_TILE // 2

    def score1(g, j, kb, q_src=qt_ref):
        rows = pl.ds(pl.multiple_of(kb * K_TILE, K_TILE), K_TILE)
        s = jnp.dot(k_ref[0, g, j, rows, :], q_src[0, g, j, 0], preferred_element_type=F32)
        s_sc[g, j] = s
        cm_sc[g, j] = jnp.max(s, axis=0, keepdims=True)

    def absorb1(g, j, kb):
        off = slope_ref[hg * HEADS_PER_STEP + g] * (kb - qi).astype(F32)
        m_prev = m_sc[g, j]
        m_new = jnp.maximum(m_prev, cm_sc[g, j] + off)
        p = jnp.exp2(s_sc[g, j] - (m_new - off))
        alpha = jnp.exp2(m_prev - m_new)
        acc_sc[g, j] = alpha * acc_sc[g, j] + jnp.dot(vt_ref[0, g, kb], p.astype(BF16),
                                                      preferred_element_type=F32)
        m_sc[g, j] = m_new

    def score_diag1(g, j):
        r0 = pl.multiple_of(qi * K_TILE, K_TILE)
        q_t = qt_ref[0, g, j, 0]
        s_a = jnp.dot(k_ref[0, g, j, pl.ds(r0, half), :], q_t, preferred_element_type=F32)
        s_b = jnp.dot(k_ref[0, g, j, pl.ds(r0 + half, half), :], q_t[:, half:],
                      preferred_element_type=F32)
        s_a = jnp.where(lax.broadcasted_iota(jnp.int32, s_a.shape, 0)
                        <= lax.broadcasted_iota(jnp.int32, s_a.shape, 1), s_a, NEG_BIG)
        s_b = jnp.where(lax.broadcasted_iota(jnp.int32, s_b.shape, 0)
                        <= lax.broadcasted_iota(jnp.int32, s_b.shape, 1), s_b, NEG_BIG)
        s_sc[g, j, :half, :] = s_a
        s_sc[g, j, half:, half:] = s_b
        cm_sc[g, j] = jnp.max(s_a, axis=0, keepdims=True)
        cm_sc[g, j, :, half:] = jnp.maximum(cm_sc[g, j, :, half:],
                                            jnp.max(s_b, axis=0, keepdims=True))

    def absorb_diag1(g, j):
        m_prev = m_sc[g, j]
        m_new = jnp.maximum(m_prev, cm_sc[g, j])
        m_sc[g, j] = m_new
        p_a = jnp.exp2(s_sc[g, j, :half, :] - m_new)
        p_b = jnp.exp2(s_sc[g, j, half:, half:] - m_sc[g, j, :, half:])
        alpha = jnp.exp2(m_prev - m_new)
        pv_a = jnp.dot(vt_ref[0, g, qi, :, :half], p_a.astype(BF16), preferred_element_type=F32)
        pv_b = jnp.dot(vt_ref[0, g, qi, :, half:], p_b.astype(BF16), preferred_element_type=F32)
        acc_sc[g, j] = alpha * acc_sc[g, j] + pv_a
        acc_sc[g, j, :, half:] += pv_b

    def score(j, kb):
        for g in group:
            score1(g, j, kb)

    def absorb(j, kb):
        for g in group:
            absorb1(g, j, kb)

    def score_diag(j):
        for g in group:
            score_diag1(g, j)

    def absorb_diag(j):
        for g in group:
            absorb_diag1(g, j)

    def unit(kb):
        for g in group:
            score1(g, 1, kb)
            absorb1(g, 0, kb)
        for g in group:
            score1(g, 0, kb + 1)
            absorb1(g, 1, kb)

    def body(kp, carry):
        for u in range(LOOP_UNROLL):
            unit(LOOP_UNROLL * kp + u)
        return carry

    n_full = jnp.maximum(qi - 1, 0)
    lax.fori_loop(0, n_full // LOOP_UNROLL, body, 0)

    done = (n_full // LOOP_UNROLL) * LOOP_UNROLL
    size = LOOP_UNROLL // 2
    while size >= 1:
        take = (n_full - done) >= size

        @pl.when(take)
        def _(done=done, size=size):
            for u in range(size):
                unit(done + u)

        done = done + jnp.where(take, size, 0)
        size //= 2

    def finish():
        score_diag(1)
        absorb_diag(0)
        absorb_diag(1)
        for g in group:
            score1(g, 0, 0, q_src=qn_ref)
        lam = (jnp.exp(jnp.sum(lq1_ref[...] * lk1_ref[...], axis=1, keepdims=True))
               - jnp.exp(jnp.sum(lq2_ref[...] * lk2_ref[...], axis=1, keepdims=True))
               + lambda_init)
        for g in group:
            a1 = acc_sc[g, 0]
            a2 = acc_sc[g, 1]
            o_t = (a1[:DIFF_V_DIM] / a1[DIFF_V_DIM:DIFF_V_DIM + 1]
                   - lam * (a2[:DIFF_V_DIM] / a2[DIFF_V_DIM:DIFF_V_DIM + 1]))
            o_ref[:, g * DIFF_V_DIM:(g + 1) * DIFF_V_DIM] = (
                _rms(o_t.T, gain_ref[...]) * (1.0 - lambda_init)).astype(BF16)

    @pl.when(qi > 0)
    def _():
        for g in group:
            score1(g, 1, qi - 1)
            absorb1(g, 0, qi - 1)
        for g in group:
            score_diag1(g, 0)
            absorb1(g, 1, qi - 1)
        finish()

    @pl.when(qi == 0)
    def _():
        score_diag(0)
        finish()


def _attn_call(slope_k, lq1, lk1, lq2, lk2, gain, qt, ka, vt, batch, seq, lambda_init):
    n_q = seq // Q_TILE
    n_k = seq // K_TILE
    hps = HEADS_PER_STEP
    smem = pl.BlockSpec(memory_space=pltpu.SMEM)
    in_specs = [
        smem,
        _const_spec(lq1.shape), _const_spec(lk1.shape), _const_spec(lq2.shape),
        _const_spec(lk2.shape), _const_spec(gain.shape),
        pl.BlockSpec((1, hps, 2, 1, LANES, Q_TILE), lambda b, h, q: (b, h, 0, q, 0, 0)),
        pl.BlockSpec((1, hps, 2, 1, LANES, Q_TILE),
                     lambda b, h, q: (b, h, 0, jnp.minimum(q + 1, n_q - 1), 0, 0)),
        pl.BlockSpec((1, hps, 2, seq, LANES), lambda b, h, q: (b, h, 0, 0, 0)),
        pl.BlockSpec((1, hps, n_k, VT_ROWS, K_TILE), lambda b, h, q: (b, h, 0, 0, 0)),
    ]
    return pl.pallas_call(
        functools.partial(_attn_kernel, lambda_init=lambda_init),
        out_shape=jax.ShapeDtypeStruct((batch * seq, DIFF_WIDTH), BF16),
        grid=(batch, DIFF_HEADS // hps, n_q),
        in_specs=in_specs,
        out_specs=pl.BlockSpec((Q_TILE, hps * DIFF_V_DIM), lambda b, h, q: (b * n_q + q, h)),
        scratch_shapes=[pltpu.VMEM((hps, 2, K_TILE, Q_TILE), F32),
                        pltpu.VMEM((hps, 2, 1, Q_TILE), F32),
                        pltpu.VMEM((hps, 2, 1, Q_TILE), F32),
                        pltpu.VMEM((hps, 2, VT_ROWS, Q_TILE), F32)],
        name="diff_attn",
        compiler_params=pltpu.CompilerParams(
            dimension_semantics=("arbitrary", "arbitrary", "arbitrary"),
            vmem_limit_bytes=VMEM_LIMIT),
    )(slope_k, lq1, lk1, lq2, lk2, gain, qt, qt, ka, vt)


def _gla_tile(tril_ref, ind_ref, gain_ref, q_ref, k_ref, v_ref, g_ref, la_ref, o_ref,
              st_ref, b_sc, oi_sc, acc_sc, kf_sc, vf_sc, after_cumsum=None, chunk_ids=None):
    c_len = GLA_CHUNK
    n_chunks = GLA_TILE // c_len
    if chunk_ids is None:
        chunk_ids = tuple(range(n_chunks))
    if 0 in chunk_ids:
        @pl.when(pl.program_id(1) == 0)
        def _():
            st_ref[...] = jnp.zeros(st_ref.shape, F32)

    row = lax.broadcasted_iota(jnp.int32, (c_len, c_len), 0)
    col = lax.broadcasted_iota(jnp.int32, (c_len, c_len), 1)
    causal = col <= row
    nt = (((1,), (1,)), ((), ()))

    def finalize(o, rows, vs):
        g = g_ref[rows, vs].astype(F32)
        return (_rms(o, gain_ref[...]) * (g * jax.nn.sigmoid(g))).astype(BF16)

    cums = {}
    for c in chunk_ids:
        la = la_ref[c * c_len:(c + 1) * c_len, :]
        hi = la.astype(BF16)
        r1 = la - hi.astype(F32)
        mid = r1.astype(BF16)
        lo = (r1 - mid.astype(F32)).astype(BF16)
        tril = tril_ref[...]
        cums[c] = (jnp.dot(tril, hi, preferred_element_type=F32)
                   + jnp.dot(tril, mid, preferred_element_type=F32)
                   + jnp.dot(tril, lo, preferred_element_type=F32))

    if after_cumsum is not None:
        after_cumsum()

    overflow = []
    for c in chunk_ids:
        rows = slice(c * c_len, (c + 1) * c_len)
        b = cums[c]
        b_sc[rows, :] = b
        b_last = b[c_len - 1:c_len, :]
        overflow.append(jnp.max(-b_last) > EXP_CLAMP)
        q = q_ref[rows, :].astype(F32)
        k = k_ref[rows, :].astype(F32)
        q_in = (q * jnp.exp(b)).astype(BF16)
        k_out = (k * jnp.exp(jnp.minimum(-b, EXP_CLAMP))).astype(BF16)
        k_dec = (k * jnp.exp(b_last - b)).astype(BF16)
        d_last = jnp.exp(b_last)
        for h in range(GLA_HEADS):
            ks = slice(h * GLA_K_DIM, (h + 1) * GLA_K_DIM)
            vs = slice(h * GLA_V_DIM, (h + 1) * GLA_V_DIM)
            a = lax.dot_general(q_in[:, ks], k_out[:, ks], nt, preferred_element_type=F32)
            a = jnp.where(causal, a, 0.0).astype(BF16)
            vh = v_ref[rows, vs]
            s_t = st_ref[h]
            o_inter = lax.dot_general(q_in[:, ks], s_t.astype(BF16), nt,
                                      preferred_element_type=F32)
            oi_sc[rows, vs] = o_inter
            o = o_inter + jnp.dot(a, vh, preferred_element_type=F32)
            v_t = vh.astype(F32).T.astype(BF16)
            st_ref[h] = s_t * d_last[:, ks] + jnp.dot(v_t, k_dec[:, ks], preferred_element_type=F32)
            o_ref[rows, vs] = finalize(o, rows, vs)

    def exact_intra(c):
        rows = slice(c * c_len, (c + 1) * c_len)
        q = q_ref[rows, :].astype(F32)
        b = b_sc[rows, :]
        kf_sc[...] = k_ref[rows, :].astype(F32)
        vf_sc[...] = v_ref[rows, :].astype(F32)
        acc_sc[...] = jnp.zeros(acc_sc.shape, F32)
        t_idx = lax.broadcasted_iota(jnp.int32, (c_len, GLA_KW), 0)

        def body(s, carry):
            k_s = kf_sc[pl.ds(s, 1), :]
            b_s = b_sc[pl.ds(c * c_len + s, 1), :]
            v_s = vf_sc[pl.ds(s, 1), :]
            w = q * k_s * jnp.exp(jnp.minimum(b - b_s, 0.0))
            w = jnp.where(t_idx >= s, w, 0.0).astype(BF16)
            a_s = jnp.dot(w, ind_ref[...], preferred_element_type=F32)
            acc_sc[...] += a_s * v_s
            return carry

        lax.fori_loop(0, c_len, body, 0)
        for h in range(GLA_HEADS):
            vs = slice(h * GLA_V_DIM, (h + 1) * GLA_V_DIM)
            o_ref[rows, vs] = finalize(oi_sc[rows, vs] + acc_sc[:, vs], rows, vs)

    return [functools.partial(exact_intra, c) for c in chunk_ids], overflow


def _ffn_kernel(x_ref, od_ref, og_ref, gf_ref, wo_ref, wg_ref, wu_ref, wd_ref, o_ref,
                wo_sc, wg_sc, wu_sc, wd_sc):
    i = pl.program_id(0)

    @pl.when(i < FFN_WARM)
    def _():
        for src, dst in ((wo_ref, wo_sc), (wg_ref, wg_sc), (wu_ref, wu_sc), (wd_ref, wd_sc)):
            rows = src.shape[0]
            dst[pl.ds(pl.multiple_of(i * rows, rows), rows), :] = src[...].astype(BF16)

    @pl.when(i >= FFN_WARM)
    def _():
        h = (x_ref[...]
             + jnp.dot(od_ref[...], wo_sc[:DIFF_WIDTH, :], preferred_element_type=F32)
             + jnp.dot(og_ref[...], wo_sc[DIFF_WIDTH:, :], preferred_element_type=F32))
        m = _rms(h, gf_ref[...]).astype(BF16)
        o_ref[...] = h
        for c0, c1 in FF_CHUNKS:
            gate = jnp.dot(m, wg_sc[:, c0:c1], preferred_element_type=F32)
            up = jnp.dot(m, wu_sc[:, c0:c1], preferred_element_type=F32)
            act = (gate * jax.nn.sigmoid(gate) * up).astype(BF16)
            o_ref[...] += jnp.dot(act, wd_sc[c0:c1, :], preferred_element_type=F32)


def _ffn_call(xf, od, og, gf, wo, wg, wu, wd):
    n = xf.shape[0]
    row = lambda i: (jnp.maximum(i - FFN_WARM, 0), 0)
    chunk = lambda i: (jnp.minimum(i, FFN_WARM - 1), 0)

    def chunk_spec(w):
        assert w.shape[0] % (FFN_WARM * 16) == 0
        return pl.BlockSpec((w.shape[0] // FFN_WARM, w.shape[1]), chunk)

    in_specs = [
        pl.BlockSpec((ROW_TILE, D_MODEL), row),
        pl.BlockSpec((ROW_TILE, DIFF_WIDTH), row),
        pl.BlockSpec((ROW_TILE, GLA_WIDTH), row),
        _const_spec(gf.shape),
        chunk_spec(wo), chunk_spec(wg), chunk_spec(wu), chunk_spec(wd),
    ]
    return pl.pallas_call(
        _ffn_kernel,
        out_shape=jax.ShapeDtypeStruct((n, D_MODEL), F32),
        grid=(FFN_WARM + n // ROW_TILE,),
        in_specs=in_specs,
        out_specs=pl.BlockSpec((ROW_TILE, D_MODEL), row),
        scratch_shapes=[pltpu.VMEM(w.shape, BF16) for w in (wo, wg, wu, wd)],
        name="out_ffn",
        compiler_params=pltpu.CompilerParams(
            dimension_semantics=("arbitrary",), vmem_limit_bytes=VMEM_LIMIT),
    )(xf, od, og, gf, wo, wg, wu, wd)


def _bf16_pieces(value, n=3):
    pieces = []
    rest = np.float32(value)
    for _ in range(n):
        piece = np.float32(np.asarray(rest, dtype=BF16))
        pieces.append(float(piece))
        rest = np.float32(rest - piece)
    return pieces


def _alibi_tables():
    assert ROW_TILE == Q_TILE == K_TILE
    pos = np.arange(ROW_TILE)
    a = (pos // ALIBI_SPLIT) * ALIBI_SPLIT
    c = pos % ALIBI_SPLIT
    extra = LANES - DIFF_HEAD_DIM
    cq = np.zeros((DIFF_HEADS, extra, ROW_TILE), np.float32)
    ck = np.zeros((DIFF_HEADS, extra, ROW_TILE), np.float32)
    for h in range(DIFF_HEADS):
        r = 0
        for piece in _bf16_pieces(_slope(h) * LOG2E):
            for q_row, k_row in ((piece, a), (piece, c), (-a, piece), (-c, piece)):
                cq[h, r, :] = q_row
                ck[h, r, :] = k_row
                r += 1
        assert r <= extra
    return jnp.asarray(cq, dtype=BF16), jnp.asarray(ck)


def _slope(h):
    return 2.0 ** (-8.0 * (h + 1) / DIFF_HEADS)


def kernel(x, attn_norm_gain, w_in, q_norm_gain, k_norm_gain, lambda_q1, lambda_k1, lambda_q2,
           lambda_k2, diff_out_norm_gain, w_gla_gate_up, b_gla_gate, gla_out_norm_gain, w_out,
           ffn_norm_gain, w_ffn_gate, w_ffn_up, w_ffn_down):
    batch, seq, _ = x.shape
    depth = w_in.shape[0]
    assert seq % ROW_TILE == 0 and seq % GLA_TILE == 0

    cq, ck = _alibi_tables()
    tril = jnp.asarray(np.tril(np.ones((GLA_CHUNK, GLA_CHUNK), np.float32)), dtype=BF16)
    ind = jnp.asarray((np.arange(GLA_KW)[:, None] // GLA_K_DIM
                       == np.arange(GLA_WIDTH)[None, :] // GLA_V_DIM).astype(np.float32), dtype=BF16)
    slope_k = jnp.asarray(
        [sum(_bf16_pieces(_slope(h) * LOG2E)) * K_TILE for h in range(DIFF_HEADS)], F32)

    hcur = x.reshape(batch * seq, D_MODEL)
    for l in range(depth):
        lambda_init = 0.8 - 0.6 * math.exp(-0.3 * l)
        w_main = w_in[l].T
        wup = w_gla_gate_up[l].astype(BF16)
        row2 = lambda a: a.reshape(1, -1).astype(F32)
        col2 = lambda a: a.reshape(-1, 1).astype(F32)

        qt, ka, vt, o_gla = _proj_call(
            hcur, row2(attn_norm_gain[l]), w_main, col2(q_norm_gain[l]), col2(k_norm_gain[l]),
            cq, ck, wup, row2(b_gla_gate[l]), tril, ind, row2(gla_out_norm_gain[l]), batch, seq)

        o_diff = _attn_call(slope_k, row2(lambda_q1[l]), row2(lambda_k1[l]), row2(lambda_q2[l]),
                            row2(lambda_k2[l]), row2(diff_out_norm_gain[l]), qt, ka, vt,
                            batch, seq, lambda_init)

        hcur = _ffn_call(hcur, o_diff, o_gla, row2(ffn_norm_gain[l]), w_out[l], w_ffn_gate[l],
                         w_ffn_up[l], w_ffn_down[l])
    return hcur.reshape(batch, seq, D_MODEL)
```

```python
import functools
import math

import jax
import jax.numpy as jnp
import numpy as np
from jax import lax
from jax.experimental import pallas as pl
from jax.experimental.pallas import tpu as pltpu

F32 = jnp.float32
BF16 = jnp.bfloat16

D_MODEL = 1024
DIFF_HEADS = 4
DIFF_HEAD_DIM = 64
DIFF_V_DIM = 128
DIFF_WIDTH = DIFF_HEADS * DIFF_V_DIM
GLA_HEADS = 4
GLA_K_DIM = 64
GLA_V_DIM = 128
GLA_WIDTH = GLA_HEADS * GLA_V_DIM
GLA_KW = GLA_HEADS * GLA_K_DIM
GLA_GATE_RANK = 16
GLA_GATE_TEMP = 16.0
D_FF = 2816
EPS = 1e-6
COL_DQ = 0
COL_DK = COL_DQ + DIFF_HEADS * 2 * DIFF_HEAD_DIM
COL_DV = COL_DK + DIFF_HEADS * 2 * DIFF_HEAD_DIM
COL_GQ = COL_DV + DIFF_WIDTH
COL_GK = COL_GQ + GLA_KW
COL_GV = COL_GK + GLA_KW
COL_GO = COL_GV + GLA_WIDTH
MAIN_WIDTH = COL_GO + GLA_WIDTH
IN_WIDTH = MAIN_WIDTH + GLA_GATE_RANK

LANES = 128
ROW_TILE = 512
Q_TILE = 512
VT_ROWS = 144
K_TILE = 512
LOOP_UNROLL = 2
HEADS_PER_STEP = 4
GLA_TILE = 512
GLA_CHUNK = 256
MXU_DIM = 256
WEIGHT_CAST_COLS = 512
FFN_WARM = 8
FF_CHUNKS = ((0, D_FF),)
ALIBI_SPLIT = 16
NEG_BIG = -1e30
LOG2E = math.log2(math.e)
EXP_CLAMP = 80.0
VMEM_LIMIT = 56 * 1024 * 1024


def _rms(x, gain):
    return x * lax.rsqrt(jnp.mean(x * x, axis=-1, keepdims=True) + EPS) * gain


def _proj_kernel(x_ref, ga_ref, w_ref, gq_ref, gk_ref, cq_ref, ck_ref,
                 wup_ref, bg_ref, tril_ref, ind_ref, ggain_ref,
                 qt_ref, ka_ref, vt_ref, og_ref,
                 wb_sc, gq_out, gk_out, gv_out, go_out, la_out,
                 st_ref, b_sc, oi_sc, acc_sc, kf_sc, vf_sc):
    @pl.when((pl.program_id(0) == 0) & (pl.program_id(1) == 0))
    def _():
        for c0 in range(0, IN_WIDTH, WEIGHT_CAST_COLS):
            c1 = min(c0 + WEIGHT_CAST_COLS, IN_WIDTH)
            wb_sc[:, c0:c1] = w_ref[c0:c1, :].T.astype(BF16)

    nb = _rms(x_ref[...], ga_ref[...]).astype(BF16)

    def proj(c0, c1):
        return jnp.dot(nb, wb_sc[:, c0:c1], preferred_element_type=F32)

    def head_norm_t(x_t, gain_col):
        ms = jnp.mean(x_t * x_t, axis=0, keepdims=True)
        return x_t * lax.rsqrt(ms + EPS) * gain_col

    d = DIFF_HEAD_DIM
    q_scale = d ** -0.5 * LOG2E
    ones_rows = (lax.broadcasted_iota(jnp.int32, (VT_ROWS - DIFF_V_DIM, ROW_TILE), 0) == 0)

    def emit_qkv(dq, dk, dv):
        for h in range(DIFF_HEADS):
            q_t = dq[:, h * LANES:(h + 1) * LANES].T
            k_t = dk[:, h * LANES:(h + 1) * LANES].T
            for j in range(2):
                yq = head_norm_t(q_t[j * d:(j + 1) * d], gq_ref[...]) * q_scale
                yk = head_norm_t(k_t[j * d:(j + 1) * d], gk_ref[...])
                qt_ref[0, h, j, 0, :d, :] = yq.astype(BF16)
                qt_ref[0, h, j, 0, d:, :] = cq_ref[h]
                ka_ref[0, h, j] = jnp.concatenate([yk, ck_ref[h]], axis=0).T.astype(BF16)
            vt_ref[0, h, 0, :DIFF_V_DIM, :] = dv[:, h * LANES:(h + 1) * LANES].T.astype(BF16)
            vt_ref[0, h, 0, DIFF_V_DIM:, :] = ones_rows.astype(BF16)

    gla_cols = proj(COL_GQ, IN_WIDTH)
    gq_out[...] = (gla_cols[:, :GLA_KW] * (GLA_K_DIM ** -0.5)).astype(BF16)
    gk_out[...] = gla_cols[:, COL_GK - COL_GQ:COL_GV - COL_GQ].astype(BF16)
    gv_out[...] = gla_cols[:, COL_GV - COL_GQ:COL_GO - COL_GQ].astype(BF16)
    go_out[...] = gla_cols[:, COL_GO - COL_GQ:MAIN_WIDTH - COL_GQ].astype(BF16)
    glr = gla_cols[:, MAIN_WIDTH - COL_GQ:]
    dq = proj(COL_DQ, COL_DK)
    z = jnp.dot(glr.astype(BF16), wup_ref[...], preferred_element_type=F32) + bg_ref[...]
    log_sig = jnp.minimum(z, 0.0) - jnp.log(1.0 + jnp.exp(-jnp.abs(z)))
    la_out[...] = log_sig * (1.0 / GLA_GATE_TEMP)

    kv = []

    def project_kv():
        kv.extend([proj(COL_DK, COL_DV), proj(COL_DV, COL_GQ)])

    redo, overflow = _gla_tile(tril_ref, ind_ref, ggain_ref, gq_out, gk_out, gv_out, go_out, la_out,
                               og_ref, st_ref, b_sc, oi_sc, acc_sc, kf_sc, vf_sc,
                               after_cumsum=project_kv)
    emit_qkv(dq, *kv)
    for fn, flag in zip(redo, overflow):
        pl.when(flag)(fn)


def _const_spec(shape):
    nd = len(shape)
    return pl.BlockSpec(shape, lambda *_: (0,) * nd, pipeline_mode=pl.Buffered(1))


def _proj_call(xf, ga, w_main, gq, gk, cq, ck, wup, bg, tril, ind, ggain, batch, seq):
    assert ROW_TILE == GLA_TILE
    n_t = seq // ROW_TILE
    n = batch * seq
    row = lambda b, t: (b * n_t + t, 0)
    out_shape = (
        jax.ShapeDtypeStruct((batch, DIFF_HEADS, 2, seq // Q_TILE, LANES, Q_TILE), BF16),
        jax.ShapeDtypeStruct((batch, DIFF_HEADS, 2, seq, LANES), BF16),
        jax.ShapeDtypeStruct((batch, DIFF_HEADS, seq // K_TILE, VT_ROWS, K_TILE), BF16),
        jax.ShapeDtypeStruct((n, GLA_WIDTH), BF16),
    )
    out_specs = (
        pl.BlockSpec((1, DIFF_HEADS, 2, 1, LANES, Q_TILE), lambda b, t: (b, 0, 0, t, 0, 0)),
        pl.BlockSpec((1, DIFF_HEADS, 2, ROW_TILE, LANES), lambda b, t: (b, 0, 0, t, 0)),
        pl.BlockSpec((1, DIFF_HEADS, 1, VT_ROWS, K_TILE), lambda b, t: (b, 0, t, 0, 0)),
        pl.BlockSpec((ROW_TILE, GLA_WIDTH), row),
    )
    consts = (ga, w_main, gq, gk, cq, ck, wup, bg, tril, ind, ggain)
    in_specs = [pl.BlockSpec((ROW_TILE, D_MODEL), row)] + [_const_spec(a.shape) for a in consts]
    scratch_shapes = [
        pltpu.VMEM((D_MODEL, IN_WIDTH), BF16),
        pltpu.VMEM((ROW_TILE, GLA_KW), BF16),
        pltpu.VMEM((ROW_TILE, GLA_KW), BF16),
        pltpu.VMEM((ROW_TILE, GLA_WIDTH), BF16),
        pltpu.VMEM((ROW_TILE, GLA_WIDTH), BF16),
        pltpu.VMEM((ROW_TILE, GLA_KW), F32),
        pltpu.VMEM((GLA_HEADS, GLA_V_DIM, GLA_K_DIM), F32),
        pltpu.VMEM((GLA_TILE, GLA_KW), F32),
        pltpu.VMEM((GLA_TILE, GLA_WIDTH), F32),
        pltpu.VMEM((GLA_CHUNK, GLA_WIDTH), F32),
        pltpu.VMEM((GLA_CHUNK, GLA_KW), F32),
        pltpu.VMEM((GLA_CHUNK, GLA_WIDTH), F32),
    ]
    return pl.pallas_call(
        _proj_kernel, out_shape=out_shape, grid=(batch, n_t),
        in_specs=in_specs, out_specs=out_specs, name="proj_gla",
        scratch_shapes=scratch_shapes,
        compiler_params=pltpu.CompilerParams(
            dimension_semantics=("arbitrary", "arbitrary"), vmem_limit_bytes=VMEM_LIMIT),
    )(xf, *consts)


def _attn_kernel(slope_ref, lq1_ref, lk1_ref, lq2_ref, lk2_ref, gain_ref, qt_ref, qn_ref, k_ref,
                 vt_ref, o_ref, s_sc, cm_sc, m_sc, acc_sc, *, lambda_init):
    hg = pl.program_id(1)
    qi = pl.program_id(2)
    m_sc[...] = jnp.full(m_sc.shape, NEG_BIG, F32)
    acc_sc[...] = jnp.zeros(acc_sc.shape, F32)
    group = range(HEADS_PER_STEP)

    half = K_TILE // 2

    def score1(g, j, kb, q_src=qt_ref):
        rows = pl.ds(pl.multiple_of(kb * K_TILE, K_TILE), K_TILE)
        s = jnp.dot(k_ref[0, g, j, rows, :], q_src[0, g, j, 0], preferred_element_type=F32)
        s_sc[g, j] = s
        cm_sc[g, j] = jnp.max(s, axis=0, keepdims=True)

    def absorb1(g, j, kb):
        off = slope_ref[hg * HEADS_PER_STEP + g] * (kb - qi).astype(F32)
        m_prev = m_sc[g, j]
        m_new = jnp.maximum(m_prev, cm_sc[g, j] + off)
        p = jnp.exp2(s_sc[g, j] - (m_new - off))
        alpha = jnp.exp2(m_prev - m_new)
        acc_sc[g, j] = alpha * acc_sc[g, j] + jnp.dot(vt_ref[0, g, kb], p.astype(BF16),
                                                      preferred_element_type=F32)
        m_sc[g, j] = m_new

    def score_diag1(g, j):
        r0 = pl.multiple_of(qi * K_TILE, K_TILE)
        q_t = qt_ref[0, g, j, 0]
        s_a = jnp.dot(k_ref[0, g, j, pl.ds(r0, half), :], q_t, preferred_element_type=F32)
        s_b = jnp.dot(k_ref[0, g, j, pl.ds(r0 + half, half), :], q_t[:, half:],
                      preferred_element_type=F32)
        s_a = jnp.where(lax.broadcasted_iota(jnp.int32, s_a.shape, 0)
                        <= lax.broadcasted_iota(jnp.int32, s_a.shape, 1), s_a, NEG_BIG)
        s_b = jnp.where(lax.broadcasted_iota(jnp.int32, s_b.shape, 0)
                        <= lax.broadcasted_iota(jnp.int32, s_b.shape, 1), s_b, NEG_BIG)
        s_sc[g, j, :half, :] = s_a
        s_sc[g, j, half:, half:] = s_b
        cm_sc[g, j] = jnp.max(s_a, axis=0, keepdims=True)
        cm_sc[g, j, :, half:] = jnp.maximum(cm_sc[g, j, :, half:],
                                            jnp.max(s_b, axis=0, keepdims=True))

    def absorb_diag1(g, j):
        m_prev = m_sc[g, j]
        m_new = jnp.maximum(m_prev, cm_sc[g, j])
        m_sc[g, j] = m_new
        p_a = jnp.exp2(s_sc[g, j, :half, :] - m_new)
        p_b = jnp.exp2(s_sc[g, j, half:, half:] - m_sc[g, j, :, half:])
        alpha = jnp.exp2(m_prev - m_new)
        pv_a = jnp.dot(vt_ref[0, g, qi, :, :half], p_a.astype(BF16), preferred_element_type=F32)
        pv_b = jnp.dot(vt_ref[0, g, qi, :, half:], p_b.astype(BF16), preferred_element_type=F32)
        acc_sc[g, j] = alpha * acc_sc[g, j] + pv_a
        acc_sc[g, j, :, half:] += pv_b

    def score(j, kb):
        for g in group:
            score1(g, j, kb)

    def absorb(j, kb):
        for g in group:
            absorb1(g, j, kb)

    def score_diag(j):
        for g in group:
            score_diag1(g, j)

    def absorb_diag(j):
        for g in group:
            absorb_diag1(g, j)

    def unit(kb):
        for g in group:
            score1(g, 1, kb)
            absorb1(g, 0, kb)
        for g in group:
            score1(g, 0, kb + 1)
            absorb1(g, 1, kb)

    def body(kp, carry):
        for u in range(LOOP_UNROLL):
            unit(LOOP_UNROLL * kp + u)
        return carry

    n_full = jnp.maximum(qi - 1, 0)
    lax.fori_loop(0, n_full // LOOP_UNROLL, body, 0)

    done = (n_full // LOOP_UNROLL) * LOOP_UNROLL
    size = LOOP_UNROLL // 2
    while size >= 1:
        take = (n_full - done) >= size

        @pl.when(take)
        def _(done=done, size=size):
            for u in range(size):
                unit(done + u)

        done = done + jnp.where(take, size, 0)
        size //= 2

    def finish():
        score_diag(1)
        absorb_diag(0)
        absorb_diag(1)
        for g in group:
            score1(g, 0, 0, q_src=qn_ref)
        lam = (jnp.exp(jnp.sum(lq1_ref[...] * lk1_ref[...], axis=1, keepdims=True))
               - jnp.exp(jnp.sum(lq2_ref[...] * lk2_ref[...], axis=1, keepdims=True))
               + lambda_init)
        for g in group:
            a1 = acc_sc[g, 0]
            a2 = acc_sc[g, 1]
            o_t = (a1[:DIFF_V_DIM] / a1[DIFF_V_DIM:DIFF_V_DIM + 1]
                   - lam * (a2[:DIFF_V_DIM] / a2[DIFF_V_DIM:DIFF_V_DIM + 1]))
            o_ref[:, g * DIFF_V_DIM:(g + 1) * DIFF_V_DIM] = (
                _rms(o_t.T, gain_ref[...]) * (1.0 - lambda_init)).astype(BF16)

    @pl.when(qi > 0)
    def _():
        for g in group:
            score1(g, 1, qi - 1)
            absorb1(g, 0, qi - 1)
        for g in group:
            score_diag1(g, 0)
            absorb1(g, 1, qi - 1)
        finish()

    @pl.when(qi == 0)
    def _():
        score_diag(0)
        finish()


def _attn_call(slope_k, lq1, lk1, lq2, lk2, gain, qt, ka, vt, batch, seq, lambda_init):
    n_q = seq // Q_TILE
    n_k = seq // K_TILE
    hps = HEADS_PER_STEP
    smem = pl.BlockSpec(memory_space=pltpu.SMEM)
    in_specs = [
        smem,
        _const_spec(lq1.shape), _const_spec(lk1.shape), _const_spec(lq2.shape),
        _const_spec(lk2.shape), _const_spec(gain.shape),
        pl.BlockSpec((1, hps, 2, 1, LANES, Q_TILE), lambda b, h, q: (b, h, 0, q, 0, 0)),
        pl.BlockSpec((1, hps, 2, 1, LANES, Q_TILE),
                     lambda b, h, q: (b, h, 0, jnp.minimum(q + 1, n_q - 1), 0, 0)),
        pl.BlockSpec((1, hps, 2, seq, LANES), lambda b, h, q: (b, h, 0, 0, 0),
                     pipeline_mode=pl.Buffered(1)),
        pl.BlockSpec((1, hps, n_k, VT_ROWS, K_TILE), lambda b, h, q: (b, h, 0, 0, 0),
                     pipeline_mode=pl.Buffered(1)),
    ]
    return pl.pallas_call(
        functools.partial(_attn_kernel, lambda_init=lambda_init),
        out_shape=jax.ShapeDtypeStruct((batch * seq, DIFF_WIDTH), BF16),
        grid=(batch, DIFF_HEADS // hps, n_q),
        in_specs=in_specs,
        out_specs=pl.BlockSpec((Q_TILE, hps * DIFF_V_DIM), lambda b, h, q: (b * n_q + q, h)),
        scratch_shapes=[pltpu.VMEM((hps, 2, K_TILE, Q_TILE), F32),
                        pltpu.VMEM((hps, 2, 1, Q_TILE), F32),
                        pltpu.VMEM((hps, 2, 1, Q_TILE), F32),
                        pltpu.VMEM((hps, 2, VT_ROWS, Q_TILE), F32)],
        name="diff_attn",
        compiler_params=pltpu.CompilerParams(
            dimension_semantics=("arbitrary", "arbitrary", "arbitrary"),
            vmem_limit_bytes=VMEM_LIMIT),
    )(slope_k, lq1, lk1, lq2, lk2, gain, qt, qt, ka, vt)


def _gla_tile(tril_ref, ind_ref, gain_ref, q_ref, k_ref, v_ref, g_ref, la_ref, o_ref,
              st_ref, b_sc, oi_sc, acc_sc, kf_sc, vf_sc, after_cumsum=None):
    @pl.when(pl.program_id(1) == 0)
    def _():
        st_ref[...] = jnp.zeros(st_ref.shape, F32)

    c_len = GLA_CHUNK
    row = lax.broadcasted_iota(jnp.int32, (c_len, c_len), 0)
    col = lax.broadcasted_iota(jnp.int32, (c_len, c_len), 1)
    causal = col <= row
    nt = (((1,), (1,)), ((), ()))

    def finalize(o, rows, vs):
        g = g_ref[rows, vs].astype(F32)
        return (_rms(o, gain_ref[...]) * (g * jax.nn.sigmoid(g))).astype(BF16)

    n_chunks = GLA_TILE // c_len
    cums = []
    for c in range(n_chunks):
        la = la_ref[c * c_len:(c + 1) * c_len, :]
        hi = la.astype(BF16)
        r1 = la - hi.astype(F32)
        mid = r1.astype(BF16)
        lo = (r1 - mid.astype(F32)).astype(BF16)
        tril = tril_ref[...]
        cums.append(jnp.dot(tril, hi, preferred_element_type=F32)
                    + jnp.dot(tril, mid, preferred_element_type=F32)
                    + jnp.dot(tril, lo, preferred_element_type=F32))

    if after_cumsum is not None:
        after_cumsum()

    overflow = []
    for c in range(n_chunks):
        rows = slice(c * c_len, (c + 1) * c_len)
        b = cums[c]
        b_sc[rows, :] = b
        b_last = b[c_len - 1:c_len, :]
        overflow.append(jnp.max(-b_last) > EXP_CLAMP)
        q = q_ref[rows, :].astype(F32)
        k = k_ref[rows, :].astype(F32)
        q_in = (q * jnp.exp(b)).astype(BF16)
        k_out = (k * jnp.exp(jnp.minimum(-b, EXP_CLAMP))).astype(BF16)
        k_dec = (k * jnp.exp(b_last - b)).astype(BF16)
        d_last = jnp.exp(b_last)
        for h in range(GLA_HEADS):
            ks = slice(h * GLA_K_DIM, (h + 1) * GLA_K_DIM)
            vs = slice(h * GLA_V_DIM, (h + 1) * GLA_V_DIM)
            a = lax.dot_general(q_in[:, ks], k_out[:, ks], nt, preferred_element_type=F32)
            a = jnp.where(causal, a, 0.0).astype(BF16)
            vh = v_ref[rows, vs]
            s_t = st_ref[h]
            o_inter = lax.dot_general(q_in[:, ks], s_t.astype(BF16), nt,
                                      preferred_element_type=F32)
            oi_sc[rows, vs] = o_inter
            o = o_inter + jnp.dot(a, vh, preferred_element_type=F32)
            v_t = vh.astype(F32).T.astype(BF16)
            st_ref[h] = s_t * d_last[:, ks] + jnp.dot(v_t, k_dec[:, ks], preferred_element_type=F32)
            o_ref[rows, vs] = finalize(o, rows, vs)

    def exact_intra(c):
        rows = slice(c * c_len, (c + 1) * c_len)
        q = q_ref[rows, :].astype(F32)
        b = b_sc[rows, :]
        kf_sc[...] = k_ref[rows, :].astype(F32)
        vf_sc[...] = v_ref[rows, :].astype(F32)
        acc_sc[...] = jnp.zeros(acc_sc.shape, F32)
        t_idx = lax.broadcasted_iota(jnp.int32, (c_len, GLA_KW), 0)

        def body(s, carry):
            k_s = kf_sc[pl.ds(s, 1), :]
            b_s = b_sc[pl.ds(c * c_len + s, 1), :]
            v_s = vf_sc[pl.ds(s, 1), :]
            w = q * k_s * jnp.exp(jnp.minimum(b - b_s, 0.0))
            w = jnp.where(t_idx >= s, w, 0.0).astype(BF16)
            a_s = jnp.dot(w, ind_ref[...], preferred_element_type=F32)
            acc_sc[...] += a_s * v_s
            return carry

        lax.fori_loop(0, c_len, body, 0)
        for h in range(GLA_HEADS):
            vs = slice(h * GLA_V_DIM, (h + 1) * GLA_V_DIM)
            o_ref[rows, vs] = finalize(oi_sc[rows, vs] + acc_sc[:, vs], rows, vs)

    return [functools.partial(exact_intra, c) for c in range(n_chunks)], overflow


def _ffn_kernel(x_ref, od_ref, og_ref, gf_ref, wo_ref, wg_ref, wu_ref, wd_ref, o_ref,
                wo_sc, wg_sc, wu_sc, wd_sc):
    i = pl.program_id(0)

    @pl.when(i < FFN_WARM)
    def _():
        for src, dst in ((wo_ref, wo_sc), (wg_ref, wg_sc), (wu_ref, wu_sc), (wd_ref, wd_sc)):
            rows = src.shape[0]
            dst[pl.ds(pl.multiple_of(i * rows, rows), rows), :] = src[...].astype(BF16)

    @pl.when(i >= FFN_WARM)
    def _():
        h = (x_ref[...]
             + jnp.dot(od_ref[...], wo_sc[:DIFF_WIDTH, :], preferred_element_type=F32)
             + jnp.dot(og_ref[...], wo_sc[DIFF_WIDTH:, :], preferred_element_type=F32))
        m = _rms(h, gf_ref[...]).astype(BF16)
        o_ref[...] = h
        for c0, c1 in FF_CHUNKS:
            gate = jnp.dot(m, wg_sc[:, c0:c1], preferred_element_type=F32)
            up = jnp.dot(m, wu_sc[:, c0:c1], preferred_element_type=F32)
            act = (gate * jax.nn.sigmoid(gate) * up).astype(BF16)
            o_ref[...] += jnp.dot(act, wd_sc[c0:c1, :], preferred_element_type=F32)


def _ffn_call(xf, od, og, gf, wo, wg, wu, wd):
    n = xf.shape[0]
    row = lambda i: (jnp.maximum(i - FFN_WARM, 0), 0)
    chunk = lambda i: (jnp.minimum(i, FFN_WARM - 1), 0)

    def chunk_spec(w):
        assert w.shape[0] % (FFN_WARM * 16) == 0
        return pl.BlockSpec((w.shape[0] // FFN_WARM, w.shape[1]), chunk)

    in_specs = [
        pl.BlockSpec((ROW_TILE, D_MODEL), row),
        pl.BlockSpec((ROW_TILE, DIFF_WIDTH), row),
        pl.BlockSpec((ROW_TILE, GLA_WIDTH), row),
        _const_spec(gf.shape),
        chunk_spec(wo), chunk_spec(wg), chunk_spec(wu), chunk_spec(wd),
    ]
    return pl.pallas_call(
        _ffn_kernel,
        out_shape=jax.ShapeDtypeStruct((n, D_MODEL), F32),
        grid=(FFN_WARM + n // ROW_TILE,),
        in_specs=in_specs,
        out_specs=pl.BlockSpec((ROW_TILE, D_MODEL), row),
        scratch_shapes=[pltpu.VMEM(w.shape, BF16) for w in (wo, wg, wu, wd)],
        name="out_ffn",
        compiler_params=pltpu.CompilerParams(
            dimension_semantics=("arbitrary",), vmem_limit_bytes=VMEM_LIMIT),
    )(xf, od, og, gf, wo, wg, wu, wd)


def _bf16_pieces(value, n=3):
    pieces = []
    rest = np.float32(value)
    for _ in range(n):
        piece = np.float32(np.asarray(rest, dtype=BF16))
        pieces.append(float(piece))
        rest = np.float32(rest - piece)
    return pieces


def _alibi_tables():
    assert ROW_TILE == Q_TILE == K_TILE
    pos = np.arange(ROW_TILE)
    a = (pos // ALIBI_SPLIT) * ALIBI_SPLIT
    c = pos % ALIBI_SPLIT
    extra = LANES - DIFF_HEAD_DIM
    cq = np.zeros((DIFF_HEADS, extra, ROW_TILE), np.float32)
    ck = np.zeros((DIFF_HEADS, extra, ROW_TILE), np.float32)
    for h in range(DIFF_HEADS):
        r = 0
        for piece in _bf16_pieces(_slope(h) * LOG2E):
            for q_row, k_row in ((piece, a), (piece, c), (-a, piece), (-c, piece)):
                cq[h, r, :] = q_row
                ck[h, r, :] = k_row
                r += 1
        assert r <= extra
    return jnp.asarray(cq, dtype=BF16), jnp.asarray(ck)


def _slope(h):
    return 2.0 ** (-8.0 * (h + 1) / DIFF_HEADS)


def kernel(x, attn_norm_gain, w_in, q_norm_gain, k_norm_gain, lambda_q1, lambda_k1, lambda_q2,
           lambda_k2, diff_out_norm_gain, w_gla_gate_up, b_gla_gate, gla_out_norm_gain, w_out,
           ffn_norm_gain, w_ffn_gate, w_ffn_up, w_ffn_down):
    batch, seq, _ = x.shape
    depth = w_in.shape[0]
    assert seq % ROW_TILE == 0 and seq % GLA_TILE == 0

    cq, ck = _alibi_tables()
    tril = jnp.asarray(np.tril(np.ones((GLA_CHUNK, GLA_CHUNK), np.float32)), dtype=BF16)
    ind = jnp.asarray((np.arange(GLA_KW)[:, None] // GLA_K_DIM
                       == np.arange(GLA_WIDTH)[None, :] // GLA_V_DIM).astype(np.float32), dtype=BF16)
    slope_k = jnp.asarray(
        [sum(_bf16_pieces(_slope(h) * LOG2E)) * K_TILE for h in range(DIFF_HEADS)], F32)

    hcur = x.reshape(batch * seq, D_MODEL)
    for l in range(depth):
        lambda_init = 0.8 - 0.6 * math.exp(-0.3 * l)
        w_main = w_in[l].T
        wup = w_gla_gate_up[l].astype(BF16)
        row2 = lambda a: a.reshape(1, -1).astype(F32)
        col2 = lambda a: a.reshape(-1, 1).astype(F32)

        qt, ka, vt, o_gla = _proj_call(
            hcur, row2(attn_norm_gain[l]), w_main, col2(q_norm_gain[l]), col2(k_norm_gain[l]),
            cq, ck, wup, row2(b_gla_gate[l]), tril, ind, row2(gla_out_norm_gain[l]), batch, seq)

        o_diff = _attn_call(slope_k, row2(lambda_q1[l]), row2(lambda_k1[l]), row2(lambda_q2[l]),
                            row2(lambda_k2[l]), row2(diff_out_norm_gain[l]), qt, ka, vt,
                            batch, seq, lambda_init)

        hcur = _ffn_call(hcur, o_diff, o_gla, row2(ffn_norm_gain[l]), w_out[l], w_ffn_gate[l],
                         w_ffn_up[l], w_ffn_down[l])
    return hcur.reshape(batch, seq, D_MODEL)
```
